```python
import math
import numpy as np
import jax
import jax.numpy as jnp
from jax import lax

D_MODEL = 1024
BATCH = 16
SEQ = 256
DEPTH = 2
DEC_BATCH = 8
DEC_SEQ = 1024
PAST_LEN = 512

GRID_W = 64
N_BRANCH = 4
H_A = 4
DK_A = 128
DV_A = 128
CONV_W = 5
CHUNK_A = 64
H_B = 4
DK_B = 64
DV_B = 128
H_C = 4
DK_C = 128
DV_C = 128
CHUNK_C = 16
H_D = 8
DH_D = 64
WIN_R = 8
WIN_C = 16
BRANCH_W = H_A * DV_A
GDN_CONV_CH = 2 * H_A * DK_A + H_A * DV_A
Q_BLOCK = 128
ROPE_BASE = 10000.0
D_FF = ((8 * D_MODEL + 3 * 256 - 1) // (3 * 256)) * 256
EPS = 1e-6
MASK_VALUE = -1e30
F_FLOOR = 1e-30
F32 = jnp.float32
IN_SIZES = (H_A * DK_A, H_A * DK_A, H_A * DV_A, H_A * DV_A, 2 * H_A, 2 * H_A,
            H_B * 2 * DK_B, H_B * 2 * DK_B, H_B * DV_B,
            H_C * DK_C, 2 * H_C * DK_C, H_C * DV_C, H_C * DV_C,
            H_D * DH_D, H_D * DH_D, H_D * DH_D,
            N_BRANCH * D_MODEL)
D_IN = sum(IN_SIZES)

kernel_name = 'hybrid_flow_gdn_diff_hgrn2_na_step'


def _rmsnorm(x, g):
    xf = x.astype(F32)
    y = xf * lax.rsqrt(jnp.mean(xf * xf, axis=-1, keepdims=True) + EPS)
    return (y * g.astype(F32)).astype(x.dtype)


def _l2norm(x):
    return x * lax.rsqrt(jnp.sum(x * x, axis=-1, keepdims=True) + EPS)


def _heads(x, n):
    b, t, _ = x.shape
    return x.reshape(b, t, n, -1).transpose(0, 2, 1, 3)


def _merge_heads(o):
    b, h, t, d = o.shape
    return o.transpose(0, 2, 1, 3).reshape(b, t, h * d)


def _flip(a):
    return jnp.flip(a, axis=2)


def _dwconv_centred(x, w):
    ch, kw = w.shape
    rhs = w.T[:, None, :].astype(x.dtype)
    return lax.conv_general_dilated(x, rhs, window_strides=(1,), padding=[(kw // 2, kw // 2)],
                                    dimension_numbers=('NWC', 'WIO', 'NWC'), feature_group_count=ch)


def _axial_rope(n_tok, dim):
    t = jnp.arange(n_tok)
    n_freq = dim // 4
    inv = ROPE_BASE ** (-jnp.arange(n_freq, dtype=F32) / n_freq)
    ang = jnp.concatenate([(t // GRID_W).astype(F32)[:, None] * inv,
                           (t % GRID_W).astype(F32)[:, None] * inv], axis=-1)
    return jnp.cos(ang), jnp.sin(ang)


def _apply_rope(x, cos, sin):
    xf = x.astype(F32)
    half = xf.shape[-1] // 2
    x1, x2 = xf[..., :half], xf[..., half:]
    return jnp.concatenate([x1 * cos - x2 * sin, x2 * cos + x1 * sin], axis=-1).astype(x.dtype)


def _block_map(fn, q):
    b, h, t = q.shape[:3]
    nb = t // Q_BLOCK
    qb = jnp.moveaxis(q.reshape((b, h, nb, Q_BLOCK) + q.shape[3:]), 2, 0)
    o = jnp.moveaxis(lax.map(fn, qb), 0, 2)
    return o.reshape((b, h, t) + o.shape[4:])


def _softmax_attend(q, k, v):
    scale = q.shape[-1] ** -0.5

    def blk(qi):
        p = jax.nn.softmax(jnp.einsum('bhqd,bhkd->bhqk', qi, k).astype(F32) * scale, axis=-1)
        return jnp.einsum('bhqk,bhkd->bhqd', p, v.astype(F32)).astype(q.dtype)

    return _block_map(blk, q)


def _diff_attend(q, k, v, lam):
    scale = q.shape[-1] ** -0.5

    def blk(qi):
        p = jax.nn.softmax(jnp.einsum('bhqmd,bhkmd->bhmqk', qi, k).astype(F32) * scale, axis=-1)
        pd = p[:, :, 0] - lam * p[:, :, 1]
        return jnp.einsum('bhqk,bhkd->bhqd', pd, v.astype(F32)).astype(q.dtype)

    return _block_map(blk, q)


def _gdn_chunked(q, k, v, g, beta, s0):
    b, h, t, dk = q.shape
    dv = v.shape[-1]
    n = t // CHUNK_A
    q = q.reshape(b, h, n, CHUNK_A, dk)
    k = k.reshape(b, h, n, CHUNK_A, dk)
    v = v.reshape(b, h, n, CHUNK_A, dv)
    beta = beta.reshape(b, h, n, CHUNK_A, 1)
    gc = jnp.cumsum(g.reshape(b, h, n, CHUNK_A), axis=-1)
    tril = jnp.tril(jnp.ones((CHUNK_A, CHUNK_A), bool))
    strict = jnp.tril(jnp.ones((CHUNK_A, CHUNK_A), bool), -1)
    diff = gc[..., :, None] - gc[..., None, :]
    decay = jnp.where(tril, jnp.exp(jnp.where(tril, diff, 0.0)), 0.0)
    kb = k * beta
    lmat = jnp.where(strict, jnp.einsum('bhncd,bhnsd->bhncs', kb, k) * decay, 0.0)
    eye = jnp.eye(CHUNK_A, dtype=F32)
    rhs = jnp.concatenate([v * beta, kb * jnp.exp(gc)[..., None]], axis=-1)
    sol = lax.linalg.triangular_solve(eye + lmat, rhs, left_side=True, lower=True)
    u, w = sol[..., :dv], sol[..., dv:]
    a_intra = jnp.where(tril, jnp.einsum('bhncd,bhnsd->bhncs', q, k) * decay, 0.0)
    q_dec = q * jnp.exp(gc)[..., None]
    k_dec = k * jnp.exp(gc[..., -1:] - gc)[..., None]
    g_last = jnp.exp(gc[..., -1])

    def step(s, xs):
        qd, kd, ui, wi, ai, gl = xs
        v_new = ui - jnp.einsum('bhcd,bhde->bhce', wi, s)
        o = jnp.einsum('bhcd,bhde->bhce', qd, s) + jnp.einsum('bhcs,bhse->bhce', ai, v_new)
        s = s * gl[..., None, None] + jnp.einsum('bhcd,bhce->bhde', kd, v_new)
        return s, o

    xs = tuple(jnp.moveaxis(a, 2, 0) for a in (q_dec, k_dec, u, w, a_intra, g_last))
    s_fin, o = lax.scan(step, s0, xs)
    return jnp.moveaxis(o, 0, 2).reshape(b, h, t, dv), s_fin


def _hgrn_chunked(q, k, v, logf, s0):
    b, h, t, dk = q.shape
    dv = v.shape[-1]
    n = t // CHUNK_C
    q = q.reshape(b, h, n, CHUNK_C, dk)
    k = k.reshape(b, h, n, CHUNK_C, dk)
    v = v.reshape(b, h, n, CHUNK_C, dv)
    bc = jnp.cumsum(logf.reshape(b, h, n, CHUNK_C, dk), axis=3)
    tril = jnp.tril(jnp.ones((CHUNK_C, CHUNK_C), bool))[:, :, None]
    diff = bc[:, :, :, :, None, :] - bc[:, :, :, None, :, :]
    dmat = jnp.where(tril, jnp.exp(jnp.where(tril, diff, 0.0)), 0.0)
    a_intra = jnp.einsum('bhntd,bhnsd,bhntsd->bhnts', q, k, dmat)
    o_intra = jnp.einsum('bhnts,bhnse->bhnte', a_intra, v)
    q_dec = q * jnp.exp(bc)
    k_dec = k * jnp.exp(bc[:, :, :, -1:, :] - bc)
    f_last = jnp.exp(bc[:, :, :, -1, :])

    def step(s, xs):
        qd, kd, vi, fl = xs
        o = jnp.einsum('bhcd,bhde->bhce', qd, s)
        s = s * fl[..., None] + jnp.einsum('bhcd,bhce->bhde', kd, vi)
        return s, o

    xs = tuple(jnp.moveaxis(a, 2, 0) for a in (q_dec, k_dec, v, f_last))
    s_fin, o_inter = lax.scan(step, s0, xs)
    o = o_intra + jnp.moveaxis(o_inter, 0, 2)
    return o.reshape(b, h, t, dv), s_fin


def _gdn_mixer(q, k, v, z, a, bgate, conv_w, a_log, dt_bias, norm_g, s0):
    bsz, t, _ = q.shape
    qkv = jax.nn.silu(_dwconv_centred(jnp.concatenate([q, k, v], axis=-1), conv_w)).astype(F32)
    q, k, v = jnp.split(qkv, [H_A * DK_A, 2 * H_A * DK_A], axis=-1)
    q = _l2norm(_heads(q, H_A)) * DK_A ** -0.5
    k = _l2norm(_heads(k, H_A))
    v = _heads(v, H_A)
    g = -jnp.exp(a_log.astype(F32)) * jax.nn.softplus(a.astype(F32).reshape(bsz, t, 2, H_A) + dt_bias.astype(F32))
    beta = jax.nn.sigmoid(bgate.astype(F32).reshape(bsz, t, 2, H_A))
    g = g.transpose(2, 0, 3, 1)
    beta = beta.transpose(2, 0, 3, 1)
    s0 = s0.astype(F32)
    o_f, s_f = _gdn_chunked(q, k, v, g[0], beta[0], s0[:, 0])
    o_b, s_b = _gdn_chunked(_flip(q), _flip(k), _flip(v), _flip(g[1]), _flip(beta[1]), s0[:, 1])
    o = (o_f + _flip(o_b)).transpose(0, 2, 1, 3)
    o = _rmsnorm(o, norm_g) * jax.nn.silu(z.astype(F32).reshape(bsz, t, H_A, DV_A))
    return o.reshape(bsz, t, H_A * DV_A), jnp.stack([s_f, s_b], axis=1)


def _hgrn_mixer(q, f, i, gate, lb, norm_g, s0):
    bsz, t, _ = q.shape
    q = _heads(jax.nn.silu(q.astype(F32)), H_C)
    v = _heads(i.astype(F32), H_C)
    lb = lb.astype(F32)
    fgate = lb + (1.0 - lb) * jax.nn.sigmoid(f.astype(F32).reshape(bsz, t, 2, H_C * DK_C))
    logf = jnp.log(jnp.maximum(fgate, F_FLOOR))
    logf = logf.reshape(bsz, t, 2, H_C, DK_C).transpose(2, 0, 3, 1, 4)
    k = -jnp.expm1(logf)
    s0 = s0.astype(F32)
    o_f, s_f = _hgrn_chunked(q, k[0], v, logf[0], s0[:, 0])
    o_b, s_b = _hgrn_chunked(_flip(q), _flip(k[1]), _flip(v), _flip(logf[1]), s0[:, 1])
    o = (o_f + _flip(o_b)).transpose(0, 2, 1, 3)
    o = _rmsnorm(o, norm_g) * jax.nn.silu(gate.astype(F32).reshape(bsz, t, H_C, DV_C))
    return o.reshape(bsz, t, H_C * DV_C), jnp.stack([s_f, s_b], axis=1)


def _na_latent(q, k, v, ck, cv, rpb):
    b, h, t, d = q.shape
    rows = t // GRID_W
    wr = min(WIN_R, rows)
    n_ctx = ck.shape[2]
    scale = d ** -0.5
    col = jnp.arange(GRID_W)
    c0 = jnp.clip(col - WIN_C // 2, 0, GRID_W - WIN_C)
    col_ok = (col[None, :] >= c0[:, None]) & (col[None, :] < c0[:, None] + WIN_C)
    dc = jnp.clip(col[None, :] - col[:, None] + WIN_C - 1, 0, 2 * WIN_C - 2)
    mask = jnp.broadcast_to(col_ok[:, None, :], (GRID_W, wr, GRID_W)).reshape(GRID_W, wr * GRID_W)
    rpb = rpb.astype(F32)
    ckf, cvf = ck.astype(F32), cv.astype(F32)

    def row_block(r):
        r0 = jnp.clip(r - wr // 2, 0, rows - wr)
        qi = lax.dynamic_slice_in_dim(q, r * GRID_W, GRID_W, axis=2)
        ki = lax.dynamic_slice_in_dim(k, r0 * GRID_W, wr * GRID_W, axis=2)
        vi = lax.dynamic_slice_in_dim(v, r0 * GRID_W, wr * GRID_W, axis=2)
        dr = r0 + jnp.arange(wr) - r + WIN_R - 1
        bias = rpb[:, dr[None, :, None], dc[:, None, :]].reshape(h, GRID_W, wr * GRID_W)
        s_lat = jnp.einsum('bhqd,bhkd->bhqk', qi, ki).astype(F32) * scale + bias
        s_lat = jnp.where(mask, s_lat, MASK_VALUE)
        s_ctx = jnp.einsum('bhqd,bhkd->bhqk', qi.astype(F32), ckf) * scale
        p = jax.nn.softmax(jnp.concatenate([s_ctx, s_lat], axis=-1), axis=-1)
        o = (jnp.einsum('bhqk,bhkd->bhqd', p[..., :n_ctx], cvf)
             + jnp.einsum('bhqk,bhkd->bhqd', p[..., n_ctx:], vi.astype(F32)))
        return o.astype(q.dtype)

    o = lax.map(row_block, jnp.arange(rows))
    return jnp.moveaxis(o, 0, 2).reshape(b, h, t, d)


def _token_mixers(h, lp, ctx):
    bsz, t, _ = h.shape
    offs = np.cumsum(IN_SIZES)[:-1].tolist()
    (aq, ak, av, az, aa, ab, bq, bk, bv, cq, cf, ci, cg, dq, dk, dv, mg) = jnp.split(h @ lp['w_in'], offs, axis=-1)
    latent = ctx is not None
    if latent:
        st_gdn0, ck_b, cv_b, st_hgrn0, ck_d, cv_d = ctx
    else:
        st_gdn0 = jnp.zeros((bsz, 2, H_A, DK_A, DV_A), F32)
        st_hgrn0 = jnp.zeros((bsz, 2, H_C, DK_C, DV_C), F32)

    o_a, st_gdn = _gdn_mixer(aq, ak, av, az, aa, ab, lp['gdn_conv_w'], lp['gdn_A_log'], lp['gdn_dt_bias'],
                             lp['gdn_norm_g'], st_gdn0)

    q_b = bq.reshape(bsz, t, H_B, 2, DK_B).transpose(0, 2, 1, 3, 4)
    k_b = bk.reshape(bsz, t, H_B, 2, DK_B).transpose(0, 2, 1, 3, 4)
    v_b = _heads(bv, H_B)
    lam_p = lp['diff_lambda'].astype(F32)
    lam = jnp.exp(jnp.sum(lam_p[0] * lam_p[1])) - jnp.exp(jnp.sum(lam_p[2] * lam_p[3])) + lp['lam_init']
    if latent:
        cos, sin = _axial_rope(t, DK_B)
        q_b = _apply_rope(q_b, cos[:, None], sin[:, None])
        keys = jnp.concatenate([ck_b.reshape(bsz, H_B, -1, 2, DK_B).astype(k_b.dtype),
                                _apply_rope(k_b, cos[:, None], sin[:, None])], axis=2)
        vals = jnp.concatenate([cv_b.astype(v_b.dtype), v_b], axis=2)
    else:
        keys, vals = k_b, v_b
    o_b = _diff_attend(q_b, keys, vals, lam)
    o_b = _merge_heads(_rmsnorm(o_b, lp['diff_norm_g']) * (1.0 - lp['lam_init']))

    o_c, st_hgrn = _hgrn_mixer(cq, cf, ci, cg, lp['hgrn_lb'], lp['hgrn_norm_g'], st_hgrn0)

    q_d, k_d, v_d = _heads(dq, H_D), _heads(dk, H_D), _heads(dv, H_D)
    if latent:
        o_d = _na_latent(q_d, k_d, v_d, ck_d, cv_d, lp['na_rpb'])
    else:
        o_d = _softmax_attend(q_d, k_d, v_d)
    o_d = _merge_heads(o_d)

    branches = jnp.stack([o_a.astype(h.dtype), o_b.astype(h.dtype), o_c.astype(h.dtype), o_d.astype(h.dtype)], axis=2)
    proj = jnp.einsum('btnw,nwd->btnd', branches, lp['w_branch'])
    gates = jax.nn.sigmoid(mg.reshape(bsz, t, N_BRANCH, D_MODEL))
    out = jnp.einsum('btnd,btnd->btd', gates, proj) @ lp['w_out']
    new_ctx = None if latent else (st_gdn, k_b.reshape(bsz, H_B, t, 2 * DK_B), v_b, st_hgrn, k_d, v_d)
    return out, new_ctx


def _swiglu(h, wg, wu, wd):
    return (jax.nn.silu(h @ wg) * (h @ wu)) @ wd


def _layer(x, cond, lp, ctx):
    mod = jax.nn.silu(cond) @ lp['w_ada'] + lp['b_ada']
    sh1, sc1, g1, sh2, sc2, g2 = jnp.split(mod[:, None, :], 6, axis=-1)
    h = _rmsnorm(x, lp['norm1_g']) * (1.0 + sc1) + sh1
    mix, new_ctx = _token_mixers(h, lp, ctx)
    x = x + g1 * mix
    h = _rmsnorm(x, lp['norm2_g']) * (1.0 + sc2) + sh2
    x = x + g2 * _swiglu(h, lp['w_ffn_gate'], lp['w_ffn_up'], lp['w_ffn_down'])
    return x, new_ctx


def setup_inputs(seed: int = 0) -> dict:
    key = jax.random.key(seed)
    ks = iter(jax.random.split(key, 40))

    def nrm(shape, s):
        return jax.random.normal(next(ks), shape, jnp.float32) * s

    L = DEPTH
    a_log = jnp.log(jax.random.uniform(next(ks), (L, 2, H_A), jnp.float32, 1.0, 16.0))
    dt = jnp.exp(jax.random.uniform(next(ks), (L, 2, H_A), jnp.float32, math.log(1e-3), math.log(1e-1)))
    return {
        'x_prompt': nrm((BATCH, SEQ, D_MODEL), 1.0),
        'x_sample': nrm((DEC_BATCH, DEC_SEQ, D_MODEL), 1.0),
        'c': nrm((DEC_BATCH, D_MODEL), 1.0),
        'state_gdn': nrm((DEC_BATCH, L, 2, H_A, DK_A, DV_A), 0.1),
        'cache_diff_k': nrm((DEC_BATCH, L, H_B, PAST_LEN, 2 * DK_B), 1.0),
        'cache_diff_v': nrm((DEC_BATCH, L, H_B, PAST_LEN, DV_B), 1.0),
        'state_hgrn': nrm((DEC_BATCH, L, 2, H_C, DK_C, DV_C), 0.1),
        'cache_na_k': nrm((DEC_BATCH, L, H_D, PAST_LEN, DH_D), 1.0),
        'cache_na_v': nrm((DEC_BATCH, L, H_D, PAST_LEN, DH_D), 1.0),
        'c_ctx': nrm((D_MODEL,), 1.0),
        'w_ada': nrm((L, D_MODEL, 6 * D_MODEL), 0.5 * D_MODEL ** -0.5),
        'b_ada': nrm((L, 6 * D_MODEL), 0.02),
        'norm1_g': 1.0 + nrm((L, D_MODEL), 0.1),
        'w_in': nrm((L, D_MODEL, D_IN), D_MODEL ** -0.5),
        'gdn_conv_w': nrm((L, GDN_CONV_CH, CONV_W), CONV_W ** -0.5),
        'gdn_A_log': a_log,
        'gdn_dt_bias': dt + jnp.log(-jnp.expm1(-dt)),
        'gdn_norm_g': 1.0 + nrm((L, DV_A), 0.1),
        'diff_lambda': nrm((L, 4, DK_B), 0.1),
        'diff_norm_g': 1.0 + nrm((L, DV_B), 0.1),
        'hgrn_lb_logits': nrm((L, 2, H_C * DK_C), 1.0),
        'hgrn_norm_g': 1.0 + nrm((L, DV_C), 0.1),
        'na_rpb': nrm((L, H_D, 2 * WIN_R - 1, 2 * WIN_C - 1), 0.1),
        'w_branch': nrm((L, N_BRANCH, BRANCH_W, D_MODEL), BRANCH_W ** -0.5),
        'w_out': nrm((L, D_MODEL, D_MODEL), D_MODEL ** -0.5),
        'norm2_g': 1.0 + nrm((L, D_MODEL), 0.1),
        'w_ffn_gate': nrm((L, D_MODEL, D_FF), D_MODEL ** -0.5),
        'w_ffn_up': nrm((L, D_MODEL, D_FF), D_MODEL ** -0.5),
        'w_ffn_down': nrm((L, D_FF, D_MODEL), D_FF ** -0.5),
        'final_norm_g': 1.0 + nrm((D_MODEL,), 0.1),
    }


def reference(x_prompt, x_sample, c, state_gdn, cache_diff_k, cache_diff_v, state_hgrn, cache_na_k, cache_na_v,
              c_ctx, w_ada, b_ada, norm1_g, w_in, gdn_conv_w, gdn_A_log, gdn_dt_bias, gdn_norm_g,
              diff_lambda, diff_norm_g, hgrn_lb_logits, hgrn_norm_g, na_rpb, w_branch, w_out, norm2_g,
              w_ffn_gate, w_ffn_up, w_ffn_down, final_norm_g):
    probs = jax.nn.softmax(hgrn_lb_logits.astype(F32), axis=0)
    lb_all = jnp.cumsum(probs, axis=0) - probs[0:1]
    xp, xs = x_prompt, x_sample
    ctx_out = []
    for l in range(DEPTH):
        lp = {
            'w_ada': w_ada[l], 'b_ada': b_ada[l], 'norm1_g': norm1_g[l], 'w_in': w_in[l],
            'gdn_conv_w': gdn_conv_w[l], 'gdn_A_log': gdn_A_log[l], 'gdn_dt_bias': gdn_dt_bias[l],
            'gdn_norm_g': gdn_norm_g[l], 'diff_lambda': diff_lambda[l], 'diff_norm_g': diff_norm_g[l],
            'lam_init': 0.8 - 0.6 * math.exp(-0.3 * l), 'hgrn_lb': lb_all[l], 'hgrn_norm_g': hgrn_norm_g[l],
            'na_rpb': na_rpb[l], 'w_branch': w_branch[l], 'w_out': w_out[l], 'norm2_g': norm2_g[l],
            'w_ffn_gate': w_ffn_gate[l], 'w_ffn_up': w_ffn_up[l], 'w_ffn_down': w_ffn_down[l],
        }
        xp, new_l = _layer(xp, c_ctx[None, :], lp, None)
        ctx_out.append(new_l)
        cached = (state_gdn[:, l], cache_diff_k[:, l], cache_diff_v[:, l], state_hgrn[:, l],
                  cache_na_k[:, l], cache_na_v[:, l])
        xs, _ = _layer(xs, c, lp, cached)
    y_prompt = _rmsnorm(xp, final_norm_g)
    y_sample = _rmsnorm(xs, final_norm_g)
    new_state_gdn = jnp.stack([n[0] for n in ctx_out], axis=1)
    new_cache_diff_k = jnp.stack([n[1] for n in ctx_out], axis=1)
    new_cache_diff_v = jnp.stack([n[2] for n in ctx_out], axis=1)
    new_state_hgrn = jnp.stack([n[3] for n in ctx_out], axis=1)
    new_cache_na_k = jnp.stack([n[4] for n in ctx_out], axis=1)
    new_cache_na_v = jnp.stack([n[5] for n in ctx_out], axis=1)
    return (y_prompt, y_sample, new_state_gdn, new_cache_diff_k, new_cache_diff_v, new_state_hgrn, new_cache_na_k, new_cache_na_v)
```

```python
import functools
import math

import jax
import jax.numpy as jnp
from jax import lax
from jax.experimental import pallas as pl
from jax.experimental.pallas import tpu as pltpu

F32 = jnp.float32
BF16 = jnp.bfloat16

D_MODEL = 1024
DEPTH = 2
GRID_W = 64
N_BRANCH = 4
H_A, DK_A, DV_A, CONV_W, CHUNK_A = 4, 128, 128, 5, 64
H_B, DK_B, DV_B = 4, 64, 128
H_C, DK_C, DV_C = 4, 128, 128
H_D, DH_D, WIN_R, WIN_C = 8, 64, 8, 16
BRANCH_W = 512
ROPE_BASE = 10000.0
D_FF = 2816
EPS = 1e-6
MASK_VALUE = -1e30
F_FLOOR = 1e-30

LANES = 128
VMEM_LIMIT_BYTES = 56 * 1024 * 1024

Y_COLS = 11776
CB_MG = 0
CB_AQ, CB_AK, CB_AV, CB_AZ = 32, 36, 40, 44
CB_BQ, CB_BK, CB_BV = 48, 52, 56
CB_CQ, CB_CFF, CB_CFB, CB_CI, CB_CG = 60, 64, 68, 72, 76
CB_DQ, CB_DK, CB_DV = 80, 84, 88
GATE_COLS = H_A * LANES

HGRN_CHUNK = 128
NA_QROWS = 4


def _cparams(*sem):
    return pltpu.CompilerParams(dimension_semantics=sem, vmem_limit_bytes=VMEM_LIMIT_BYTES)


def _dot(a, b):
    return jnp.dot(a.astype(BF16), b.astype(BF16), preferred_element_type=F32)


def _dot_t(a, b):
    return lax.dot_general(a.astype(BF16), b.astype(BF16), (((1,), (1,)), ((), ())), preferred_element_type=F32)


def _dot_tl(a, b):
    return lax.dot_general(a.astype(BF16), b.astype(BF16), (((0,), (0,)), ((), ())), preferred_element_type=F32)


def _silu(x):
    return x * jax.nn.sigmoid(x)


def _rms(x, g):
    return x * lax.rsqrt(jnp.mean(x * x, axis=-1, keepdims=True) + EPS) * g


def _softmax_parts(s):
    m = jnp.max(s, axis=-1, keepdims=True)
    e = jnp.exp(s - m)
    return e, 1.0 / jnp.sum(e, axis=-1, keepdims=True)


def _adaln_kernel(c_ref, w_ref, b_ref, o_ref):
    s = _silu(c_ref[...])
    o_ref[0] = jnp.dot(s, w_ref[0], precision=lax.Precision.HIGHEST, preferred_element_type=F32) + b_ref[0]


def _adaln(cond, w_ada, b_ada):
    rows = cond.shape[0]
    tn = 1024
    n_out = w_ada.shape[-1]
    return pl.pallas_call(
        _adaln_kernel,
        grid=(DEPTH, n_out // tn),
        in_specs=[pl.BlockSpec((rows, D_MODEL), lambda l, j: (0, 0)),
                  pl.BlockSpec((1, D_MODEL, tn), lambda l, j: (l, 0, j)),
                  pl.BlockSpec((1, 1, tn), lambda l, j: (l, 0, j))],
        out_specs=pl.BlockSpec((1, rows, tn), lambda l, j: (l, 0, j)),
        out_shape=jax.ShapeDtypeStruct((DEPTH, rows, n_out), F32),
        compiler_params=_cparams("parallel", "parallel"),
        name="adaln",
    )(cond, w_ada, b_ada.reshape(DEPTH, 1, n_out))


def _inproj_kernel(x_ref, sh_ref, sc_ref, g_ref, w_ref, wg_ref, y_ref, gate_ref, h_scr):
    @pl.when(pl.program_id(1) == 0)
    def _():
        h = _rms(x_ref[...], g_ref[...]) * (1.0 + sc_ref[0]) + sh_ref[0]
        hb = h.astype(BF16)
        h_scr[...] = hb
        gate_ref[...] = jnp.dot(hb, wg_ref[...], preferred_element_type=F32)

    y_ref[...] = jnp.dot(h_scr[...], w_ref[...], preferred_element_type=F32).astype(y_ref.dtype)


def _inproj(x, mod, norm_g, w, wg, tok_per_row):
    m = x.shape[0]
    tm, tn = 512, 512
    mrow = lambda c: pl.BlockSpec((1, 1, D_MODEL), lambda i, j: ((i * tm) // tok_per_row, 0, c))
    return pl.pallas_call(
        _inproj_kernel,
        grid=(m // tm, Y_COLS // tn),
        in_specs=[pl.BlockSpec((tm, D_MODEL), lambda i, j: (i, 0)),
                  mrow(0), mrow(1),
                  pl.BlockSpec((1, D_MODEL), lambda i, j: (0, 0)),
                  pl.BlockSpec((D_MODEL, tn), lambda i, j: (0, j)),
                  pl.BlockSpec((D_MODEL, GATE_COLS), lambda i, j: (0, 0))],
        out_specs=[pl.BlockSpec((tm, tn), lambda i, j: (i, j)),
                   pl.BlockSpec((tm, GATE_COLS), lambda i, j: (i, 0))],
        out_shape=[jax.ShapeDtypeStruct((m, Y_COLS), BF16),
                   jax.ShapeDtypeStruct((m, GATE_COLS), F32)],
        scratch_shapes=[pltpu.VMEM((tm, D_MODEL), BF16)],
        compiler_params=_cparams("parallel", "arbitrary"),
        name="inproj",
    )(x, mod, mod, norm_g, w, wg)


def _merge_kernel(x_ref, g1_ref, mg_ref, oa_ref, ob_ref, oc_ref, od_ref, wbr_ref, wout_ref, xo_ref):
    acc = None
    for n, o_ref in enumerate((oa_ref, ob_ref, oc_ref, od_ref)):
        proj = jnp.dot(o_ref[...], wbr_ref[n], preferred_element_type=F32)
        gate = jax.nn.sigmoid(mg_ref[:, n * D_MODEL:(n + 1) * D_MODEL].astype(F32))
        acc = gate * proj if acc is None else acc + gate * proj
    out = jnp.dot(acc.astype(BF16), wout_ref[...], preferred_element_type=F32)
    xo_ref[...] = x_ref[...] + g1_ref[0] * out


def _merge(x, mod, y, branches, w_branch, w_out, tok_per_row):
    m = x.shape[0]
    tm = 256
    tok = lambda: pl.BlockSpec((tm, BRANCH_W), lambda i: (i, 0))
    return pl.pallas_call(
        _merge_kernel,
        grid=(m // tm,),
        in_specs=[pl.BlockSpec((tm, D_MODEL), lambda i: (i, 0)),
                  pl.BlockSpec((1, 1, D_MODEL), lambda i: ((i * tm) // tok_per_row, 0, 2)),
                  pl.BlockSpec((tm, N_BRANCH * D_MODEL), lambda i: (i, 0)),
                  tok(), tok(), tok(), tok(),
                  pl.BlockSpec((N_BRANCH, BRANCH_W, D_MODEL), lambda i: (0, 0, 0)),
                  pl.BlockSpec((D_MODEL, D_MODEL), lambda i: (0, 0))],
        out_specs=pl.BlockSpec((tm, D_MODEL), lambda i: (i, 0)),
        out_shape=jax.ShapeDtypeStruct((m, D_MODEL), F32),
        compiler_params=_cparams("parallel"),
        name="merge",
    )(x, mod, y, *branches, w_branch, w_out)


def _ffn_kernel(x_ref, sh_ref, sc_ref, g2_ref, ng_ref, fg_ref, wg_ref, wu_ref, wd_ref, xo_ref, h_scr, acc_scr,
                *, final):
    f = pl.program_id(1)

    @pl.when(f == 0)
    def _():
        h = _rms(x_ref[...], ng_ref[...]) * (1.0 + sc_ref[0]) + sh_ref[0]
        h_scr[...] = h.astype(BF16)
        acc_scr[...] = jnp.zeros_like(acc_scr)

    h = h_scr[...]
    a = jnp.dot(h, wg_ref[...], preferred_element_type=F32)
    u = jnp.dot(h, wu_ref[...], preferred_element_type=F32)
    acc_scr[...] += jnp.dot((_silu(a) * u).astype(BF16), wd_ref[...], preferred_element_type=F32)

    @pl.when(f == pl.num_programs(1) - 1)
    def _():
        xn = x_ref[...] + g2_ref[0] * acc_scr[...]
        xo_ref[...] = _rms(xn, fg_ref[...]) if final else xn


def _ffn(x, mod, norm_g, final_g, wg, wu, wd, tok_per_row, final):
    m = x.shape[0]
    tm, tf = 512, D_FF // 2
    mrow = lambda c: pl.BlockSpec((1, 1, D_MODEL), lambda i, f: ((i * tm) // tok_per_row, 0, c))
    vec = lambda: pl.BlockSpec((1, D_MODEL), lambda i, f: (0, 0))
    return pl.pallas_call(
        functools.partial(_ffn_kernel, final=final),
        grid=(m // tm, D_FF // tf),
        in_specs=[pl.BlockSpec((tm, D_MODEL), lambda i, f: (i, 0)),
                  mrow(3), mrow(4), mrow(5), vec(), vec(),
                  pl.BlockSpec((D_MODEL, tf), lambda i, f: (0, f)),
                  pl.BlockSpec((D_MODEL, tf), lambda i, f: (0, f)),
                  pl.BlockSpec((tf, D_MODEL), lambda i, f: (f, 0))],
        out_specs=pl.BlockSpec((tm, D_MODEL), lambda i, f: (i, 0)),
        out_shape=jax.ShapeDtypeStruct((m, D_MODEL), F32),
        scratch_shapes=[pltpu.VMEM((tm, D_MODEL), BF16), pltpu.VMEM((tm, D_MODEL), F32)],
        compiler_params=_cparams("parallel", "arbitrary"),
        name="ffn",
    )(x, mod, mod, mod, norm_g, final_g, wg, wu, wd)


def _shift_rows(x, off):
    n = x.shape[0]
    return x if off % n == 0 else pltpu.roll(x, (-off) % n, 0)


def _unit_tri_inverse(nmat, ii, jj):
    blk16 = (ii // 16) == (jj // 16)
    blk32 = (ii // 32) == (jj // 32)
    eye = (ii == jj).astype(F32)
    d = jnp.where(blk16, nmat, 0.0)
    x = eye - d
    p = _dot(d, d)
    for _ in range(2):
        x = x + _dot(x, p)
        p = _dot(p, p)
    x = x + _dot(x, p)
    for e in (jnp.where(blk32 & ~blk16, nmat, 0.0), jnp.where(~blk32, nmat, 0.0)):
        x = x - _dot(x, _dot(e, x))
    return x


def _row_form(col_b):
    n = col_b.shape[0]
    sel = (lax.broadcasted_iota(jnp.int32, (n, LANES), 1) == 0).astype(BF16)
    out = None
    rem = col_b
    for _ in range(3):
        part = rem.astype(BF16)
        rem = rem - part.astype(F32)
        r = lax.dot_general(sel, part, (((1,), (1,)), ((), ())), preferred_element_type=F32)
        out = r if out is None else out + r
    return out


def _gdn_chunk(q, k, v, gcb, betab, s, lower, ii, jj):
    c = q.shape[0]
    kb = k * betab
    gram_kk = _dot_t(kb, k)
    gram_qk = _dot_t(q, k)
    diff = gcb[:, :c] - _row_form(gcb)
    tri = (ii >= jj) if lower else (ii <= jj)
    strict = (ii > jj) if lower else (ii < jj)
    decay = jnp.where(tri, jnp.exp(jnp.where(tri, diff, 0.0)), 0.0)
    tinv = _unit_tri_inverse(jnp.where(strict, gram_kk * decay, 0.0), ii, jj)
    eg = jnp.exp(gcb)
    sol = _dot(tinv, jnp.concatenate([v * betab, kb * eg], axis=1))
    u, w = sol[:, :DV_A], sol[:, DV_A:]
    a = jnp.where(tri, gram_qk * decay, 0.0)
    g_end = gcb[c - 1:c, :] if lower else gcb[0:1, :]
    v_new = u - _dot(w, s)
    o = _dot(q * eg, s) + _dot(a, v_new)
    s = s * jnp.exp(g_end) + _dot_tl(k * jnp.exp(g_end - gcb), v_new)
    return o, s


def _gdn_kernel(alog_ref, dtb_ref, q_ref, k_ref, v_ref, z_ref, gate_ref, cwq_ref, cwk_ref, cwv_ref, ng_ref, s0_ref,
                o_ref, sfin_ref, qs, ks, vs, gcf, gcr, btf, btr, of_scr, or_scr):
    t = q_ref.shape[0]
    h = pl.program_id(1)
    row = lax.broadcasted_iota(jnp.int32, (t, LANES), 0)

    def conv_silu(x_ref, cw_ref):
        x = x_ref[...].astype(F32)
        acc = None
        for j in range(CONV_W):
            off = j - CONV_W // 2
            xs = _shift_rows(x, off)
            if off != 0:
                xs = jnp.where((row + off >= 0) & (row + off < t), xs, 0.0)
            term = xs * cw_ref[j:j + 1, :]
            acc = term if acc is None else acc + term
        return _silu(acc)

    q = conv_silu(q_ref, cwq_ref)
    k = conv_silu(k_ref, cwk_ref)
    qs[...] = q * lax.rsqrt(jnp.sum(q * q, axis=-1, keepdims=True) + EPS) * (DK_A ** -0.5)
    ks[...] = k * lax.rsqrt(jnp.sum(k * k, axis=-1, keepdims=True) + EPS)
    vs[...] = conv_silu(v_ref, cwv_ref)

    gt = gate_ref[...]
    pos = row % CHUNK_A
    for d, (gc_scr, bt_scr) in enumerate(((gcf, btf), (gcr, btr))):
        a = jnp.broadcast_to(gt[:, d:d + 1], (t, LANES)) + dtb_ref[d, h]
        softplus = jnp.maximum(a, 0.0) + jnp.log1p(jnp.exp(-jnp.abs(a)))
        g = -jnp.exp(alog_ref[d, h]) * softplus
        step = 1
        while step < CHUNK_A:
            if d == 0:
                g = g + jnp.where(pos >= step, _shift_rows(g, -step), 0.0)
            else:
                g = g + jnp.where(pos < CHUNK_A - step, _shift_rows(g, step), 0.0)
            step *= 2
        gc_scr[...] = g
        bt_scr[...] = jax.nn.sigmoid(jnp.broadcast_to(gt[:, 2 + d:3 + d], (t, LANES)))

    n_chunks = t // CHUNK_A
    ii = lax.broadcasted_iota(jnp.int32, (CHUNK_A, CHUNK_A), 0)
    jj = lax.broadcasted_iota(jnp.int32, (CHUNK_A, CHUNK_A), 1)

    def body(n, carry):
        sf, sr = carry
        slf = pl.ds(pl.multiple_of(n * CHUNK_A, CHUNK_A), CHUNK_A)
        slr = pl.ds(pl.multiple_of((n_chunks - 1 - n) * CHUNK_A, CHUNK_A), CHUNK_A)
        o, sf = _gdn_chunk(qs[slf, :], ks[slf, :], vs[slf, :], gcf[slf, :], btf[slf, :], sf, True, ii, jj)
        of_scr[slf, :] = o
        o, sr = _gdn_chunk(qs[slr, :], ks[slr, :], vs[slr, :], gcr[slr, :], btr[slr, :], sr, False, ii, jj)
        or_scr[slr, :] = o
        return sf, sr

    sf, sr = lax.fori_loop(0, n_chunks, body, (s0_ref[0, 0, 0, 0], s0_ref[0, 0, 1, 0]))
    sfin_ref[0, 0, 0] = sf
    sfin_ref[0, 1, 0] = sr
    o = of_scr[...] + or_scr[...]
    o_ref[...] = (_rms(o, ng_ref[...]) * _silu(z_ref[...].astype(F32))).astype(o_ref.dtype)


def _gdn(y, gates, conv_w, a_log, dt_bias, norm_g, s0, layer, bsz, t):
    col = lambda cb: pl.BlockSpec((t, LANES), lambda b, h: (b, cb + h))
    cw = lambda cb: pl.BlockSpec((8, LANES), lambda b, h: (0, cb + h))
    smem = lambda: pl.BlockSpec(memory_space=pltpu.SMEM)
    scr = lambda: pltpu.VMEM((t, LANES), F32)
    return pl.pallas_call(
        _gdn_kernel,
        grid=(bsz, H_A),
        in_specs=[smem(), smem(), col(CB_AQ), col(CB_AK), col(CB_AV), col(CB_AZ),
                  pl.BlockSpec((t, LANES), lambda b, h: (b, h)),
                  cw(0), cw(H_A), cw(2 * H_A),
                  pl.BlockSpec((1, LANES), lambda b, h: (0, 0)),
                  pl.BlockSpec((1, 1, 2, 1, DK_A, DV_A), lambda b, h: (b, layer, 0, h, 0, 0))],
        out_specs=[pl.BlockSpec((t, LANES), lambda b, h: (b, h)),
                   pl.BlockSpec((1, 2, 1, DK_A, DV_A), lambda b, h: (b, 0, h, 0, 0))],
        out_shape=[jax.ShapeDtypeStruct((bsz * t, BRANCH_W), BF16),
                   jax.ShapeDtypeStruct((bsz, 2, H_A, DK_A, DV_A), F32)],
        scratch_shapes=[scr() for _ in range(9)],
        compiler_params=_cparams("parallel", "parallel"),
        name="gdn",
    )(a_log, dt_bias, y, y, y, y, gates, conv_w, conv_w, conv_w, norm_g, s0)


def _boundary_rows(bc, w, fwd):
    c = bc.shape[0]
    if w >= 4:
        pieces = []
        for start in range(0, c, 2 * w):
            r = start + w - 1 if fwd else start + w
            pieces.append(jnp.broadcast_to(bc[r:r + 1, :], (2 * w, LANES)))
        return pieces[0] if len(pieces) == 1 else jnp.concatenate(pieces, axis=0)
    pos = lax.broadcasted_iota(jnp.int32, (c, LANES), 0) % (2 * w)
    target = w - 1 if fwd else w
    out = bc
    for p in range(2 * w):
        if p != target:
            out = jnp.where(pos == p, _shift_rows(bc, target - p), out)
    return out


def _hgrn_tile(q, k, v, logf, st, fwd, ii, jj):
    c = q.shape[0]
    tri = ((ii >= jj) if fwd else (ii <= jj)).astype(BF16)
    bc = None
    rem = logf
    for _ in range(3):
        part = rem.astype(BF16)
        rem = rem - part.astype(F32)
        r = jnp.dot(tri, part, preferred_element_type=F32)
        bc = r if bc is None else bc + r
    row = lax.broadcasted_iota(jnp.int32, (c, LANES), 0)
    amat = jnp.zeros((c, c), F32)
    w = c // 2
    while w >= 1:
        bm = _boundary_rows(bc, w, fwd)
        upper = (row % (2 * w)) >= w
        qside = upper if fwd else ~upper
        e = jnp.exp(jnp.where(qside, bc - bm, bm - bc))
        g = _dot_t(jnp.where(qside, q * e, 0.0), jnp.where(qside, 0.0, k * e))
        amat = amat + (g if 2 * w == c else jnp.where((ii // (2 * w)) == (jj // (2 * w)), g, 0.0))
        w //= 2
    o = _dot(amat, v) + jnp.sum(q * k, axis=-1, keepdims=True) * v
    tot = bc[c - 1:c, :] if fwd else bc[0:1, :]
    o = o + _dot_t(q * jnp.exp(bc), st)
    st = st * jnp.exp(tot) + _dot_tl(v, k * jnp.exp(tot - bc))
    return o, st


def _hgrn_kernel(q_ref, ff_ref, fr_ref, i_ref, g_ref, lbf_ref, lbr_ref, ng_ref, s0_ref, o_ref, sfin_ref,
                 of_scr, or_scr):
    t = q_ref.shape[0]
    n_tiles = t // HGRN_CHUNK
    ii = lax.broadcasted_iota(jnp.int32, (HGRN_CHUNK, HGRN_CHUNK), 0)
    jj = lax.broadcasted_iota(jnp.int32, (HGRN_CHUNK, HGRN_CHUNK), 1)

    def one(sl, f_ref, lb_ref, st, fwd, o_scr):
        q = _silu(q_ref[sl, :].astype(F32))
        v = i_ref[sl, :].astype(F32)
        lb = lb_ref[...]
        f = jnp.maximum(lb + (1.0 - lb) * jax.nn.sigmoid(f_ref[sl, :].astype(F32)), F_FLOOR)
        o, st = _hgrn_tile(q, 1.0 - f, v, jnp.log(f), st, fwd, ii, jj)
        o_scr[sl, :] = o
        return st

    def body(n, carry):
        sf, sr = carry
        slf = pl.ds(pl.multiple_of(n * HGRN_CHUNK, HGRN_CHUNK), HGRN_CHUNK)
        slr = pl.ds(pl.multiple_of((n_tiles - 1 - n) * HGRN_CHUNK, HGRN_CHUNK), HGRN_CHUNK)
        return one(slf, ff_ref, lbf_ref, sf, True, of_scr), one(slr, fr_ref, lbr_ref, sr, False, or_scr)

    sf, sr = lax.fori_loop(0, n_tiles, body, (s0_ref[0, 0, 0, 0].T, s0_ref[0, 0, 1, 0].T))
    sfin_ref[0, 0, 0] = sf.T
    sfin_ref[0, 1, 0] = sr.T
    o = of_scr[...] + or_scr[...]
    o_ref[...] = (_rms(o, ng_ref[...]) * _silu(g_ref[...].astype(F32))).astype(o_ref.dtype)


def _hgrn(y, lb, norm_g, s0, layer, bsz, t):
    col = lambda cb: pl.BlockSpec((t, LANES), lambda b, h: (b, cb + h))
    return pl.pallas_call(
        _hgrn_kernel,
        grid=(bsz, H_C),
        in_specs=[col(CB_CQ), col(CB_CFF), col(CB_CFB), col(CB_CI), col(CB_CG),
                  pl.BlockSpec((1, LANES), lambda b, h: (0, h)),
                  pl.BlockSpec((1, LANES), lambda b, h: (0, H_C + h)),
                  pl.BlockSpec((1, LANES), lambda b, h: (0, 0)),
                  pl.BlockSpec((1, 1, 2, 1, DK_C, DV_C), lambda b, h: (b, layer, 0, h, 0, 0))],
        out_specs=[pl.BlockSpec((t, LANES), lambda b, h: (b, h)),
                   pl.BlockSpec((1, 2, 1, DK_C, DV_C), lambda b, h: (b, 0, h, 0, 0))],
        out_shape=[jax.ShapeDtypeStruct((bsz * t, BRANCH_W), BF16),
                   jax.ShapeDtypeStruct((bsz, 2, H_C, DK_C, DV_C), F32)],
        scratch_shapes=[pltpu.VMEM((t, LANES), F32), pltpu.VMEM((t, LANES), F32)],
        compiler_params=_cparams("parallel", "parallel"),
        name="hgrn",
    )(y, y, y, y, y, lb, lb, norm_g, s0)


def _diff_lambda(lam_ref, lam_init):
    lp = lam_ref[...]
    return (jnp.exp(jnp.sum(lp[0:1] * lp[1:2], axis=-1, keepdims=True))
            - jnp.exp(jnp.sum(lp[2:3] * lp[3:4], axis=-1, keepdims=True)) + lam_init)


def _diff_core(q, k, v, lam, ng, lam_init):
    lane = lax.broadcasted_iota(jnp.int32, q.shape, 1)
    zero = jnp.zeros_like(q)
    e1, r1 = _softmax_parts(_dot_t(jnp.where(lane < DK_B, q, zero), k))
    e2, r2 = _softmax_parts(_dot_t(jnp.where(lane >= DK_B, q, zero), k))
    o = _dot(e1 * r1 - (lam * r2) * e2, v)
    return _rms(o, ng) * (1.0 - lam_init)


def _diff_ctx_kernel(lam_ref, q_ref, k_ref, v_ref, ng_ref, o_ref, *, lam_init):
    lam = _diff_lambda(lam_ref, lam_init)
    for h in range(H_B):
        sl = slice(h * LANES, (h + 1) * LANES)
        q = q_ref[:, sl] * (DK_B ** -0.5)
        o_ref[:, sl] = _diff_core(q, k_ref[:, sl], v_ref[:, sl], lam, ng_ref[...], lam_init).astype(o_ref.dtype)


def _diff_ctx(y, lam_p, norm_g, lam_init, bsz, t):
    blk = lambda cb: pl.BlockSpec((t, BRANCH_W), lambda b: (b, cb // H_B))
    return pl.pallas_call(
        functools.partial(_diff_ctx_kernel, lam_init=lam_init),
        grid=(bsz,),
        in_specs=[pl.BlockSpec((4, DK_B), lambda b: (0, 0)), blk(CB_BQ), blk(CB_BK), blk(CB_BV),
                  pl.BlockSpec((1, LANES), lambda b: (0, 0))],
        out_specs=pl.BlockSpec((t, BRANCH_W), lambda b: (b, 0)),
        out_shape=jax.ShapeDtypeStruct((bsz * t, BRANCH_W), BF16),
        compiler_params=_cparams("parallel"),
        name="diff_ctx",
    )(lam_p, y, y, y, norm_g)


def _rope(x, cos, sin_signed):
    lane = lax.broadcasted_iota(jnp.int32, x.shape, 1)
    rot = jnp.where((lane % DK_B) < DK_B // 2, pltpu.roll(x, LANES - DK_B // 2, 1), pltpu.roll(x, DK_B // 2, 1))
    return x * cos + rot * sin_signed


def _diff_lat_kernel(lam_ref, q_ref, k_ref, v_ref, ck_ref, cv_ref, cosq_ref, sinq_ref, cos_ref, sin_ref, ng_ref,
                     o_ref, k_scr, v_scr, *, lam_init):
    past = ck_ref.shape[3]

    @pl.when(pl.program_id(2) == 0)
    def _():
        k_scr[:past, :] = ck_ref[0, 0, 0].astype(BF16)
        v_scr[:past, :] = cv_ref[0, 0, 0].astype(BF16)
        k_scr[past:, :] = _rope(k_ref[...].astype(F32), cos_ref[...], sin_ref[...]).astype(BF16)
        v_scr[past:, :] = v_ref[...]

    lam = _diff_lambda(lam_ref, lam_init)
    q = _rope(q_ref[...].astype(F32), cosq_ref[...], sinq_ref[...]) * (DK_B ** -0.5)
    o_ref[...] = _diff_core(q, k_scr[...], v_scr[...], lam, ng_ref[...], lam_init).astype(o_ref.dtype)


def _diff_lat(y, ck, cv, cos, sin_signed, lam_p, norm_g, lam_init, layer, bsz, t):
    tq = 256
    nq = t // tq
    past = ck.shape[3]
    full = lambda cb: pl.BlockSpec((t, LANES), lambda b, h, i: (b, cb + h))
    cache = lambda: pl.BlockSpec((1, 1, 1, past, LANES), lambda b, h, i: (b, layer, h, 0, 0))
    tab_q = lambda: pl.BlockSpec((tq, LANES), lambda b, h, i: (i, 0))
    tab = lambda: pl.BlockSpec((t, LANES), lambda b, h, i: (0, 0))
    return pl.pallas_call(
        functools.partial(_diff_lat_kernel, lam_init=lam_init),
        grid=(bsz, H_B, nq),
        in_specs=[pl.BlockSpec((4, DK_B), lambda b, h, i: (0, 0)),
                  pl.BlockSpec((tq, LANES), lambda b, h, i: (b * nq + i, CB_BQ + h)),
                  full(CB_BK), full(CB_BV), cache(), cache(), tab_q(), tab_q(), tab(), tab(),
                  pl.BlockSpec((1, LANES), lambda b, h, i: (0, 0))],
        out_specs=pl.BlockSpec((tq, LANES), lambda b, h, i: (b * nq + i, h)),
        out_shape=jax.ShapeDtypeStruct((bsz * t, BRANCH_W), BF16),
        scratch_shapes=[pltpu.VMEM((past + t, LANES), BF16), pltpu.VMEM((past + t, LANES), BF16)],
        compiler_params=_cparams("parallel", "parallel", "arbitrary"),
        name="diff_lat",
    )(lam_p, y, y, y, ck, cv, cos, sin_signed, cos, sin_signed, norm_g)


def _pair_attend(q, keys, vals, bias=None):
    lane = lax.broadcasted_iota(jnp.int32, q.shape, 1)
    zero = jnp.zeros_like(q)
    outs = []
    for par in range(2):
        qm = jnp.where((lane < DH_D) == (par == 0), q, zero)
        s = [_dot_t(qm, kk) for kk in keys]
        if bias is not None:
            s[-1] = jnp.where(bias[par] > 0.5 * MASK_VALUE, s[-1] + bias[par], MASK_VALUE)
        m = s[0].max(axis=-1, keepdims=True)
        for x in s[1:]:
            m = jnp.maximum(m, x.max(axis=-1, keepdims=True))
        e = [jnp.exp(x - m) for x in s]
        den = e[0].sum(axis=-1, keepdims=True)
        for x in e[1:]:
            den = den + x.sum(axis=-1, keepdims=True)
        o = _dot(e[0], vals[0])
        for x, vv in zip(e[1:], vals[1:]):
            o = o + _dot(x, vv)
        outs.append(o * (1.0 / den))
    return jnp.where(lane < DH_D, outs[0], outs[1])


def _na_ctx_kernel(q_ref, k_ref, v_ref, o_ref):
    for p in range(H_D // 2):
        sl = slice(p * LANES, (p + 1) * LANES)
        q = q_ref[:, sl] * (DH_D ** -0.5)
        o_ref[:, sl] = _pair_attend(q, [k_ref[:, sl]], [v_ref[:, sl]]).astype(o_ref.dtype)


def _na_ctx(y, bsz, t):
    blk = lambda cb: pl.BlockSpec((t, BRANCH_W), lambda b: (b, cb // 4))
    return pl.pallas_call(
        _na_ctx_kernel,
        grid=(bsz,),
        in_specs=[blk(CB_DQ), blk(CB_DK), blk(CB_DV)],
        out_specs=pl.BlockSpec((t, BRANCH_W), lambda b: (b, 0)),
        out_shape=jax.ShapeDtypeStruct((bsz * t, BRANCH_W), BF16),
        compiler_params=_cparams("parallel"),
        name="na_ctx",
    )(y, y, y)


def _na_bias_kernel(rpb_ref, o_ref):
    hd = pl.program_id(0)
    qc = lax.broadcasted_iota(jnp.int32, (GRID_W, LANES), 0)
    lane = lax.broadcasted_iota(jnp.int32, (GRID_W, LANES), 1)
    kc = lane % GRID_W
    dc = jnp.clip(kc - qc + WIN_C - 1, 0, 2 * WIN_C - 2)
    c0 = jnp.clip(qc - WIN_C // 2, 0, GRID_W - WIN_C)
    col_ok = (kc >= c0) & (kc < c0 + WIN_C)
    n_dr = 2 * WIN_R - 1
    tables = []
    for dr in range(n_dr):
        acc = jnp.zeros((GRID_W, LANES), F32)
        for d in range(2 * WIN_C - 1):
            acc = jnp.where(dc == d, rpb_ref[hd, dr * (2 * WIN_C - 1) + d], acc)
        tables.append(jnp.where(col_ok, acc, MASK_VALUE))
    masked = jnp.full((GRID_W, LANES), MASK_VALUE, F32)
    for i in range(n_dr + 1):
        lo = tables[i - 1] if i >= 1 else masked
        hi = tables[i] if i < n_dr else masked
        o_ref[0, i] = jnp.where(lane < GRID_W, lo, hi)


def _na_bias(rpb):
    n_dr = 2 * WIN_R - 1
    return pl.pallas_call(
        _na_bias_kernel,
        grid=(H_D,),
        in_specs=[pl.BlockSpec(memory_space=pltpu.SMEM)],
        out_specs=pl.BlockSpec((1, n_dr + 1, GRID_W, LANES), lambda h: (h, 0, 0, 0)),
        out_shape=jax.ShapeDtypeStruct((H_D, n_dr + 1, GRID_W, LANES), F32),
        compiler_params=_cparams("parallel"),
        name="na_bias",
    )(rpb.reshape(H_D, n_dr * (2 * WIN_C - 1)))


def _na_lat_kernel(q_ref, k_ref, v_ref, ck_ref, cv_ref, tab_ref, o_ref, *, rows):
    wr = min(WIN_R, rows)
    lane = lax.broadcasted_iota(jnp.int32, (GRID_W, LANES), 1)
    masked = jnp.full((GRID_W, LANES), MASK_VALUE, F32)
    k_lat, v_lat = k_ref[...], v_ref[...]
    ck, cv = ck_ref[0, 0, 0], cv_ref[0, 0, 0]
    for g in range(rows // NA_QROWS):
        bias = []
        for par in range(2):
            blocks = []
            for qi in range(NA_QROWS):
                r = g * NA_QROWS + qi
                r0 = min(max(r - wr // 2, 0), rows - wr)
                tiles = []
                for kp in range(rows // 2):
                    ok0 = r0 <= 2 * kp < r0 + wr
                    ok1 = r0 <= 2 * kp + 1 < r0 + wr
                    if not (ok0 or ok1):
                        tiles.append(masked)
                        continue
                    tile = tab_ref[par, 2 * kp - r + WIN_R]
                    if not ok0:
                        tile = jnp.where(lane >= GRID_W, tile, MASK_VALUE)
                    if not ok1:
                        tile = jnp.where(lane < GRID_W, tile, MASK_VALUE)
                    tiles.append(tile)
                blocks.append(jnp.concatenate(tiles, axis=1))
            bias.append(jnp.concatenate(blocks, axis=0))
        sl = slice(g * NA_QROWS * GRID_W, (g + 1) * NA_QROWS * GRID_W)
        q = q_ref[sl, :] * (DH_D ** -0.5)
        o_ref[sl, :] = _pair_attend(q, [ck, k_lat], [cv, v_lat], bias).astype(o_ref.dtype)


def _na_lat(y, ck, cv, tables, layer, bsz, t):
    rows = t // GRID_W
    past = ck.shape[3]
    n_tab = tables.shape[1]
    col = lambda cb: pl.BlockSpec((t, LANES), lambda b, p: (b, cb + p))
    cache = lambda: pl.BlockSpec((1, 1, 1, past, LANES), lambda b, p: (b, layer, p, 0, 0))
    return pl.pallas_call(
        functools.partial(_na_lat_kernel, rows=rows),
        grid=(bsz, H_D // 2),
        in_specs=[col(CB_DQ), col(CB_DK), col(CB_DV), cache(), cache(),
                  pl.BlockSpec((2, n_tab, GRID_W, LANES), lambda b, p: (p, 0, 0, 0))],
        out_specs=pl.BlockSpec((t, LANES), lambda b, p: (b, p)),
        out_shape=jax.ShapeDtypeStruct((bsz * t, BRANCH_W), BF16),
        compiler_params=_cparams("parallel", "parallel"),
        name="na_lat",
    )(y, y, y, ck, cv, tables)


def _axial_rope_tables(n_tok):
    tok = jnp.arange(n_tok)
    n_freq = DK_B // 4
    inv = ROPE_BASE ** (-jnp.arange(n_freq, dtype=F32) / n_freq)
    ang = jnp.concatenate([(tok // GRID_W).astype(F32)[:, None] * inv,
                           (tok % GRID_W).astype(F32)[:, None] * inv], axis=-1)
    cos, sin = jnp.cos(ang), jnp.sin(ang)
    return jnp.tile(cos, (1, 4)), jnp.tile(jnp.concatenate([-sin, sin], axis=-1), (1, 2))


def _permute_w_in(w):
    sizes = (512, 512, 512, 512, 8, 8, 512, 512, 512, 512, 1024, 512, 512, 512, 512, 512, 4096)
    offs = [0]
    for s in sizes:
        offs.append(offs[-1] + s)
    seg = lambda i: w[:, offs[i]:offs[i + 1]]
    main = jnp.concatenate([seg(16)] + [seg(i) for i in (0, 1, 2, 3, 6, 7, 8, 9, 10, 11, 12, 13, 14, 15)], axis=1)
    aa, ab = seg(4), seg(5)
    gate = jnp.zeros((D_MODEL, H_A, LANES), w.dtype)
    for j, src in enumerate((aa[:, :H_A], aa[:, H_A:], ab[:, :H_A], ab[:, H_A:])):
        gate = gate.at[:, :, j].set(src)
    return main.astype(BF16), gate.reshape(D_MODEL, GATE_COLS).astype(BF16)


def _heads_major(y3, cb, n_heads, width):
    b, t, _ = y3.shape
    x = y3[:, :, cb * LANES: cb * LANES + n_heads * width].astype(F32)
    return x.reshape(b, t, n_heads, width).transpose(0, 2, 1, 3)


def kernel(x_prompt, x_sample, c, state_gdn, cache_diff_k, cache_diff_v, state_hgrn, cache_na_k, cache_na_v, c_ctx,
           w_ada, b_ada, norm1_g, w_in, gdn_conv_w, gdn_A_log, gdn_dt_bias, gdn_norm_g, diff_lambda, diff_norm_g,
           hgrn_lb_logits, hgrn_norm_g, na_rpb, w_branch, w_out, norm2_g, w_ffn_gate, w_ffn_up, w_ffn_down,
           final_norm_g):
    bp, tp, _ = x_prompt.shape
    bs, ts, _ = x_sample.shape
    past = cache_diff_k.shape[3]

    cond = jnp.zeros((16, D_MODEL), F32).at[:bs].set(c).at[bs].set(c_ctx)
    mods = _adaln(cond, w_ada, b_ada)
    probs = jax.nn.softmax(hgrn_lb_logits.astype(F32), axis=0)
    lb_all = jnp.cumsum(probs, axis=0) - probs[0:1]
    cos, sin_signed = _axial_rope_tables(ts)
    pair = lambda a: a.reshape(bs, DEPTH, H_D // 2, 2, past, DH_D).transpose(0, 1, 2, 4, 3, 5).reshape(
        bs, DEPTH, H_D // 2, past, LANES).astype(BF16)
    na_ck, na_cv = pair(cache_na_k), pair(cache_na_v)
    zeros_gdn = jnp.zeros((bp, 1, 2, H_A, DK_A, DV_A), F32)
    zeros_hgrn = jnp.zeros((bp, 1, 2, H_C, DK_C, DV_C), F32)

    xp = x_prompt.reshape(bp * tp, D_MODEL)
    xs = x_sample.reshape(bs * ts, D_MODEL)
    new_ctx = []
    for l in range(DEPTH):
        lam_init = 0.8 - 0.6 * math.exp(-0.3 * l)
        w_main, w_gate = _permute_w_in(w_in[l])
        conv_w = jnp.zeros((8, 3 * BRANCH_W), F32).at[:CONV_W].set(gdn_conv_w[l].T)
        lb = lb_all[l].reshape(1, 2 * H_C * DK_C)
        vecs = dict(n1=norm1_g[l].reshape(1, D_MODEL), n2=norm2_g[l].reshape(1, D_MODEL),
                    fin=final_norm_g.reshape(1, D_MODEL), gdn=gdn_norm_g[l].reshape(1, LANES),
                    diff=diff_norm_g[l].reshape(1, LANES), hgrn=hgrn_norm_g[l].reshape(1, LANES))
        wbr, wo = w_branch[l].astype(BF16), w_out[l].astype(BF16)
        wfg, wfu, wfd = w_ffn_gate[l].astype(BF16), w_ffn_up[l].astype(BF16), w_ffn_down[l].astype(BF16)
        tables = _na_bias(na_rpb[l])
        final = l == DEPTH - 1

        def dense_tail(x, mod, y, branches, tok_per_row):
            x = _merge(x, mod, y, branches, wbr, wo, tok_per_row)
            return _ffn(x, mod, vecs["n2"], vecs["fin"], wfg, wfu, wfd, tok_per_row, final)

        mod = mods[l, bs:bs + 1].reshape(1, 1, 6 * D_MODEL)
        y, gates = _inproj(xp, mod, vecs["n1"], w_main, w_gate, bp * tp)
        o_a, st_gdn = _gdn(y, gates, conv_w, gdn_A_log[l], gdn_dt_bias[l], vecs["gdn"], zeros_gdn, 0, bp, tp)
        o_b = _diff_ctx(y, diff_lambda[l], vecs["diff"], lam_init, bp, tp)
        o_c, st_hgrn = _hgrn(y, lb, vecs["hgrn"], zeros_hgrn, 0, bp, tp)
        o_d = _na_ctx(y, bp, tp)
        xp = dense_tail(xp, mod, y, (o_a, o_b, o_c, o_d), bp * tp)
        y3 = y.reshape(bp, tp, Y_COLS)
        new_ctx.append((st_gdn, _heads_major(y3, CB_BK, H_B, 2 * DK_B), _heads_major(y3, CB_BV, H_B, DV_B),
                        st_hgrn, _heads_major(y3, CB_DK, H_D, DH_D), _heads_major(y3, CB_DV, H_D, DH_D)))

        mod = mods[l, :bs].reshape(bs, 1, 6 * D_MODEL)
        y, gates = _inproj(xs, mod, vecs["n1"], w_main, w_gate, ts)
        o_a, _ = _gdn(y, gates, conv_w, gdn_A_log[l], gdn_dt_bias[l], vecs["gdn"], state_gdn, l, bs, ts)
        o_b = _diff_lat(y, cache_diff_k, cache_diff_v, cos, sin_signed, diff_lambda[l], vecs["diff"], lam_init,
                        l, bs, ts)
        o_c, _ = _hgrn(y, lb, vecs["hgrn"], state_hgrn, l, bs, ts)
        o_d = _na_lat(y, na_ck, na_cv, tables, l, bs, ts)
        xs = dense_tail(xs, mod, y, (o_a, o_b, o_c, o_d), ts)

    stack = lambda i: jnp.stack([n[i] for n in new_ctx], axis=1)
    return (xp.reshape(bp, tp, D_MODEL), xs.reshape(bs, ts, D_MODEL), stack(0), stack(1), stack(2), stack(3),
            stack(4), stack(5))
```

```python
import functools
import math

import jax
import jax.numpy as jnp
import numpy as np
from jax import lax
from jax.experimental import pallas as pl
from jax.experimental.pallas import tpu as pltpu

F32 = jnp.float32
BF16 = jnp.bfloat16

D_MODEL = 1024
DEPTH = 2
GRID_W = 64
N_BRANCH = 4
H_A, DK_A, DV_A, CONV_W, CHUNK_A = 4, 128, 128, 5, 64
H_B, DK_B, DV_B = 4, 64, 128
H_C, DK_C, DV_C = 4, 128, 128
H_D, DH_D, WIN_R, WIN_C = 8, 64, 8, 16
BRANCH_W = 512
ROPE_BASE = 10000.0
D_FF = 2816
EPS = 1e-6
MASK_VALUE = -1e30
F_FLOOR = 1e-30

LANES = 128
VMEM_LIMIT_BYTES = 56 * 1024 * 1024

Y_COLS = 11776
CB_MG = 0
CB_AQ, CB_AK, CB_AV, CB_AZ = 32, 36, 40, 44
CB_BQ, CB_BK, CB_BV = 48, 52, 56
CB_CQ, CB_CFF, CB_CFB, CB_CI, CB_CG = 60, 64, 68, 72, 76
CB_DQ, CB_DK, CB_DV = 80, 84, 88
GATE_COLS = LANES

GDN_CHUNKS_PER_ITER = 2
CONV_PAD = 8
CONV_ROWS = 128
HGRN_CHUNK = 128
NA_QROWS = 4


def _cparams(*sem):
    return pltpu.CompilerParams(dimension_semantics=sem, vmem_limit_bytes=VMEM_LIMIT_BYTES)


def _dot(a, b):
    return jnp.dot(a.astype(BF16), b.astype(BF16), preferred_element_type=F32)


def _dot_t(a, b):
    return lax.dot_general(a.astype(BF16), b.astype(BF16), (((1,), (1,)), ((), ())), preferred_element_type=F32)


def _silu(x):
    return x * jax.nn.sigmoid(x)


def _rms(x, g):
    return x * lax.rsqrt(jnp.mean(x * x, axis=-1, keepdims=True) + EPS) * g


def _softmax_parts(s):
    m = jnp.max(s, axis=-1, keepdims=True)
    e = jnp.exp(s - m)
    return e, 1.0 / jnp.sum(e, axis=-1, keepdims=True)


def _adaln_kernel(c_ref, w_ref, b_ref, o_ref):
    s = _silu(c_ref[...])
    o_ref[0] = jnp.dot(s, w_ref[0], precision=lax.Precision.HIGHEST, preferred_element_type=F32) + b_ref[0]


def _adaln(cond, w_ada, b_ada):
    rows = cond.shape[0]
    tn = 1024
    n_out = w_ada.shape[-1]
    return pl.pallas_call(
        _adaln_kernel,
        grid=(DEPTH, n_out // tn),
        in_specs=[pl.BlockSpec((rows, D_MODEL), lambda l, j: (0, 0)),
                  pl.BlockSpec((1, D_MODEL, tn), lambda l, j: (l, 0, j)),
                  pl.BlockSpec((1, 1, tn), lambda l, j: (l, 0, j))],
        out_specs=pl.BlockSpec((1, rows, tn), lambda l, j: (l, 0, j)),
        out_shape=jax.ShapeDtypeStruct((DEPTH, rows, n_out), F32),
        compiler_params=_cparams("parallel", "parallel"),
        name="adaln",
    )(cond, w_ada, b_ada.reshape(DEPTH, 1, n_out))


def _inproj_kernel(x_ref, sh_ref, sc_ref, g_ref, w_ref, wg_ref, y_ref, gate_ref, h_scr):
    @pl.when(pl.program_id(1) == 0)
    def _():
        h = _rms(x_ref[...], g_ref[...]) * (1.0 + sc_ref[0]) + sh_ref[0]
        hb = h.astype(BF16)
        h_scr[...] = hb
        gate_ref[...] = jnp.dot(hb, wg_ref[...], preferred_element_type=F32)

    y_ref[...] = jnp.dot(h_scr[...], w_ref[...], preferred_element_type=F32).astype(y_ref.dtype)


def _inproj(x, mod, norm_g, w, wg, tok_per_row):
    m = x.shape[0]
    tm, tn = 512, Y_COLS // 4
    mrow = lambda c: pl.BlockSpec((1, 1, D_MODEL), lambda i, j: ((i * tm) // tok_per_row, 0, c))
    return pl.pallas_call(
        _inproj_kernel,
        grid=(m // tm, Y_COLS // tn),
        in_specs=[pl.BlockSpec((tm, D_MODEL), lambda i, j: (i, 0)),
                  mrow(0), mrow(1),
                  pl.BlockSpec((1, D_MODEL), lambda i, j: (0, 0)),
                  pl.BlockSpec((D_MODEL, tn), lambda i, j: (0, j)),
                  pl.BlockSpec((D_MODEL, GATE_COLS), lambda i, j: (0, 0))],
        out_specs=[pl.BlockSpec((tm, tn), lambda i, j: (i, j)),
                   pl.BlockSpec((tm, GATE_COLS), lambda i, j: (i, 0))],
        out_shape=[jax.ShapeDtypeStruct((m, Y_COLS), BF16),
                   jax.ShapeDtypeStruct((m, GATE_COLS), F32)],
        scratch_shapes=[pltpu.VMEM((tm, D_MODEL), BF16)],
        compiler_params=_cparams("parallel", "arbitrary"),
        name="inproj",
    )(x, mod, mod, norm_g, w, wg)


def _merge_kernel(x_ref, g1_ref, mg_ref, oa_ref, ob_ref, oc_ref, od_ref, wbr_ref, wout_ref, xo_ref):
    acc = None
    for n, o_ref in enumerate((oa_ref, ob_ref, oc_ref, od_ref)):
        proj = jnp.dot(o_ref[...], wbr_ref[n], preferred_element_type=F32)
        gate = jax.nn.sigmoid(mg_ref[:, n * D_MODEL:(n + 1) * D_MODEL].astype(F32))
        acc = gate * proj if acc is None else acc + gate * proj
    out = jnp.dot(acc.astype(BF16), wout_ref[...], preferred_element_type=F32)
    xo_ref[...] = x_ref[...] + g1_ref[0] * out


def _merge(x, mod, y, branches, w_branch, w_out, tok_per_row):
    m = x.shape[0]
    tm = 256
    tok = lambda: pl.BlockSpec((tm, BRANCH_W), lambda i: (i, 0))
    return pl.pallas_call(
        _merge_kernel,
        grid=(m // tm,),
        in_specs=[pl.BlockSpec((tm, D_MODEL), lambda i: (i, 0)),
                  pl.BlockSpec((1, 1, D_MODEL), lambda i: ((i * tm) // tok_per_row, 0, 2)),
                  pl.BlockSpec((tm, N_BRANCH * D_MODEL), lambda i: (i, 0)),
                  tok(), tok(), tok(), tok(),
                  pl.BlockSpec((N_BRANCH, BRANCH_W, D_MODEL), lambda i: (0, 0, 0)),
                  pl.BlockSpec((D_MODEL, D_MODEL), lambda i: (0, 0))],
        out_specs=pl.BlockSpec((tm, D_MODEL), lambda i: (i, 0)),
        out_shape=jax.ShapeDtypeStruct((m, D_MODEL), F32),
        compiler_params=_cparams("parallel"),
        name="merge",
    )(x, mod, y, *branches, w_branch, w_out)


def _ffn_kernel(x_ref, sh_ref, sc_ref, g2_ref, ng_ref, fg_ref, wg_ref, wu_ref, wd_ref, xo_ref, h_scr, acc_scr,
                *, final):
    f = pl.program_id(1)

    @pl.when(f == 0)
    def _():
        h = _rms(x_ref[...], ng_ref[...]) * (1.0 + sc_ref[0]) + sh_ref[0]
        h_scr[...] = h.astype(BF16)
        acc_scr[...] = jnp.zeros_like(acc_scr)

    h = h_scr[...]
    a = jnp.dot(h, wg_ref[...], preferred_element_type=F32)
    u = jnp.dot(h, wu_ref[...], preferred_element_type=F32)
    acc_scr[...] += jnp.dot((_silu(a) * u).astype(BF16), wd_ref[...], preferred_element_type=F32)

    @pl.when(f == pl.num_programs(1) - 1)
    def _():
        xn = x_ref[...] + g2_ref[0] * acc_scr[...]
        xo_ref[...] = _rms(xn, fg_ref[...]) if final else xn


def _ffn(x, mod, norm_g, final_g, wg, wu, wd, tok_per_row, final):
    m = x.shape[0]
    tm, tf = 512, D_FF // 2
    mrow = lambda c: pl.BlockSpec((1, 1, D_MODEL), lambda i, f: ((i * tm) // tok_per_row, 0, c))
    vec = lambda: pl.BlockSpec((1, D_MODEL), lambda i, f: (0, 0))
    return pl.pallas_call(
        functools.partial(_ffn_kernel, final=final),
        grid=(m // tm, D_FF // tf),
        in_specs=[pl.BlockSpec((tm, D_MODEL), lambda i, f: (i, 0)),
                  mrow(3), mrow(4), mrow(5), vec(), vec(),
                  pl.BlockSpec((D_MODEL, tf), lambda i, f: (0, f)),
                  pl.BlockSpec((D_MODEL, tf), lambda i, f: (0, f)),
                  pl.BlockSpec((tf, D_MODEL), lambda i, f: (f, 0))],
        out_specs=pl.BlockSpec((tm, D_MODEL), lambda i, f: (i, 0)),
        out_shape=jax.ShapeDtypeStruct((m, D_MODEL), F32),
        scratch_shapes=[pltpu.VMEM((tm, D_MODEL), BF16), pltpu.VMEM((tm, D_MODEL), F32)],
        compiler_params=_cparams("parallel", "arbitrary"),
        name="ffn",
    )(x, mod, mod, mod, norm_g, final_g, wg, wu, wd)


def _shift_rows(x, off):
    n = x.shape[0]
    return x if off % n == 0 else pltpu.roll(x, (-off) % n, 0)


def _bdot(a, b):
    return jnp.einsum('bij,bjk->bik', a.astype(BF16), b.astype(BF16), preferred_element_type=F32)


def _bdot_t(a, b):
    return jnp.einsum('bik,bjk->bij', a.astype(BF16), b.astype(BF16), preferred_element_type=F32)


def _bdot_tl(a, b):
    return jnp.einsum('bki,bkj->bij', a.astype(BF16), b.astype(BF16), preferred_element_type=F32)


def _unit_tri_inverse(nmat, ii, jj):
    blk16 = (ii // 16) == (jj // 16)
    blk32 = (ii // 32) == (jj // 32)
    eye = (ii == jj).astype(F32)
    d = jnp.where(blk16, nmat, 0.0)
    x = eye - d
    p = _bdot(d, d)
    for _ in range(2):
        x = x + _bdot(x, p)
        p = _bdot(p, p)
    x = x + _bdot(x, p)
    for e in (jnp.where(blk32 & ~blk16, nmat, 0.0), jnp.where(~blk32, nmat, 0.0)):
        x = x - _bdot(x, _bdot(e, x))
    return x


def _gdn_prepare(q, k, v, gcb, rowgc, g_end, betab, sgn, ii, jj):
    c = q.shape[1]
    kb = k * betab
    gram_kk = _bdot_t(kb, k)
    gram_qk = _bdot_t(q, k)
    diff = gcb[:, :, :c] - rowgc
    order = (ii - jj) * sgn
    tri, strict = order >= 0, order > 0
    decay = jnp.where(tri, jnp.exp(jnp.where(tri, diff, 0.0)), 0.0)
    tinv = _unit_tri_inverse(jnp.where(strict, gram_kk * decay, 0.0), ii, jj)
    eg = jnp.exp(gcb)
    sol = _bdot(tinv, jnp.concatenate([v * betab, kb * eg], axis=2))
    a = jnp.where(tri, gram_qk * decay, 0.0)
    return sol[:, :, :DV_A], sol[:, :, DV_A:], a, q * eg, k * jnp.exp(g_end - gcb)


def _gdn_kernel(q_ref, k_ref, v_ref, z_ref, gate_ref, cw_ref, alog_ref, dtb_ref, ng_ref, s0_ref, o_ref, sfin_ref,
                xpad, qs, ks, vs, gc_scr, bt_scr, u_scr, w_scr, qd_scr, kd_scr, a_scr, o_scr, s_scr):
    t = q_ref.shape[0]
    n_chunks = t // CHUNK_A
    row = lax.broadcasted_iota(jnp.int32, (t, LANES), 0)
    lane = lax.broadcasted_iota(jnp.int32, (t, LANES), 1)

    xpad[0:CONV_PAD, :] = jnp.zeros((CONV_PAD, BRANCH_W), F32)
    xpad[t + CONV_PAD:t + 2 * CONV_PAD, :] = jnp.zeros((CONV_PAD, BRANCH_W), F32)

    rb = min(t, CONV_ROWS)
    for seg, (x_ref, dst) in enumerate(((q_ref, qs), (k_ref, ks), (v_ref, vs))):
        xpad[CONV_PAD:t + CONV_PAD, :] = x_ref[...].astype(F32)
        for h in range(H_A):
            lanes = slice(h * LANES, (h + 1) * LANES)
            for r in range(0, t, rb):
                acc = None
                for j in range(CONV_W):
                    r0 = CONV_PAD + r + j - CONV_W // 2
                    wj = cw_ref[j:j + 1, seg * BRANCH_W + h * LANES:seg * BRANCH_W + (h + 1) * LANES]
                    term = xpad[r0:r0 + rb, lanes] * wj
                    acc = term if acc is None else acc + term
                x = _silu(acc)
                if seg == 0:
                    x = x * lax.rsqrt(jnp.sum(x * x, axis=-1, keepdims=True) + EPS) * (DK_A ** -0.5)
                elif seg == 1:
                    x = x * lax.rsqrt(jnp.sum(x * x, axis=-1, keepdims=True) + EPS)
                dst[h, r:r + rb, :] = x

    gt = gate_ref[...]
    a = gt + dtb_ref[...]
    g = -jnp.exp(alog_ref[...]) * (jnp.maximum(a, 0.0) + jnp.log1p(jnp.exp(-jnp.abs(a))))
    pos = row % CHUNK_A
    pre, suf = g, g
    step = 1
    while step < CHUNK_A:
        pre = pre + jnp.where(pos >= step, _shift_rows(pre, -step), 0.0)
        suf = suf + jnp.where(pos < CHUNK_A - step, _shift_rows(suf, step), 0.0)
        step *= 2
    gc_scr[...] = jnp.where(lane < H_A, pre, suf)
    bt_scr[...] = jax.nn.sigmoid(gt)

    def chunk_rows(n, d):
        cn = n if d == 0 else n_chunks - 1 - n
        return pl.multiple_of(cn * CHUNK_A, CHUNK_A)

    def head_cols(x, first, rows):
        return jnp.stack([jnp.broadcast_to(x[:, first + h:first + h + 1], (rows, LANES)) for h in range(H_A)], axis=0)

    groups = [(j, d) for j in range(GDN_CHUNKS_PER_ITER) for d in range(2)]
    nb = len(groups) * H_A
    ii = lax.broadcasted_iota(jnp.int32, (nb, CHUNK_A, CHUNK_A), 1)
    jj = lax.broadcasted_iota(jnp.int32, (nb, CHUNK_A, CHUNK_A), 2)
    bb = lax.broadcasted_iota(jnp.int32, (nb, CHUNK_A, CHUNK_A), 0)
    sgn = 1 - 2 * ((bb // H_A) % 2)

    def prepare_body(i, carry):
        parts = {name: [] for name in ("q", "k", "v", "gcb", "rowgc", "g_end", "betab")}
        slices = []
        for j, d in groups:
            sl = pl.ds(chunk_rows(i * GDN_CHUNKS_PER_ITER + j, d), CHUNK_A)
            slices.append(sl)
            gch = gc_scr[sl, :]
            g_row = gch[CHUNK_A - 1:CHUNK_A, :] if d == 0 else gch[0:1, :]
            parts["q"].append(qs[:, sl, :])
            parts["k"].append(ks[:, sl, :])
            parts["v"].append(vs[:, sl, :])
            parts["gcb"].append(head_cols(gch, d * H_A, CHUNK_A))
            parts["rowgc"].append(jnp.broadcast_to(gch.T[d * H_A:(d + 1) * H_A][:, None, :],
                                                   (H_A, CHUNK_A, CHUNK_A)))
            parts["g_end"].append(head_cols(g_row, d * H_A, 1))
            parts["betab"].append(head_cols(bt_scr[sl, :], 2 * H_A + d * H_A, CHUNK_A))
        args = [jnp.concatenate(parts[name], axis=0) for name in ("q", "k", "v", "gcb", "rowgc", "g_end", "betab")]
        outs = _gdn_prepare(*args, sgn, ii, jj)
        for gi, ((j, d), sl) in enumerate(zip(groups, slices)):
            for scr, val in zip((u_scr, w_scr, a_scr, qd_scr, kd_scr), outs):
                scr[d * H_A:(d + 1) * H_A, sl, :] = val[gi * H_A:(gi + 1) * H_A].astype(BF16)
        return carry

    lax.fori_loop(0, n_chunks // GDN_CHUNKS_PER_ITER, prepare_body, 0)

    for d in range(2):
        for h in range(H_A):
            s_scr[d * H_A + h] = s0_ref[0, 0, d, h]

    def scan_body(n, carry):
        slices = [pl.ds(chunk_rows(n, d), CHUNK_A) for d in range(2)]
        both = lambda scr: jnp.concatenate([scr[d * H_A:(d + 1) * H_A, slices[d], :] for d in range(2)], axis=0)
        g_end = jnp.concatenate(
            [head_cols(gc_scr[pl.ds(chunk_rows(n, d) + (CHUNK_A - 1 if d == 0 else 0), 1), :], d * H_A, 1)
             for d in range(2)], axis=0)
        s = s_scr[...]
        v_new = both(u_scr).astype(F32) - _bdot(both(w_scr), s)
        o = _bdot(both(qd_scr), s) + _bdot(both(a_scr), v_new)
        s_scr[...] = s * jnp.exp(g_end) + _bdot_tl(both(kd_scr), v_new)
        for d in range(2):
            o_scr[d * H_A:(d + 1) * H_A, slices[d], :] = o[d * H_A:(d + 1) * H_A]
        return carry

    lax.fori_loop(0, n_chunks, scan_body, 0)

    for h in range(H_A):
        sl = slice(h * LANES, (h + 1) * LANES)
        sfin_ref[0, 0, h] = s_scr[h]
        sfin_ref[0, 1, h] = s_scr[H_A + h]
        o = o_scr[h] + o_scr[H_A + h]
        o_ref[:, sl] = (_rms(o, ng_ref[...]) * _silu(z_ref[:, sl].astype(F32))).astype(o_ref.dtype)


def _gdn(y, gates, conv_w, alog_lanes, dtb_lanes, norm_g, s0, layer, bsz, t):
    blk = lambda cb: pl.BlockSpec((t, BRANCH_W), lambda b: (b, cb // H_A))
    vec = lambda: pl.BlockSpec((1, LANES), lambda b: (0, 0))
    nhd = 2 * H_A
    return pl.pallas_call(
        _gdn_kernel,
        grid=(bsz,),
        in_specs=[blk(CB_AQ), blk(CB_AK), blk(CB_AV), blk(CB_AZ),
                  pl.BlockSpec((t, LANES), lambda b: (b, 0)),
                  pl.BlockSpec((8, 3 * BRANCH_W), lambda b: (0, 0)),
                  vec(), vec(), vec(),
                  pl.BlockSpec((1, 1, 2, H_A, DK_A, DV_A), lambda b: (b, layer, 0, 0, 0, 0))],
        out_specs=[pl.BlockSpec((t, BRANCH_W), lambda b: (b, 0)),
                   pl.BlockSpec((1, 2, H_A, DK_A, DV_A), lambda b: (b, 0, 0, 0, 0))],
        out_shape=[jax.ShapeDtypeStruct((bsz * t, BRANCH_W), BF16),
                   jax.ShapeDtypeStruct((bsz, 2, H_A, DK_A, DV_A), F32)],
        scratch_shapes=[pltpu.VMEM((t + 2 * CONV_PAD, BRANCH_W), F32),
                        pltpu.VMEM((H_A, t, LANES), F32), pltpu.VMEM((H_A, t, LANES), F32),
                        pltpu.VMEM((H_A, t, LANES), F32),
                        pltpu.VMEM((t, LANES), F32), pltpu.VMEM((t, LANES), F32),
                        pltpu.VMEM((nhd, t, DV_A), BF16), pltpu.VMEM((nhd, t, DK_A), BF16),
                        pltpu.VMEM((nhd, t, DK_A), BF16), pltpu.VMEM((nhd, t, DK_A), BF16),
                        pltpu.VMEM((nhd, t, CHUNK_A), BF16),
                        pltpu.VMEM((nhd, t, DV_A), F32), pltpu.VMEM((nhd, DK_A, DV_A), F32)],
        compiler_params=_cparams("parallel"),
        name="gdn",
    )(y, y, y, y, gates, conv_w, alog_lanes, dtb_lanes, norm_g, s0)


def _boundary_rows(bc, w, fwd):
    c = bc.shape[0]
    if w >= 4:
        pieces = []
        for start in range(0, c, 2 * w):
            r = start + w - 1 if fwd else start + w
            pieces.append(jnp.broadcast_to(bc[r:r + 1, :], (2 * w, LANES)))
        return pieces[0] if len(pieces) == 1 else jnp.concatenate(pieces, axis=0)
    pos = lax.broadcasted_iota(jnp.int32, (c, LANES), 0) % (2 * w)
    target = w - 1 if fwd else w
    out = bc
    for p in range(2 * w):
        if p != target:
            out = jnp.where(pos == p, _shift_rows(bc, target - p), out)
    return out


def _hgrn_levels():
    w = HGRN_CHUNK // 2
    while w >= 1:
        yield w
        w //= 2


def _hgrn_constants():
    idx = np.arange(HGRN_CHUNK)
    t, s = idx[:, None], idx[None, :]
    tri = np.stack([t >= s, t <= s]).astype(np.float32)
    pair = []
    for w in _hgrn_levels():
        same = (t // (2 * w)) == (s // (2 * w))
        upper_t, upper_s = (t % (2 * w)) >= w, (s % (2 * w)) >= w
        pair.append(np.stack([same & upper_t & ~upper_s, same & ~upper_t & upper_s]))
    return jnp.asarray(tri, BF16), jnp.asarray(np.stack(pair, axis=1).astype(np.float32))


def _hgrn_tiles(q, k, v, logf, st, tri, pair_ref):
    c = q.shape[1]
    bc = None
    rem = logf
    for _ in range(3):
        part = rem.astype(BF16)
        rem = rem - part.astype(F32)
        r = jnp.einsum('bij,bjk->bik', tri, part, preferred_element_type=F32)
        bc = r if bc is None else bc + r
    amat = None
    for lvl, w in enumerate(_hgrn_levels()):
        bm = jnp.stack([_boundary_rows(bc[0], w, True), _boundary_rows(bc[1], w, False)], axis=0)
        e = jnp.exp(-jnp.abs(bc - bm))
        g = _bdot_t(q * e, k * e) * pair_ref[:, lvl]
        amat = g if amat is None else amat + g
    o = _bdot(amat, v) + jnp.sum(q * k, axis=-1, keepdims=True) * v
    tot = jnp.stack([bc[0, c - 1:c, :], bc[1, 0:1, :]], axis=0)
    o = o + _bdot_t(q * jnp.exp(bc), st)
    st = st * jnp.exp(tot) + _bdot_tl(v, k * jnp.exp(tot - bc))
    return o, st


def _hgrn_kernel(q_ref, ff_ref, fr_ref, i_ref, g_ref, lbf_ref, lbr_ref, ng_ref, s0_ref, tri_ref, pair_ref,
                 o_ref, sfin_ref, of_scr, or_scr):
    t = q_ref.shape[0]
    n_tiles = t // HGRN_CHUNK
    tri = tri_ref[...]

    def load(sl, f_ref, lb_ref):
        lb = lb_ref[...]
        f = jnp.maximum(lb + (1.0 - lb) * jax.nn.sigmoid(f_ref[sl, :].astype(F32)), F_FLOOR)
        return _silu(q_ref[sl, :].astype(F32)), 1.0 - f, i_ref[sl, :].astype(F32), jnp.log(f)

    def body(n, st):
        slf = pl.ds(pl.multiple_of(n * HGRN_CHUNK, HGRN_CHUNK), HGRN_CHUNK)
        slr = pl.ds(pl.multiple_of((n_tiles - 1 - n) * HGRN_CHUNK, HGRN_CHUNK), HGRN_CHUNK)
        both = [jnp.stack(pair, axis=0) for pair in zip(load(slf, ff_ref, lbf_ref), load(slr, fr_ref, lbr_ref))]
        o, st = _hgrn_tiles(*both, st, tri, pair_ref)
        of_scr[slf, :] = o[0]
        or_scr[slr, :] = o[1]
        return st

    st = lax.fori_loop(0, n_tiles, body, jnp.stack([s0_ref[0, 0, 0, 0].T, s0_ref[0, 0, 1, 0].T], axis=0))
    sfin_ref[0, 0, 0] = st[0].T
    sfin_ref[0, 1, 0] = st[1].T
    o = of_scr[...] + or_scr[...]
    o_ref[...] = (_rms(o, ng_ref[...]) * _silu(g_ref[...].astype(F32))).astype(o_ref.dtype)


def _hgrn(y, lb, norm_g, s0, layer, bsz, t):
    col = lambda cb: pl.BlockSpec((t, LANES), lambda b, h: (b, cb + h))
    tri, pair = _hgrn_constants()
    return pl.pallas_call(
        _hgrn_kernel,
        grid=(bsz, H_C),
        in_specs=[col(CB_CQ), col(CB_CFF), col(CB_CFB), col(CB_CI), col(CB_CG),
                  pl.BlockSpec((1, LANES), lambda b, h: (0, h)),
                  pl.BlockSpec((1, LANES), lambda b, h: (0, H_C + h)),
                  pl.BlockSpec((1, LANES), lambda b, h: (0, 0)),
                  pl.BlockSpec((1, 1, 2, 1, DK_C, DV_C), lambda b, h: (b, layer, 0, h, 0, 0)),
                  pl.BlockSpec(tri.shape, lambda b, h: (0, 0, 0)),
                  pl.BlockSpec(pair.shape, lambda b, h: (0, 0, 0, 0))],
        out_specs=[pl.BlockSpec((t, LANES), lambda b, h: (b, h)),
                   pl.BlockSpec((1, 2, 1, DK_C, DV_C), lambda b, h: (b, 0, h, 0, 0))],
        out_shape=[jax.ShapeDtypeStruct((bsz * t, BRANCH_W), BF16),
                   jax.ShapeDtypeStruct((bsz, 2, H_C, DK_C, DV_C), F32)],
        scratch_shapes=[pltpu.VMEM((t, LANES), F32), pltpu.VMEM((t, LANES), F32)],
        compiler_params=_cparams("parallel", "parallel"),
        name="hgrn",
    )(y, y, y, y, y, lb, lb, norm_g, s0, tri, pair)


def _diff_lambda(lam_ref, lam_init):
    lp = lam_ref[...]
    return (jnp.exp(jnp.sum(lp[0:1] * lp[1:2], axis=-1, keepdims=True))
            - jnp.exp(jnp.sum(lp[2:3] * lp[3:4], axis=-1, keepdims=True)) + lam_init)


def _diff_core(q, k, v, lam, ng, lam_init):
    lane = lax.broadcasted_iota(jnp.int32, q.shape, 1)
    zero = jnp.zeros_like(q)
    e1, r1 = _softmax_parts(_dot_t(jnp.where(lane < DK_B, q, zero), k))
    e2, r2 = _softmax_parts(_dot_t(jnp.where(lane >= DK_B, q, zero), k))
    o = _dot(e1 * r1 - (lam * r2) * e2, v)
    return _rms(o, ng) * (1.0 - lam_init)


def _diff_ctx_kernel(lam_ref, q_ref, k_ref, v_ref, ng_ref, o_ref, *, lam_init):
    lam = _diff_lambda(lam_ref, lam_init)
    for h in range(H_B):
        sl = slice(h * LANES, (h + 1) * LANES)
        q = q_ref[:, sl] * (DK_B ** -0.5)
        o_ref[:, sl] = _diff_core(q, k_ref[:, sl], v_ref[:, sl], lam, ng_ref[...], lam_init).astype(o_ref.dtype)


def _diff_ctx(y, lam_p, norm_g, lam_init, bsz, t):
    blk = lambda cb: pl.BlockSpec((t, BRANCH_W), lambda b: (b, cb // H_B))
    return pl.pallas_call(
        functools.partial(_diff_ctx_kernel, lam_init=lam_init),
        grid=(bsz,),
        in_specs=[pl.BlockSpec((4, DK_B), lambda b: (0, 0)), blk(CB_BQ), blk(CB_BK), blk(CB_BV),
                  pl.BlockSpec((1, LANES), lambda b: (0, 0))],
        out_specs=pl.BlockSpec((t, BRANCH_W), lambda b: (b, 0)),
        out_shape=jax.ShapeDtypeStruct((bsz * t, BRANCH_W), BF16),
        compiler_params=_cparams("parallel"),
        name="diff_ctx",
    )(lam_p, y, y, y, norm_g)


def _rope(x, cos, sin_signed):
    lane = lax.broadcasted_iota(jnp.int32, x.shape, 1)
    rot = jnp.where((lane % DK_B) < DK_B // 2, pltpu.roll(x, LANES - DK_B // 2, 1), pltpu.roll(x, DK_B // 2, 1))
    return x * cos + rot * sin_signed


def _diff_lat_kernel(lam_ref, q_ref, k_ref, v_ref, ck_ref, cv_ref, cosq_ref, sinq_ref, cos_ref, sin_ref, ng_ref,
                     o_ref, k_scr, v_scr, *, lam_init):
    past = ck_ref.shape[3]

    @pl.when(pl.program_id(2) == 0)
    def _():
        k_scr[:past, :] = ck_ref[0, 0, 0].astype(BF16)
        v_scr[:past, :] = cv_ref[0, 0, 0].astype(BF16)
        k_scr[past:, :] = _rope(k_ref[...].astype(F32), cos_ref[...], sin_ref[...]).astype(BF16)
        v_scr[past:, :] = v_ref[...]

    lam = _diff_lambda(lam_ref, lam_init)
    q = _rope(q_ref[...].astype(F32), cosq_ref[...], sinq_ref[...]) * (DK_B ** -0.5)
    o_ref[...] = _diff_core(q, k_scr[...], v_scr[...], lam, ng_ref[...], lam_init).astype(o_ref.dtype)


def _diff_lat(y, ck, cv, cos, sin_signed, lam_p, norm_g, lam_init, layer, bsz, t):
    tq = 256
    nq = t // tq
    past = ck.shape[3]
    full = lambda cb: pl.BlockSpec((t, LANES), lambda b, h, i: (b, cb + h))
    cache = lambda: pl.BlockSpec((1, 1, 1, past, LANES), lambda b, h, i: (b, layer, h, 0, 0))
    tab_q = lambda: pl.BlockSpec((tq, LANES), lambda b, h, i: (i, 0))
    tab = lambda: pl.BlockSpec((t, LANES), lambda b, h, i: (0, 0))
    return pl.pallas_call(
        functools.partial(_diff_lat_kernel, lam_init=lam_init),
        grid=(bsz, H_B, nq),
        in_specs=[pl.BlockSpec((4, DK_B), lambda b, h, i: (0, 0)),
                  pl.BlockSpec((tq, LANES), lambda b, h, i: (b * nq + i, CB_BQ + h)),
                  full(CB_BK), full(CB_BV), cache(), cache(), tab_q(), tab_q(), tab(), tab(),
                  pl.BlockSpec((1, LANES), lambda b, h, i: (0, 0))],
        out_specs=pl.BlockSpec((tq, LANES), lambda b, h, i: (b * nq + i, h)),
        out_shape=jax.ShapeDtypeStruct((bsz * t, BRANCH_W), BF16),
        scratch_shapes=[pltpu.VMEM((past + t, LANES), BF16), pltpu.VMEM((past + t, LANES), BF16)],
        compiler_params=_cparams("parallel", "parallel", "arbitrary"),
        name="diff_lat",
    )(lam_p, y, y, y, ck, cv, cos, sin_signed, cos, sin_signed, norm_g)


def _pair_attend(q, keys, vals, bias=None):
    lane = lax.broadcasted_iota(jnp.int32, q.shape, 1)
    zero = jnp.zeros_like(q)
    outs = []
    for par in range(2):
        qm = jnp.where((lane < DH_D) == (par == 0), q, zero)
        s = [_dot_t(qm, kk) for kk in keys]
        if bias is not None:
            s[-1] = jnp.where(bias[par] > 0.5 * MASK_VALUE, s[-1] + bias[par], MASK_VALUE)
        m = s[0].max(axis=-1, keepdims=True)
        for x in s[1:]:
            m = jnp.maximum(m, x.max(axis=-1, keepdims=True))
        e = [jnp.exp(x - m) for x in s]
        den = e[0].sum(axis=-1, keepdims=True)
        for x in e[1:]:
            den = den + x.sum(axis=-1, keepdims=True)
        o = _dot(e[0], vals[0])
        for x, vv in zip(e[1:], vals[1:]):
            o = o + _dot(x, vv)
        outs.append(o * (1.0 / den))
    return jnp.where(lane < DH_D, outs[0], outs[1])


def _na_ctx_kernel(q_ref, k_ref, v_ref, o_ref):
    for p in range(H_D // 2):
        sl = slice(p * LANES, (p + 1) * LANES)
        q = q_ref[:, sl] * (DH_D ** -0.5)
        o_ref[:, sl] = _pair_attend(q, [k_ref[:, sl]], [v_ref[:, sl]]).astype(o_ref.dtype)


def _na_ctx(y, bsz, t):
    blk = lambda cb: pl.BlockSpec((t, BRANCH_W), lambda b: (b, cb // 4))
    return pl.pallas_call(
        _na_ctx_kernel,
        grid=(bsz,),
        in_specs=[blk(CB_DQ), blk(CB_DK), blk(CB_DV)],
        out_specs=pl.BlockSpec((t, BRANCH_W), lambda b: (b, 0)),
        out_shape=jax.ShapeDtypeStruct((bsz * t, BRANCH_W), BF16),
        compiler_params=_cparams("parallel"),
        name="na_ctx",
    )(y, y, y)


def _na_bias_kernel(rpb_ref, o_ref):
    hd = pl.program_id(0)
    qc = lax.broadcasted_iota(jnp.int32, (GRID_W, LANES), 0)
    lane = lax.broadcasted_iota(jnp.int32, (GRID_W, LANES), 1)
    kc = lane % GRID_W
    dc = jnp.clip(kc - qc + WIN_C - 1, 0, 2 * WIN_C - 2)
    c0 = jnp.clip(qc - WIN_C // 2, 0, GRID_W - WIN_C)
    col_ok = (kc >= c0) & (kc < c0 + WIN_C)
    n_dr = 2 * WIN_R - 1
    tables = []
    for dr in range(n_dr):
        acc = jnp.zeros((GRID_W, LANES), F32)
        for d in range(2 * WIN_C - 1):
            acc = jnp.where(dc == d, rpb_ref[hd, dr * (2 * WIN_C - 1) + d], acc)
        tables.append(jnp.where(col_ok, acc, MASK_VALUE))
    masked = jnp.full((GRID_W, LANES), MASK_VALUE, F32)
    for i in range(n_dr + 1):
        lo = tables[i - 1] if i >= 1 else masked
        hi = tables[i] if i < n_dr else masked
        o_ref[0, i] = jnp.where(lane < GRID_W, lo, hi)


def _na_bias(rpb):
    n_dr = 2 * WIN_R - 1
    return pl.pallas_call(
        _na_bias_kernel,
        grid=(H_D,),
        in_specs=[pl.BlockSpec(memory_space=pltpu.SMEM)],
        out_specs=pl.BlockSpec((1, n_dr + 1, GRID_W, LANES), lambda h: (h, 0, 0, 0)),
        out_shape=jax.ShapeDtypeStruct((H_D, n_dr + 1, GRID_W, LANES), F32),
        compiler_params=_cparams("parallel"),
        name="na_bias",
    )(rpb.reshape(H_D, n_dr * (2 * WIN_C - 1)))


def _na_lat_kernel(q_ref, k_ref, v_ref, ck_ref, cv_ref, tab_ref, o_ref, *, rows):
    wr = min(WIN_R, rows)
    lane = lax.broadcasted_iota(jnp.int32, (GRID_W, LANES), 1)
    masked = jnp.full((GRID_W, LANES), MASK_VALUE, F32)
    k_lat, v_lat = k_ref[...], v_ref[...]
    ck, cv = ck_ref[0, 0, 0], cv_ref[0, 0, 0]
    for g in range(rows // NA_QROWS):
        bias = []
        for par in range(2):
            blocks = []
            for qi in range(NA_QROWS):
                r = g * NA_QROWS + qi
                r0 = min(max(r - wr // 2, 0), rows - wr)
                tiles = []
                for kp in range(rows // 2):
                    ok0 = r0 <= 2 * kp < r0 + wr
                    ok1 = r0 <= 2 * kp + 1 < r0 + wr
                    if not (ok0 or ok1):
                        tiles.append(masked)
                        continue
                    tile = tab_ref[par, 2 * kp - r + WIN_R]
                    if not ok0:
                        tile = jnp.where(lane >= GRID_W, tile, MASK_VALUE)
                    if not ok1:
                        tile = jnp.where(lane < GRID_W, tile, MASK_VALUE)
                    tiles.append(tile)
                blocks.append(jnp.concatenate(tiles, axis=1))
            bias.append(jnp.concatenate(blocks, axis=0))
        sl = slice(g * NA_QROWS * GRID_W, (g + 1) * NA_QROWS * GRID_W)
        q = q_ref[sl, :] * (DH_D ** -0.5)
        o_ref[sl, :] = _pair_attend(q, [ck, k_lat], [cv, v_lat], bias).astype(o_ref.dtype)


def _na_lat(y, ck, cv, tables, layer, bsz, t):
    rows = t // GRID_W
    past = ck.shape[3]
    n_tab = tables.shape[1]
    col = lambda cb: pl.BlockSpec((t, LANES), lambda b, p: (b, cb + p))
    cache = lambda: pl.BlockSpec((1, 1, 1, past, LANES), lambda b, p: (b, layer, p, 0, 0))
    return pl.pallas_call(
        functools.partial(_na_lat_kernel, rows=rows),
        grid=(bsz, H_D // 2),
        in_specs=[col(CB_DQ), col(CB_DK), col(CB_DV), cache(), cache(),
                  pl.BlockSpec((2, n_tab, GRID_W, LANES), lambda b, p: (p, 0, 0, 0))],
        out_specs=pl.BlockSpec((t, LANES), lambda b, p: (b, p)),
        out_shape=jax.ShapeDtypeStruct((bsz * t, BRANCH_W), BF16),
        compiler_params=_cparams("parallel", "parallel"),
        name="na_lat",
    )(y, y, y, ck, cv, tables)


def _axial_rope_tables(n_tok):
    tok = jnp.arange(n_tok)
    n_freq = DK_B // 4
    inv = ROPE_BASE ** (-jnp.arange(n_freq, dtype=F32) / n_freq)
    ang = jnp.concatenate([(tok // GRID_W).astype(F32)[:, None] * inv,
                           (tok % GRID_W).astype(F32)[:, None] * inv], axis=-1)
    cos, sin = jnp.cos(ang), jnp.sin(ang)
    return jnp.tile(cos, (1, 4)), jnp.tile(jnp.concatenate([-sin, sin], axis=-1), (1, 2))


def _permute_w_in(w):
    sizes = (512, 512, 512, 512, 8, 8, 512, 512, 512, 512, 1024, 512, 512, 512, 512, 512, 4096)
    offs = [0]
    for s in sizes:
        offs.append(offs[-1] + s)
    seg = lambda i: w[:, offs[i]:offs[i + 1]]
    main = jnp.concatenate([seg(16)] + [seg(i) for i in (0, 1, 2, 3, 6, 7, 8, 9, 10, 11, 12, 13, 14, 15)], axis=1)
    gate = jnp.pad(jnp.concatenate([seg(4), seg(5)], axis=1), ((0, 0), (0, GATE_COLS - 4 * H_A)))
    return main.astype(BF16), gate.astype(BF16)


def _heads_major(y3, cb, n_heads, width):
    b, t, _ = y3.shape
    x = y3[:, :, cb * LANES: cb * LANES + n_heads * width].astype(F32)
    return x.reshape(b, t, n_heads, width).transpose(0, 2, 1, 3)


def kernel(x_prompt, x_sample, c, state_gdn, cache_diff_k, cache_diff_v, state_hgrn, cache_na_k, cache_na_v, c_ctx,
           w_ada, b_ada, norm1_g, w_in, gdn_conv_w, gdn_A_log, gdn_dt_bias, gdn_norm_g, diff_lambda, diff_norm_g,
           hgrn_lb_logits, hgrn_norm_g, na_rpb, w_branch, w_out, norm2_g, w_ffn_gate, w_ffn_up, w_ffn_down,
           final_norm_g):
    bp, tp, _ = x_prompt.shape
    bs, ts, _ = x_sample.shape
    past = cache_diff_k.shape[3]

    cond = jnp.zeros((16, D_MODEL), F32).at[:bs].set(c).at[bs].set(c_ctx)
    mods = _adaln(cond, w_ada, b_ada)
    probs = jax.nn.softmax(hgrn_lb_logits.astype(F32), axis=0)
    lb_all = jnp.cumsum(probs, axis=0) - probs[0:1]
    cos, sin_signed = _axial_rope_tables(ts)
    pair = lambda a: a.reshape(bs, DEPTH, H_D // 2, 2, past, DH_D).transpose(0, 1, 2, 4, 3, 5).reshape(
        bs, DEPTH, H_D // 2, past, LANES).astype(BF16)
    na_ck, na_cv = pair(cache_na_k), pair(cache_na_v)
    zeros_gdn = jnp.zeros((bp, 1, 2, H_A, DK_A, DV_A), F32)
    zeros_hgrn = jnp.zeros((bp, 1, 2, H_C, DK_C, DV_C), F32)

    xp = x_prompt.reshape(bp * tp, D_MODEL)
    xs = x_sample.reshape(bs * ts, D_MODEL)
    new_ctx = []
    for l in range(DEPTH):
        lam_init = 0.8 - 0.6 * math.exp(-0.3 * l)
        w_main, w_gate = _permute_w_in(w_in[l])
        conv_w = jnp.zeros((8, 3 * BRANCH_W), F32).at[:CONV_W].set(gdn_conv_w[l].T)
        lanes = lambda p: jnp.pad(p.astype(F32).reshape(1, 2 * H_A), ((0, 0), (0, LANES - 2 * H_A)))
        alog, dtb = lanes(gdn_A_log[l]), lanes(gdn_dt_bias[l])
        lb = lb_all[l].reshape(1, 2 * H_C * DK_C)
        vecs = dict(n1=norm1_g[l].reshape(1, D_MODEL), n2=norm2_g[l].reshape(1, D_MODEL),
                    fin=final_norm_g.reshape(1, D_MODEL), gdn=gdn_norm_g[l].reshape(1, LANES),
                    diff=diff_norm_g[l].reshape(1, LANES), hgrn=hgrn_norm_g[l].reshape(1, LANES))
        wbr, wo = w_branch[l].astype(BF16), w_out[l].astype(BF16)
        wfg, wfu, wfd = w_ffn_gate[l].astype(BF16), w_ffn_up[l].astype(BF16), w_ffn_down[l].astype(BF16)
        tables = _na_bias(na_rpb[l])
        final = l == DEPTH - 1

        def dense_tail(x, mod, y, branches, tok_per_row):
            x = _merge(x, mod, y, branches, wbr, wo, tok_per_row)
            return _ffn(x, mod, vecs["n2"], vecs["fin"], wfg, wfu, wfd, tok_per_row, final)

        mod = mods[l, bs:bs + 1].reshape(1, 1, 6 * D_MODEL)
        y, gates = _inproj(xp, mod, vecs["n1"], w_main, w_gate, bp * tp)
        o_a, st_gdn = _gdn(y, gates, conv_w, alog, dtb, vecs["gdn"], zeros_gdn, 0, bp, tp)
        o_b = _diff_ctx(y, diff_lambda[l], vecs["diff"], lam_init, bp, tp)
        o_c, st_hgrn = _hgrn(y, lb, vecs["hgrn"], zeros_hgrn, 0, bp, tp)
        o_d = _na_ctx(y, bp, tp)
        xp = dense_tail(xp, mod, y, (o_a, o_b, o_c, o_d), bp * tp)
        y3 = y.reshape(bp, tp, Y_COLS)
        new_ctx.append((st_gdn, _heads_major(y3, CB_BK, H_B, 2 * DK_B), _heads_major(y3, CB_BV, H_B, DV_B),
                        st_hgrn, _heads_major(y3, CB_DK, H_D, DH_D), _heads_major(y3, CB_DV, H_D, DH_D)))

        mod = mods[l, :bs].reshape(bs, 1, 6 * D_MODEL)
        y, gates = _inproj(xs, mod, vecs["n1"], w_main, w_gate, ts)
        o_a, _ = _gdn(y, gates, conv_w, alog, dtb, vecs["gdn"], state_gdn, l, bs, ts)
        o_b = _diff_lat(y, cache_diff_k, cache_diff_v, cos, sin_signed, diff_lambda[l], vecs["diff"], lam_init,
                        l, bs, ts)
        o_c, _ = _hgrn(y, lb, vecs["hgrn"], state_hgrn, l, bs, ts)
        o_d = _na_lat(y, na_ck, na_cv, tables, l, bs, ts)
        xs = dense_tail(xs, mod, y, (o_a, o_b, o_c, o_d), ts)

    stack = lambda i: jnp.stack([n[i] for n in new_ctx], axis=1)
    return (xp.reshape(bp, tp, D_MODEL), xs.reshape(bs, ts, D_MODEL), stack(0), stack(1), stack(2), stack(3),
            stack(4), stack(5))
```

```python
import functools
import math

import jax
import jax.numpy as jnp
import numpy as np
from jax import lax
from jax.experimental import pallas as pl
from jax.experimental.pallas import tpu as pltpu

F32 = jnp.float32
BF16 = jnp.bfloat16

D_MODEL = 1024
DEPTH = 2
GRID_W = 64
N_BRANCH = 4
H_A, DK_A, DV_A, CONV_W, CHUNK_A = 4, 128, 128, 5, 64
H_B, DK_B, DV_B = 4, 64, 128
H_C, DK_C, DV_C = 4, 128, 128
H_D, DH_D, WIN_R, WIN_C = 8, 64, 8, 16
BRANCH_W = 512
ROPE_BASE = 10000.0
D_FF = 2816
EPS = 1e-6
MASK_VALUE = -1e30
F_FLOOR = 1e-30

LANES = 128
VMEM_LIMIT_BYTES = 56 * 1024 * 1024

Y_COLS = 11776
CB_MG = 0
CB_AQ, CB_AK, CB_AV, CB_AZ = 32, 36, 40, 44
CB_BQ, CB_BK, CB_BV = 48, 52, 56
CB_CQ, CB_CFF, CB_CFB, CB_CI, CB_CG = 60, 64, 68, 72, 76
CB_DQ, CB_DK, CB_DV = 80, 84, 88
GATE_COLS = LANES

GDN_CHUNKS_PER_ITER = 4
CONV_PAD = 8
CONV_ROWS = 128
HGRN_CHUNK = 128
HGRN_UNROLL = 2
NA_QROWS = 4


def _cparams(*sem):
    return pltpu.CompilerParams(dimension_semantics=sem, vmem_limit_bytes=VMEM_LIMIT_BYTES)


def _dot(a, b):
    return jnp.dot(a.astype(BF16), b.astype(BF16), preferred_element_type=F32)


def _dot_t(a, b):
    return lax.dot_general(a.astype(BF16), b.astype(BF16), (((1,), (1,)), ((), ())), preferred_element_type=F32)


def _silu(x):
    return x * jax.nn.sigmoid(x)


def _rms(x, g):
    return x * lax.rsqrt(jnp.mean(x * x, axis=-1, keepdims=True) + EPS) * g


def _softmax_parts(s):
    m = jnp.max(s, axis=-1, keepdims=True)
    e = jnp.exp(s - m)
    return e, 1.0 / jnp.sum(e, axis=-1, keepdims=True)


def _adaln_kernel(c_ref, w_ref, b_ref, o_ref):
    s = _silu(c_ref[...])
    o_ref[0] = jnp.dot(s, w_ref[0], precision=lax.Precision.HIGHEST, preferred_element_type=F32) + b_ref[0]


def _adaln(cond, w_ada, b_ada):
    rows = cond.shape[0]
    tn = 1024
    n_out = w_ada.shape[-1]
    return pl.pallas_call(
        _adaln_kernel,
        grid=(DEPTH, n_out // tn),
        in_specs=[pl.BlockSpec((rows, D_MODEL), lambda l, j: (0, 0)),
                  pl.BlockSpec((1, D_MODEL, tn), lambda l, j: (l, 0, j)),
                  pl.BlockSpec((1, 1, tn), lambda l, j: (l, 0, j))],
        out_specs=pl.BlockSpec((1, rows, tn), lambda l, j: (l, 0, j)),
        out_shape=jax.ShapeDtypeStruct((DEPTH, rows, n_out), F32),
        compiler_params=_cparams("parallel", "parallel"),
        name="adaln",
    )(cond, w_ada, b_ada.reshape(DEPTH, 1, n_out))


def _inproj_kernel(x_ref, sh_ref, sc_ref, g_ref, w_ref, wg_ref, y_ref, gate_ref, h_scr):
    @pl.when(pl.program_id(1) == 0)
    def _():
        h = _rms(x_ref[...], g_ref[...]) * (1.0 + sc_ref[0]) + sh_ref[0]
        hb = h.astype(BF16)
        h_scr[...] = hb
        gate_ref[...] = jnp.dot(hb, wg_ref[...], preferred_element_type=F32)

    y_ref[...] = jnp.dot(h_scr[...], w_ref[...], preferred_element_type=F32).astype(y_ref.dtype)


def _inproj(x, mod, norm_g, w, wg, tok_per_row):
    m = x.shape[0]
    tm, tn = min(1024, m), Y_COLS // 4
    mrow = lambda c: pl.BlockSpec((1, 1, D_MODEL), lambda i, j: ((i * tm) // tok_per_row, 0, c))
    return pl.pallas_call(
        _inproj_kernel,
        grid=(m // tm, Y_COLS // tn),
        in_specs=[pl.BlockSpec((tm, D_MODEL), lambda i, j: (i, 0)),
                  mrow(0), mrow(1),
                  pl.BlockSpec((1, D_MODEL), lambda i, j: (0, 0)),
                  pl.BlockSpec((D_MODEL, tn), lambda i, j: (0, j)),
                  pl.BlockSpec((D_MODEL, GATE_COLS), lambda i, j: (0, 0))],
        out_specs=[pl.BlockSpec((tm, tn), lambda i, j: (i, j)),
                   pl.BlockSpec((tm, GATE_COLS), lambda i, j: (i, 0))],
        out_shape=[jax.ShapeDtypeStruct((m, Y_COLS), BF16),
                   jax.ShapeDtypeStruct((m, GATE_COLS), F32)],
        scratch_shapes=[pltpu.VMEM((tm, D_MODEL), BF16)],
        compiler_params=_cparams("parallel", "arbitrary"),
        name="inproj",
    )(x, mod, mod, norm_g, w, wg)


def _merge_kernel(x_ref, g1_ref, mg_ref, oa_ref, ob_ref, oc_ref, od_ref, wbr_ref, wout_ref, xo_ref):
    acc = None
    for n, o_ref in enumerate((oa_ref, ob_ref, oc_ref, od_ref)):
        proj = jnp.dot(o_ref[...], wbr_ref[n], preferred_element_type=F32)
        gate = jax.nn.sigmoid(mg_ref[:, n * D_MODEL:(n + 1) * D_MODEL].astype(F32))
        acc = gate * proj if acc is None else acc + gate * proj
    out = jnp.dot(acc.astype(BF16), wout_ref[...], preferred_element_type=F32)
    xo_ref[...] = x_ref[...] + g1_ref[0] * out


def _merge(x, mod, y, branches, w_branch, w_out, tok_per_row):
    m = x.shape[0]
    tm = 512
    tok = lambda: pl.BlockSpec((tm, BRANCH_W), lambda i: (i, 0))
    return pl.pallas_call(
        _merge_kernel,
        grid=(m // tm,),
        in_specs=[pl.BlockSpec((tm, D_MODEL), lambda i: (i, 0)),
                  pl.BlockSpec((1, 1, D_MODEL), lambda i: ((i * tm) // tok_per_row, 0, 2)),
                  pl.BlockSpec((tm, N_BRANCH * D_MODEL), lambda i: (i, 0)),
                  tok(), tok(), tok(), tok(),
                  pl.BlockSpec((N_BRANCH, BRANCH_W, D_MODEL), lambda i: (0, 0, 0)),
                  pl.BlockSpec((D_MODEL, D_MODEL), lambda i: (0, 0))],
        out_specs=pl.BlockSpec((tm, D_MODEL), lambda i: (i, 0)),
        out_shape=jax.ShapeDtypeStruct((m, D_MODEL), F32),
        compiler_params=_cparams("parallel"),
        name="merge",
    )(x, mod, y, *branches, w_branch, w_out)


def _ffn_kernel(x_ref, sh_ref, sc_ref, g2_ref, ng_ref, fg_ref, wg_ref, wu_ref, wd_ref, xo_ref, h_scr, acc_scr,
                *, final):
    f = pl.program_id(1)

    @pl.when(f == 0)
    def _():
        h = _rms(x_ref[...], ng_ref[...]) * (1.0 + sc_ref[0]) + sh_ref[0]
        h_scr[...] = h.astype(BF16)
        acc_scr[...] = jnp.zeros_like(acc_scr)

    h = h_scr[...]
    a = jnp.dot(h, wg_ref[...], preferred_element_type=F32)
    u = jnp.dot(h, wu_ref[...], preferred_element_type=F32)
    acc_scr[...] += jnp.dot((_silu(a) * u).astype(BF16), wd_ref[...], preferred_element_type=F32)

    @pl.when(f == pl.num_programs(1) - 1)
    def _():
        xn = x_ref[...] + g2_ref[0] * acc_scr[...]
        xo_ref[...] = _rms(xn, fg_ref[...]) if final else xn


def _ffn(x, mod, norm_g, final_g, wg, wu, wd, tok_per_row, final):
    m = x.shape[0]
    tm, tf = 512, D_FF // 2
    mrow = lambda c: pl.BlockSpec((1, 1, D_MODEL), lambda i, f: ((i * tm) // tok_per_row, 0, c))
    vec = lambda: pl.BlockSpec((1, D_MODEL), lambda i, f: (0, 0))
    return pl.pallas_call(
        functools.partial(_ffn_kernel, final=final),
        grid=(m // tm, D_FF // tf),
        in_specs=[pl.BlockSpec((tm, D_MODEL), lambda i, f: (i, 0)),
                  mrow(3), mrow(4), mrow(5), vec(), vec(),
                  pl.BlockSpec((D_MODEL, tf), lambda i, f: (0, f)),
                  pl.BlockSpec((D_MODEL, tf), lambda i, f: (0, f)),
                  pl.BlockSpec((tf, D_MODEL), lambda i, f: (f, 0))],
        out_specs=pl.BlockSpec((tm, D_MODEL), lambda i, f: (i, 0)),
        out_shape=jax.ShapeDtypeStruct((m, D_MODEL), F32),
        scratch_shapes=[pltpu.VMEM((tm, D_MODEL), BF16), pltpu.VMEM((tm, D_MODEL), F32)],
        compiler_params=_cparams("parallel", "arbitrary"),
        name="ffn",
    )(x, mod, mod, mod, norm_g, final_g, wg, wu, wd)


def _shift_rows(x, off):
    n = x.shape[0]
    return x if off % n == 0 else pltpu.roll(x, (-off) % n, 0)


def _bdot(a, b):
    return jnp.einsum('bij,bjk->bik', a.astype(BF16), b.astype(BF16), preferred_element_type=F32)


def _bdot_t(a, b):
    return jnp.einsum('bik,bjk->bij', a.astype(BF16), b.astype(BF16), preferred_element_type=F32)


def _bdot_tl(a, b):
    return jnp.einsum('bki,bkj->bij', a.astype(BF16), b.astype(BF16), preferred_element_type=F32)


def _unit_tri_inverse(nmat, ii, jj):
    blk16 = (ii // 16) == (jj // 16)
    blk32 = (ii // 32) == (jj // 32)
    eye = (ii == jj).astype(F32)
    d = jnp.where(blk16, nmat, 0.0)
    x = eye - d
    p = _bdot(d, d)
    for _ in range(2):
        x = x + _bdot(x, p)
        p = _bdot(p, p)
    x = x + _bdot(x, p)
    for e in (jnp.where(blk32 & ~blk16, nmat, 0.0), jnp.where(~blk32, nmat, 0.0)):
        x = x - _bdot(x, _bdot(e, x))
    return x


def _gdn_prepare(q, k, v, gcb, rowgc, g_end, betab, sgn, ii, jj):
    c = q.shape[1]
    kb = k * betab
    gram_kk = _bdot_t(kb, k)
    gram_qk = _bdot_t(q, k)
    diff = gcb[:, :, :c] - rowgc
    order = (ii - jj) * sgn
    tri, strict = order >= 0, order > 0
    decay = jnp.where(tri, jnp.exp(jnp.where(tri, diff, 0.0)), 0.0)
    tinv = _unit_tri_inverse(jnp.where(strict, gram_kk * decay, 0.0), ii, jj)
    eg = jnp.exp(gcb)
    sol = _bdot(tinv, jnp.concatenate([v * betab, kb * eg], axis=2))
    a = jnp.where(tri, gram_qk * decay, 0.0)
    return sol[:, :, :DV_A], sol[:, :, DV_A:], a, q * eg, k * jnp.exp(g_end - gcb)


def _gdn_kernel(q_ref, k_ref, v_ref, z_ref, gate_ref, cw_ref, alog_ref, dtb_ref, ng_ref, s0_ref, o_ref, sfin_ref,
                xpad, qs, ks, vs, gc_scr, bt_scr, u_scr, w_scr, qd_scr, kd_scr, a_scr, o_scr, s_scr):
    t = q_ref.shape[0]
    n_chunks = t // CHUNK_A
    row = lax.broadcasted_iota(jnp.int32, (t, LANES), 0)
    lane = lax.broadcasted_iota(jnp.int32, (t, LANES), 1)

    xpad[0:CONV_PAD, :] = jnp.zeros((CONV_PAD, BRANCH_W), F32)
    xpad[t + CONV_PAD:t + 2 * CONV_PAD, :] = jnp.zeros((CONV_PAD, BRANCH_W), F32)

    rb = min(t, CONV_ROWS)
    for seg, (x_ref, dst) in enumerate(((q_ref, qs), (k_ref, ks), (v_ref, vs))):
        xpad[CONV_PAD:t + CONV_PAD, :] = x_ref[...].astype(F32)
        for h in range(H_A):
            lanes = slice(h * LANES, (h + 1) * LANES)
            for r in range(0, t, rb):
                acc = None
                for j in range(CONV_W):
                    r0 = CONV_PAD + r + j - CONV_W // 2
                    wj = cw_ref[j:j + 1, seg * BRANCH_W + h * LANES:seg * BRANCH_W + (h + 1) * LANES]
                    term = xpad[r0:r0 + rb, lanes] * wj
                    acc = term if acc is None else acc + term
                x = _silu(acc)
                if seg == 0:
                    x = x * lax.rsqrt(jnp.sum(x * x, axis=-1, keepdims=True) + EPS) * (DK_A ** -0.5)
                elif seg == 1:
                    x = x * lax.rsqrt(jnp.sum(x * x, axis=-1, keepdims=True) + EPS)
                dst[h, r:r + rb, :] = x

    gt = gate_ref[...]
    a = gt + dtb_ref[...]
    g = -jnp.exp(alog_ref[...]) * (jnp.maximum(a, 0.0) + jnp.log1p(jnp.exp(-jnp.abs(a))))
    pos = row % CHUNK_A
    pre, suf = g, g
    step = 1
    while step < CHUNK_A:
        pre = pre + jnp.where(pos >= step, _shift_rows(pre, -step), 0.0)
        suf = suf + jnp.where(pos < CHUNK_A - step, _shift_rows(suf, step), 0.0)
        step *= 2
    gc_scr[...] = jnp.where(lane < H_A, pre, suf)
    bt_scr[...] = jax.nn.sigmoid(gt)

    def chunk_rows(n, d):
        cn = n if d == 0 else n_chunks - 1 - n
        return pl.multiple_of(cn * CHUNK_A, CHUNK_A)

    def head_cols(x, first, rows):
        return jnp.stack([jnp.broadcast_to(x[:, first + h:first + h + 1], (rows, LANES)) for h in range(H_A)], axis=0)

    groups = [(j, d) for j in range(GDN_CHUNKS_PER_ITER) for d in range(2)]
    nb = len(groups) * H_A
    ii = lax.broadcasted_iota(jnp.int32, (nb, CHUNK_A, CHUNK_A), 1)
    jj = lax.broadcasted_iota(jnp.int32, (nb, CHUNK_A, CHUNK_A), 2)
    bb = lax.broadcasted_iota(jnp.int32, (nb, CHUNK_A, CHUNK_A), 0)
    sgn = 1 - 2 * ((bb // H_A) % 2)

    def prepare_body(i, carry):
        parts = {name: [] for name in ("q", "k", "v", "gcb", "rowgc", "g_end", "betab")}
        slices = []
        for j, d in groups:
            sl = pl.ds(chunk_rows(i * GDN_CHUNKS_PER_ITER + j, d), CHUNK_A)
            slices.append(sl)
            gch = gc_scr[sl, :]
            g_row = gch[CHUNK_A - 1:CHUNK_A, :] if d == 0 else gch[0:1, :]
            parts["q"].append(qs[:, sl, :])
            parts["k"].append(ks[:, sl, :])
            parts["v"].append(vs[:, sl, :])
            parts["gcb"].append(head_cols(gch, d * H_A, CHUNK_A))
            parts["rowgc"].append(jnp.broadcast_to(gch.T[d * H_A:(d + 1) * H_A][:, None, :],
                                                   (H_A, CHUNK_A, CHUNK_A)))
            parts["g_end"].append(head_cols(g_row, d * H_A, 1))
            parts["betab"].append(head_cols(bt_scr[sl, :], 2 * H_A + d * H_A, CHUNK_A))
        args = [jnp.concatenate(parts[name], axis=0) for name in ("q", "k", "v", "gcb", "rowgc", "g_end", "betab")]
        outs = _gdn_prepare(*args, sgn, ii, jj)
        for gi, ((j, d), sl) in enumerate(zip(groups, slices)):
            for scr, val in zip((u_scr, w_scr, a_scr, qd_scr, kd_scr), outs):
                scr[d * H_A:(d + 1) * H_A, sl, :] = val[gi * H_A:(gi + 1) * H_A].astype(BF16)
        return carry

    lax.fori_loop(0, n_chunks // GDN_CHUNKS_PER_ITER, prepare_body, 0)

    for d in range(2):
        for h in range(H_A):
            s_scr[d * H_A + h] = s0_ref[0, 0, d, h]

    def scan_body(n, carry):
        slices = [pl.ds(chunk_rows(n, d), CHUNK_A) for d in range(2)]
        both = lambda scr: jnp.concatenate([scr[d * H_A:(d + 1) * H_A, slices[d], :] for d in range(2)], axis=0)
        g_end = jnp.concatenate(
            [head_cols(gc_scr[pl.ds(chunk_rows(n, d) + (CHUNK_A - 1 if d == 0 else 0), 1), :], d * H_A, 1)
             for d in range(2)], axis=0)
        s = s_scr[...]
        v_new = both(u_scr).astype(F32) - _bdot(both(w_scr), s)
        o = _bdot(both(qd_scr), s) + _bdot(both(a_scr), v_new)
        s_scr[...] = s * jnp.exp(g_end) + _bdot_tl(both(kd_scr), v_new)
        for d in range(2):
            o_scr[d * H_A:(d + 1) * H_A, slices[d], :] = o[d * H_A:(d + 1) * H_A]
        return carry

    lax.fori_loop(0, n_chunks, scan_body, 0)

    for h in range(H_A):
        sl = slice(h * LANES, (h + 1) * LANES)
        sfin_ref[0, 0, h] = s_scr[h]
        sfin_ref[0, 1, h] = s_scr[H_A + h]
        o = o_scr[h] + o_scr[H_A + h]
        o_ref[:, sl] = (_rms(o, ng_ref[...]) * _silu(z_ref[:, sl].astype(F32))).astype(o_ref.dtype)


def _gdn(y, gates, conv_w, alog_lanes, dtb_lanes, norm_g, s0, layer, bsz, t):
    blk = lambda cb: pl.BlockSpec((t, BRANCH_W), lambda b: (b, cb // H_A))
    vec = lambda: pl.BlockSpec((1, LANES), lambda b: (0, 0))
    nhd = 2 * H_A
    return pl.pallas_call(
        _gdn_kernel,
        grid=(bsz,),
        in_specs=[blk(CB_AQ), blk(CB_AK), blk(CB_AV), blk(CB_AZ),
                  pl.BlockSpec((t, LANES), lambda b: (b, 0)),
                  pl.BlockSpec((8, 3 * BRANCH_W), lambda b: (0, 0)),
                  vec(), vec(), vec(),
                  pl.BlockSpec((1, 1, 2, H_A, DK_A, DV_A), lambda b: (b, layer, 0, 0, 0, 0))],
        out_specs=[pl.BlockSpec((t, BRANCH_W), lambda b: (b, 0)),
                   pl.BlockSpec((1, 2, H_A, DK_A, DV_A), lambda b: (b, 0, 0, 0, 0))],
        out_shape=[jax.ShapeDtypeStruct((bsz * t, BRANCH_W), BF16),
                   jax.ShapeDtypeStruct((bsz, 2, H_A, DK_A, DV_A), F32)],
        scratch_shapes=[pltpu.VMEM((t + 2 * CONV_PAD, BRANCH_W), F32),
                        pltpu.VMEM((H_A, t, LANES), F32), pltpu.VMEM((H_A, t, LANES), F32),
                        pltpu.VMEM((H_A, t, LANES), F32),
                        pltpu.VMEM((t, LANES), F32), pltpu.VMEM((t, LANES), F32),
                        pltpu.VMEM((nhd, t, DV_A), BF16), pltpu.VMEM((nhd, t, DK_A), BF16),
                        pltpu.VMEM((nhd, t, DK_A), BF16), pltpu.VMEM((nhd, t, DK_A), BF16),
                        pltpu.VMEM((nhd, t, CHUNK_A), BF16),
                        pltpu.VMEM((nhd, t, DV_A), F32), pltpu.VMEM((nhd, DK_A, DV_A), F32)],
        compiler_params=_cparams("parallel"),
        name="gdn",
    )(y, y, y, y, gates, conv_w, alog_lanes, dtb_lanes, norm_g, s0)


def _boundary_rows(bc, w, fwd):
    c = bc.shape[0]
    if w >= 4:
        pieces = []
        for start in range(0, c, 2 * w):
            r = start + w - 1 if fwd else start + w
            pieces.append(jnp.broadcast_to(bc[r:r + 1, :], (2 * w, LANES)))
        return pieces[0] if len(pieces) == 1 else jnp.concatenate(pieces, axis=0)
    pos = lax.broadcasted_iota(jnp.int32, (c, LANES), 0) % (2 * w)
    target = w - 1 if fwd else w
    out = bc
    for p in range(2 * w):
        if p != target:
            out = jnp.where(pos == p, _shift_rows(bc, target - p), out)
    return out


def _hgrn_levels():
    w = HGRN_CHUNK // 2
    while w >= 1:
        yield w
        w //= 2


def _hgrn_constants():
    idx = np.arange(HGRN_CHUNK)
    t, s = idx[:, None], idx[None, :]
    tri = np.stack([t >= s, t <= s]).astype(np.float32)
    pair = []
    for w in _hgrn_levels():
        same = (t // (2 * w)) == (s // (2 * w))
        upper_t, upper_s = (t % (2 * w)) >= w, (s % (2 * w)) >= w
        pair.append(np.stack([same & upper_t & ~upper_s, same & ~upper_t & upper_s]))
    return jnp.asarray(tri, BF16), jnp.asarray(np.stack(pair, axis=1).astype(np.float32))


def _hgrn_tiles(q, k, v, logf, st, tri, pair_ref):
    c = q.shape[1]
    bc = None
    rem = logf
    for _ in range(3):
        part = rem.astype(BF16)
        rem = rem - part.astype(F32)
        r = jnp.einsum('bij,bjk->bik', tri, part, preferred_element_type=F32)
        bc = r if bc is None else bc + r
    nb = q.shape[0]
    amat = None
    for lvl, w in enumerate(_hgrn_levels()):
        bm = jnp.stack([_boundary_rows(bc[i], w, i % 2 == 0) for i in range(nb)], axis=0)
        e = jnp.exp(-jnp.abs(bc - bm))
        pair = jnp.concatenate([pair_ref[:, lvl]] * (nb // 2), axis=0)
        g = _bdot_t(q * e, k * e) * pair
        amat = g if amat is None else amat + g
    o = _bdot(amat, v) + jnp.sum(q * k, axis=-1, keepdims=True) * v
    tot = jnp.stack([bc[i, c - 1:c, :] if i % 2 == 0 else bc[i, 0:1, :] for i in range(nb)], axis=0)
    inc = _bdot_tl(v, k * jnp.exp(tot - bc))
    decay = jnp.exp(tot)
    states = [st]
    for j in range(nb // 2):
        states.append(states[-1] * decay[2 * j:2 * j + 2] + inc[2 * j:2 * j + 2])
    o = o + _bdot_t(q * jnp.exp(bc), jnp.concatenate(states[:-1], axis=0))
    return o, states[-1]


def _hgrn_kernel(q_ref, ff_ref, fr_ref, i_ref, g_ref, lbf_ref, lbr_ref, ng_ref, s0_ref, tri_ref, pair_ref,
                 o_ref, sfin_ref, of_scr, or_scr):
    t = q_ref.shape[0]
    n_tiles = t // HGRN_CHUNK
    tri = jnp.concatenate([tri_ref[...]] * HGRN_UNROLL, axis=0)

    def load(sl, f_ref, lb_ref):
        lb = lb_ref[...]
        f = jnp.maximum(lb + (1.0 - lb) * jax.nn.sigmoid(f_ref[sl, :].astype(F32)), F_FLOOR)
        return _silu(q_ref[sl, :].astype(F32)), 1.0 - f, i_ref[sl, :].astype(F32), jnp.log(f)

    def body(i, st):
        slices, loaded = [], []
        for j in range(HGRN_UNROLL):
            n = i * HGRN_UNROLL + j
            slf = pl.ds(pl.multiple_of(n * HGRN_CHUNK, HGRN_CHUNK), HGRN_CHUNK)
            slr = pl.ds(pl.multiple_of((n_tiles - 1 - n) * HGRN_CHUNK, HGRN_CHUNK), HGRN_CHUNK)
            slices += [slf, slr]
            loaded += [load(slf, ff_ref, lbf_ref), load(slr, fr_ref, lbr_ref)]
        o, st = _hgrn_tiles(*[jnp.stack(parts, axis=0) for parts in zip(*loaded)], st, tri, pair_ref)
        for idx, sl in enumerate(slices):
            (of_scr if idx % 2 == 0 else or_scr)[sl, :] = o[idx]
        return st

    st = lax.fori_loop(0, n_tiles // HGRN_UNROLL, body,
                       jnp.stack([s0_ref[0, 0, 0, 0].T, s0_ref[0, 0, 1, 0].T], axis=0))
    sfin_ref[0, 0, 0] = st[0].T
    sfin_ref[0, 1, 0] = st[1].T
    o = of_scr[...] + or_scr[...]
    o_ref[...] = (_rms(o, ng_ref[...]) * _silu(g_ref[...].astype(F32))).astype(o_ref.dtype)


def _hgrn(y, lb, norm_g, s0, layer, bsz, t):
    col = lambda cb: pl.BlockSpec((t, LANES), lambda b, h: (b, cb + h))
    tri, pair = _hgrn_constants()
    return pl.pallas_call(
        _hgrn_kernel,
        grid=(bsz, H_C),
        in_specs=[col(CB_CQ), col(CB_CFF), col(CB_CFB), col(CB_CI), col(CB_CG),
                  pl.BlockSpec((1, LANES), lambda b, h: (0, h)),
                  pl.BlockSpec((1, LANES), lambda b, h: (0, H_C + h)),
                  pl.BlockSpec((1, LANES), lambda b, h: (0, 0)),
                  pl.BlockSpec((1, 1, 2, 1, DK_C, DV_C), lambda b, h: (b, layer, 0, h, 0, 0)),
                  pl.BlockSpec(tri.shape, lambda b, h: (0, 0, 0)),
                  pl.BlockSpec(pair.shape, lambda b, h: (0, 0, 0, 0))],
        out_specs=[pl.BlockSpec((t, LANES), lambda b, h: (b, h)),
                   pl.BlockSpec((1, 2, 1, DK_C, DV_C), lambda b, h: (b, 0, h, 0, 0))],
        out_shape=[jax.ShapeDtypeStruct((bsz * t, BRANCH_W), BF16),
                   jax.ShapeDtypeStruct((bsz, 2, H_C, DK_C, DV_C), F32)],
        scratch_shapes=[pltpu.VMEM((t, LANES), F32), pltpu.VMEM((t, LANES), F32)],
        compiler_params=_cparams("parallel", "parallel"),
        name="hgrn",
    )(y, y, y, y, y, lb, lb, norm_g, s0, tri, pair)


def _diff_lambda(lam_ref, lam_init):
    lp = lam_ref[...]
    return (jnp.exp(jnp.sum(lp[0:1] * lp[1:2], axis=-1, keepdims=True))
            - jnp.exp(jnp.sum(lp[2:3] * lp[3:4], axis=-1, keepdims=True)) + lam_init)


def _diff_core(problems, lam, ng, lam_init):
    scores = []
    for q, k, _ in problems:
        lane = lax.broadcasted_iota(jnp.int32, q.shape, 1)
        zero = jnp.zeros_like(q)
        scores.append((_dot_t(jnp.where(lane < DK_B, q, zero), k), _dot_t(jnp.where(lane >= DK_B, q, zero), k)))
    outs = []
    for (s1, s2), (_, _, v) in zip(scores, problems):
        e1, r1 = _softmax_parts(s1)
        e2, r2 = _softmax_parts(s2)
        outs.append(_dot(e1 * r1 - (lam * r2) * e2, v))
    return [_rms(o, ng) * (1.0 - lam_init) for o in outs]


def _diff_ctx_kernel(lam_ref, q_ref, k_ref, v_ref, ng_ref, o_ref, *, lam_init):
    lam = _diff_lambda(lam_ref, lam_init)
    heads = [slice(h * LANES, (h + 1) * LANES) for h in range(H_B)]
    outs = _diff_core([(q_ref[:, sl] * (DK_B ** -0.5), k_ref[:, sl], v_ref[:, sl]) for sl in heads],
                      lam, ng_ref[...], lam_init)
    for sl, o in zip(heads, outs):
        o_ref[:, sl] = o.astype(o_ref.dtype)


def _diff_ctx(y, lam_p, norm_g, lam_init, bsz, t):
    blk = lambda cb: pl.BlockSpec((t, BRANCH_W), lambda b: (b, cb // H_B))
    return pl.pallas_call(
        functools.partial(_diff_ctx_kernel, lam_init=lam_init),
        grid=(bsz,),
        in_specs=[pl.BlockSpec((4, DK_B), lambda b: (0, 0)), blk(CB_BQ), blk(CB_BK), blk(CB_BV),
                  pl.BlockSpec((1, LANES), lambda b: (0, 0))],
        out_specs=pl.BlockSpec((t, BRANCH_W), lambda b: (b, 0)),
        out_shape=jax.ShapeDtypeStruct((bsz * t, BRANCH_W), BF16),
        compiler_params=_cparams("parallel"),
        name="diff_ctx",
    )(lam_p, y, y, y, norm_g)


def _rope(x, cos, sin_signed):
    lane = lax.broadcasted_iota(jnp.int32, x.shape, 1)
    rot = jnp.where((lane % DK_B) < DK_B // 2, pltpu.roll(x, LANES - DK_B // 2, 1), pltpu.roll(x, DK_B // 2, 1))
    return x * cos + rot * sin_signed


def _diff_lat_kernel(lam_ref, q_ref, k_ref, v_ref, ck_ref, cv_ref, cosq_ref, sinq_ref, cos_ref, sin_ref, ng_ref,
                     o_ref, k_scr, v_scr, *, lam_init):
    past = ck_ref.shape[3]

    @pl.when(pl.program_id(2) == 0)
    def _():
        k_scr[:past, :] = ck_ref[0, 0, 0].astype(BF16)
        v_scr[:past, :] = cv_ref[0, 0, 0].astype(BF16)
        k_scr[past:, :] = _rope(k_ref[...].astype(F32), cos_ref[...], sin_ref[...]).astype(BF16)
        v_scr[past:, :] = v_ref[...]

    lam = _diff_lambda(lam_ref, lam_init)
    q = (_rope(q_ref[...].astype(F32), cosq_ref[...], sinq_ref[...]) * (DK_B ** -0.5)).astype(BF16)
    k, v = k_scr[...], v_scr[...]
    half = q.shape[0] // 2
    outs = _diff_core([(q[:half], k, v), (q[half:], k, v)], lam, ng_ref[...], lam_init)
    o_ref[:half, :] = outs[0].astype(o_ref.dtype)
    o_ref[half:, :] = outs[1].astype(o_ref.dtype)


def _diff_lat(y, ck, cv, cos, sin_signed, lam_p, norm_g, lam_init, layer, bsz, t):
    tq = 512
    nq = t // tq
    past = ck.shape[3]
    full = lambda cb: pl.BlockSpec((t, LANES), lambda b, h, i: (b, cb + h))
    cache = lambda: pl.BlockSpec((1, 1, 1, past, LANES), lambda b, h, i: (b, layer, h, 0, 0))
    tab_q = lambda: pl.BlockSpec((tq, LANES), lambda b, h, i: (i, 0))
    tab = lambda: pl.BlockSpec((t, LANES), lambda b, h, i: (0, 0))
    return pl.pallas_call(
        functools.partial(_diff_lat_kernel, lam_init=lam_init),
        grid=(bsz, H_B, nq),
        in_specs=[pl.BlockSpec((4, DK_B), lambda b, h, i: (0, 0)),
                  pl.BlockSpec((tq, LANES), lambda b, h, i: (b * nq + i, CB_BQ + h)),
                  full(CB_BK), full(CB_BV), cache(), cache(), tab_q(), tab_q(), tab(), tab(),
                  pl.BlockSpec((1, LANES), lambda b, h, i: (0, 0))],
        out_specs=pl.BlockSpec((tq, LANES), lambda b, h, i: (b * nq + i, h)),
        out_shape=jax.ShapeDtypeStruct((bsz * t, BRANCH_W), BF16),
        scratch_shapes=[pltpu.VMEM((past + t, LANES), BF16), pltpu.VMEM((past + t, LANES), BF16)],
        compiler_params=_cparams("parallel", "parallel", "arbitrary"),
        name="diff_lat",
    )(lam_p, y, y, y, ck, cv, cos, sin_signed, cos, sin_signed, norm_g)


def _pair_attend(q, keys, vals, bias=None):
    lane = lax.broadcasted_iota(jnp.int32, q.shape, 1)
    zero = jnp.zeros_like(q)
    scores = [[_dot_t(jnp.where((lane < DH_D) == (par == 0), q, zero), kk) for kk in keys] for par in range(2)]
    outs = []
    for par in range(2):
        s = scores[par]
        if bias is not None:
            s[-1] = jnp.where(bias[par] > 0.5 * MASK_VALUE, s[-1] + bias[par], MASK_VALUE)
        m = s[0].max(axis=-1, keepdims=True)
        for x in s[1:]:
            m = jnp.maximum(m, x.max(axis=-1, keepdims=True))
        e = [jnp.exp(x - m) for x in s]
        den = e[0].sum(axis=-1, keepdims=True)
        for x in e[1:]:
            den = den + x.sum(axis=-1, keepdims=True)
        o = _dot(e[0], vals[0])
        for x, vv in zip(e[1:], vals[1:]):
            o = o + _dot(x, vv)
        outs.append(o * (1.0 / den))
    return jnp.where(lane < DH_D, outs[0], outs[1])


def _na_ctx_kernel(q_ref, k_ref, v_ref, o_ref):
    for p in range(H_D // 2):
        sl = slice(p * LANES, (p + 1) * LANES)
        q = q_ref[:, sl] * (DH_D ** -0.5)
        o_ref[:, sl] = _pair_attend(q, [k_ref[:, sl]], [v_ref[:, sl]]).astype(o_ref.dtype)


def _na_ctx(y, bsz, t):
    blk = lambda cb: pl.BlockSpec((t, BRANCH_W), lambda b: (b, cb // 4))
    return pl.pallas_call(
        _na_ctx_kernel,
        grid=(bsz,),
        in_specs=[blk(CB_DQ), blk(CB_DK), blk(CB_DV)],
        out_specs=pl.BlockSpec((t, BRANCH_W), lambda b: (b, 0)),
        out_shape=jax.ShapeDtypeStruct((bsz * t, BRANCH_W), BF16),
        compiler_params=_cparams("parallel"),
        name="na_ctx",
    )(y, y, y)


def _na_bias_kernel(rpb_ref, o_ref):
    hd = pl.program_id(0)
    qc = lax.broadcasted_iota(jnp.int32, (GRID_W, LANES), 0)
    lane = lax.broadcasted_iota(jnp.int32, (GRID_W, LANES), 1)
    kc = lane % GRID_W
    dc = jnp.clip(kc - qc + WIN_C - 1, 0, 2 * WIN_C - 2)
    c0 = jnp.clip(qc - WIN_C // 2, 0, GRID_W - WIN_C)
    col_ok = (kc >= c0) & (kc < c0 + WIN_C)
    n_dr = 2 * WIN_R - 1
    tables = []
    for dr in range(n_dr):
        acc = jnp.zeros((GRID_W, LANES), F32)
        for d in range(2 * WIN_C - 1):
            acc = jnp.where(dc == d, rpb_ref[hd, dr * (2 * WIN_C - 1) + d], acc)
        tables.append(jnp.where(col_ok, acc, MASK_VALUE))
    masked = jnp.full((GRID_W, LANES), MASK_VALUE, F32)
    for i in range(n_dr + 1):
        lo = tables[i - 1] if i >= 1 else masked
        hi = tables[i] if i < n_dr else masked
        o_ref[0, i] = jnp.where(lane < GRID_W, lo, hi)


def _na_bias(rpb):
    n_dr = 2 * WIN_R - 1
    return pl.pallas_call(
        _na_bias_kernel,
        grid=(H_D,),
        in_specs=[pl.BlockSpec(memory_space=pltpu.SMEM)],
        out_specs=pl.BlockSpec((1, n_dr + 1, GRID_W, LANES), lambda h: (h, 0, 0, 0)),
        out_shape=jax.ShapeDtypeStruct((H_D, n_dr + 1, GRID_W, LANES), F32),
        compiler_params=_cparams("parallel"),
        name="na_bias",
    )(rpb.reshape(H_D, n_dr * (2 * WIN_C - 1)))


def _na_lat_kernel(q_ref, k_ref, v_ref, ck_ref, cv_ref, tab_ref, o_ref, *, rows):
    wr = min(WIN_R, rows)
    lane = lax.broadcasted_iota(jnp.int32, (GRID_W, LANES), 1)
    masked = jnp.full((GRID_W, LANES), MASK_VALUE, F32)
    ck, cv = ck_ref[0, 0, 0], cv_ref[0, 0, 0]
    first_row = lambda r: min(max(r - wr // 2, 0), rows - wr)
    for g in range(rows // NA_QROWS):
        starts = [first_row(g * NA_QROWS + qi) for qi in range(NA_QROWS)]
        kp0, kp1 = min(starts) // 2, (max(starts) + wr + 1) // 2
        k_lat = k_ref[2 * kp0 * GRID_W:2 * kp1 * GRID_W, :]
        v_lat = v_ref[2 * kp0 * GRID_W:2 * kp1 * GRID_W, :]
        bias = []
        for par in range(2):
            blocks = []
            for qi in range(NA_QROWS):
                r = g * NA_QROWS + qi
                r0 = first_row(r)
                tiles = []
                for kp in range(kp0, kp1):
                    ok0 = r0 <= 2 * kp < r0 + wr
                    ok1 = r0 <= 2 * kp + 1 < r0 + wr
                    if not (ok0 or ok1):
                        tiles.append(masked)
                        continue
                    tile = tab_ref[par, 2 * kp - r + WIN_R]
                    if not ok0:
                        tile = jnp.where(lane >= GRID_W, tile, MASK_VALUE)
                    if not ok1:
                        tile = jnp.where(lane < GRID_W, tile, MASK_VALUE)
                    tiles.append(tile)
                blocks.append(jnp.concatenate(tiles, axis=1))
            bias.append(jnp.concatenate(blocks, axis=0))
        sl = slice(g * NA_QROWS * GRID_W, (g + 1) * NA_QROWS * GRID_W)
        q = q_ref[sl, :] * (DH_D ** -0.5)
        o_ref[sl, :] = _pair_attend(q, [ck, k_lat], [cv, v_lat], bias).astype(o_ref.dtype)


def _na_lat(y, ck, cv, tables, layer, bsz, t):
    rows = t // GRID_W
    past = ck.shape[3]
    n_tab = tables.shape[1]
    col = lambda cb: pl.BlockSpec((t, LANES), lambda b, p: (b, cb + p))
    cache = lambda: pl.BlockSpec((1, 1, 1, past, LANES), lambda b, p: (b, layer, p, 0, 0))
    return pl.pallas_call(
        functools.partial(_na_lat_kernel, rows=rows),
        grid=(bsz, H_D // 2),
        in_specs=[col(CB_DQ), col(CB_DK), col(CB_DV), cache(), cache(),
                  pl.BlockSpec((2, n_tab, GRID_W, LANES), lambda b, p: (p, 0, 0, 0))],
        out_specs=pl.BlockSpec((t, LANES), lambda b, p: (b, p)),
        out_shape=jax.ShapeDtypeStruct((bsz * t, BRANCH_W), BF16),
        compiler_params=_cparams("parallel", "parallel"),
        name="na_lat",
    )(y, y, y, ck, cv, tables)


def _axial_rope_tables(n_tok):
    tok = jnp.arange(n_tok)
    n_freq = DK_B // 4
    inv = ROPE_BASE ** (-jnp.arange(n_freq, dtype=F32) / n_freq)
    ang = jnp.concatenate([(tok // GRID_W).astype(F32)[:, None] * inv,
                           (tok % GRID_W).astype(F32)[:, None] * inv], axis=-1)
    cos, sin = jnp.cos(ang), jnp.sin(ang)
    return jnp.tile(cos, (1, 4)), jnp.tile(jnp.concatenate([-sin, sin], axis=-1), (1, 2))


def _permute_w_in(w):
    sizes = (512, 512, 512, 512, 8, 8, 512, 512, 512, 512, 1024, 512, 512, 512, 512, 512, 4096)
    offs = [0]
    for s in sizes:
        offs.append(offs[-1] + s)
    seg = lambda i: w[:, offs[i]:offs[i + 1]]
    main = jnp.concatenate([seg(16)] + [seg(i) for i in (0, 1, 2, 3, 6, 7, 8, 9, 10, 11, 12, 13, 14, 15)], axis=1)
    gate = jnp.pad(jnp.concatenate([seg(4), seg(5)], axis=1), ((0, 0), (0, GATE_COLS - 4 * H_A)))
    return main.astype(BF16), gate.astype(BF16)


def _heads_major(y3, cb, n_heads, width):
    b, t, _ = y3.shape
    x = y3[:, :, cb * LANES: cb * LANES + n_heads * width].astype(F32)
    return x.reshape(b, t, n_heads, width).transpose(0, 2, 1, 3)


def kernel(x_prompt, x_sample, c, state_gdn, cache_diff_k, cache_diff_v, state_hgrn, cache_na_k, cache_na_v, c_ctx,
           w_ada, b_ada, norm1_g, w_in, gdn_conv_w, gdn_A_log, gdn_dt_bias, gdn_norm_g, diff_lambda, diff_norm_g,
           hgrn_lb_logits, hgrn_norm_g, na_rpb, w_branch, w_out, norm2_g, w_ffn_gate, w_ffn_up, w_ffn_down,
           final_norm_g):
    bp, tp, _ = x_prompt.shape
    bs, ts, _ = x_sample.shape
    past = cache_diff_k.shape[3]

    cond = jnp.zeros((16, D_MODEL), F32).at[:bs].set(c).at[bs].set(c_ctx)
    mods = _adaln(cond, w_ada, b_ada)
    probs = jax.nn.softmax(hgrn_lb_logits.astype(F32), axis=0)
    lb_all = jnp.cumsum(probs, axis=0) - probs[0:1]
    cos, sin_signed = _axial_rope_tables(ts)
    pair = lambda a: a.reshape(bs, DEPTH, H_D // 2, 2, past, DH_D).transpose(0, 1, 2, 4, 3, 5).reshape(
        bs, DEPTH, H_D // 2, past, LANES).astype(BF16)
    na_ck, na_cv = pair(cache_na_k), pair(cache_na_v)
    zeros_gdn = jnp.zeros((bp, 1, 2, H_A, DK_A, DV_A), F32)
    zeros_hgrn = jnp.zeros((bp, 1, 2, H_C, DK_C, DV_C), F32)

    xp = x_prompt.reshape(bp * tp, D_MODEL)
    xs = x_sample.reshape(bs * ts, D_MODEL)
    new_ctx = []
    for l in range(DEPTH):
        lam_init = 0.8 - 0.6 * math.exp(-0.3 * l)
        w_main, w_gate = _permute_w_in(w_in[l])
        conv_w = jnp.zeros((8, 3 * BRANCH_W), F32).at[:CONV_W].set(gdn_conv_w[l].T)
        lanes = lambda p: jnp.pad(p.astype(F32).reshape(1, 2 * H_A), ((0, 0), (0, LANES - 2 * H_A)))
        alog, dtb = lanes(gdn_A_log[l]), lanes(gdn_dt_bias[l])
        lb = lb_all[l].reshape(1, 2 * H_C * DK_C)
        vecs = dict(n1=norm1_g[l].reshape(1, D_MODEL), n2=norm2_g[l].reshape(1, D_MODEL),
                    fin=final_norm_g.reshape(1, D_MODEL), gdn=gdn_norm_g[l].reshape(1, LANES),
                    diff=diff_norm_g[l].reshape(1, LANES), hgrn=hgrn_norm_g[l].reshape(1, LANES))
        wbr, wo = w_branch[l].astype(BF16), w_out[l].astype(BF16)
        wfg, wfu, wfd = w_ffn_gate[l].astype(BF16), w_ffn_up[l].astype(BF16), w_ffn_down[l].astype(BF16)
        tables = _na_bias(na_rpb[l])
        final = l == DEPTH - 1

        def dense_tail(x, mod, y, branches, tok_per_row):
            x = _merge(x, mod, y, branches, wbr, wo, tok_per_row)
            return _ffn(x, mod, vecs["n2"], vecs["fin"], wfg, wfu, wfd, tok_per_row, final)

        mod = mods[l, bs:bs + 1].reshape(1, 1, 6 * D_MODEL)
        y, gates = _inproj(xp, mod, vecs["n1"], w_main, w_gate, bp * tp)
        o_a, st_gdn = _gdn(y, gates, conv_w, alog, dtb, vecs["gdn"], zeros_gdn, 0, bp, tp)
        o_b = _diff_ctx(y, diff_lambda[l], vecs["diff"], lam_init, bp, tp)
        o_c, st_hgrn = _hgrn(y, lb, vecs["hgrn"], zeros_hgrn, 0, bp, tp)
        o_d = _na_ctx(y, bp, tp)
        xp = dense_tail(xp, mod, y, (o_a, o_b, o_c, o_d), bp * tp)
        y3 = y.reshape(bp, tp, Y_COLS)
        new_ctx.append((st_gdn, _heads_major(y3, CB_BK, H_B, 2 * DK_B), _heads_major(y3, CB_BV, H_B, DV_B),
                        st_hgrn, _heads_major(y3, CB_DK, H_D, DH_D), _heads_major(y3, CB_DV, H_D, DH_D)))

        mod = mods[l, :bs].reshape(bs, 1, 6 * D_MODEL)
        y, gates = _inproj(xs, mod, vecs["n1"], w_main, w_gate, ts)
        o_a, _ = _gdn(y, gates, conv_w, alog, dtb, vecs["gdn"], state_gdn, l, bs, ts)
        o_b = _diff_lat(y, cache_diff_k, cache_diff_v, cos, sin_signed, diff_lambda[l], vecs["diff"], lam_init,
                        l, bs, ts)
        o_c, _ = _hgrn(y, lb, vecs["hgrn"], state_hgrn, l, bs, ts)
        o_d = _na_lat(y, na_ck, na_cv, tables, l, bs, ts)
        xs = dense_tail(xs, mod, y, (o_a, o_b, o_c, o_d), ts)

    stack = lambda i: jnp.stack([n[i] for n in new_ctx], axis=1)
    return (xp.reshape(bp, tp, D_MODEL), xs.reshape(bs, ts, D_MODEL), stack(0), stack(1), stack(2), stack(3),
            stack(4), stack(5))
```

```python
import functools
import math

import jax
import jax.numpy as jnp
import numpy as np
from jax import lax
from jax.experimental import pallas as pl
from jax.experimental.pallas import tpu as pltpu

F32 = jnp.float32
BF16 = jnp.bfloat16

D_MODEL = 1024
DEPTH = 2
GRID_W = 64
N_BRANCH = 4
H_A, DK_A, DV_A, CONV_W, CHUNK_A = 4, 128, 128, 5, 64
H_B, DK_B, DV_B = 4, 64, 128
H_C, DK_C, DV_C = 4, 128, 128
H_D, DH_D, WIN_R, WIN_C = 8, 64, 8, 16
BRANCH_W = 512
ROPE_BASE = 10000.0
D_FF = 2816
EPS = 1e-6
MASK_VALUE = -1e30
F_FLOOR = 1e-30
LOG2E = math.log2(math.e)
DIFF_QSCALE = DK_B ** -0.5 * LOG2E
NA_QSCALE = DH_D ** -0.5 * LOG2E

LANES = 128
VMEM_LIMIT_BYTES = 56 * 1024 * 1024

Y_COLS = 11776
CB_MG = 0
CB_AQ, CB_AK, CB_AV, CB_AZ = 32, 36, 40, 44
CB_BQ, CB_BK, CB_BV = 48, 52, 56
CB_CQ, CB_CFF, CB_CFB, CB_CI, CB_CG = 60, 64, 68, 72, 76
CB_DQ, CB_DK, CB_DV = 80, 84, 88
GATE_COLS = LANES

GDN_CHUNKS_PER_ITER = 4
CONV_PAD = 8
CONV_ROWS = 128
HGRN_CHUNK = 128
HGRN_UNROLL = 2
NA_QROWS = 4


def _cparams(*sem):
    return pltpu.CompilerParams(dimension_semantics=sem, vmem_limit_bytes=VMEM_LIMIT_BYTES)


def _dot(a, b):
    return jnp.dot(a.astype(BF16), b.astype(BF16), preferred_element_type=F32)


def _dot_t(a, b):
    return lax.dot_general(a.astype(BF16), b.astype(BF16), (((1,), (1,)), ((), ())), preferred_element_type=F32)


def _silu(x):
    return x * jax.nn.sigmoid(x)


def _rms(x, g):
    return x * lax.rsqrt(jnp.mean(x * x, axis=-1, keepdims=True) + EPS) * g


def _softmax_parts(s):
    m = jnp.max(s, axis=-1, keepdims=True)
    e = jnp.exp2(s - m)
    return e, 1.0 / jnp.sum(e, axis=-1, keepdims=True)


def _adaln_kernel(c_ref, w_ref, b_ref, o_ref):
    s = _silu(c_ref[...])
    o_ref[0] = jnp.dot(s, w_ref[0], precision=lax.Precision.HIGHEST, preferred_element_type=F32) + b_ref[0]


def _adaln(cond, w_ada, b_ada):
    rows = cond.shape[0]
    tn = 1024
    n_out = w_ada.shape[-1]
    return pl.pallas_call(
        _adaln_kernel,
        grid=(DEPTH, n_out // tn),
        in_specs=[pl.BlockSpec((rows, D_MODEL), lambda l, j: (0, 0)),
                  pl.BlockSpec((1, D_MODEL, tn), lambda l, j: (l, 0, j)),
                  pl.BlockSpec((1, 1, tn), lambda l, j: (l, 0, j))],
        out_specs=pl.BlockSpec((1, rows, tn), lambda l, j: (l, 0, j)),
        out_shape=jax.ShapeDtypeStruct((DEPTH, rows, n_out), F32),
        compiler_params=_cparams("parallel", "parallel"),
        name="adaln",
    )(cond, w_ada, b_ada.reshape(DEPTH, 1, n_out))


def _inproj_kernel(x_ref, sh_ref, sc_ref, g_ref, w_ref, wg_ref, y_ref, gate_ref, h_scr):
    @pl.when(pl.program_id(1) == 0)
    def _():
        h = _rms(x_ref[...], g_ref[...]) * (1.0 + sc_ref[0]) + sh_ref[0]
        hb = h.astype(BF16)
        h_scr[...] = hb
        gate_ref[...] = jnp.dot(hb, wg_ref[0], preferred_element_type=F32)

    y_ref[...] = jnp.dot(h_scr[...], w_ref[0], preferred_element_type=F32).astype(y_ref.dtype)


def _inproj(x, mod, norm_g, w, wg, layer, tok_per_row):
    m = x.shape[0]
    tm, tn = min(1024, m), Y_COLS // 4
    mrow = lambda c: pl.BlockSpec((1, 1, D_MODEL), lambda i, j: ((i * tm) // tok_per_row, 0, c))
    return pl.pallas_call(
        _inproj_kernel,
        grid=(m // tm, Y_COLS // tn),
        in_specs=[pl.BlockSpec((tm, D_MODEL), lambda i, j: (i, 0)),
                  mrow(0), mrow(1),
                  pl.BlockSpec((1, D_MODEL), lambda i, j: (0, 0)),
                  pl.BlockSpec((1, D_MODEL, tn), lambda i, j: (layer, 0, j)),
                  pl.BlockSpec((1, D_MODEL, GATE_COLS), lambda i, j: (layer, 0, 0))],
        out_specs=[pl.BlockSpec((tm, tn), lambda i, j: (i, j)),
                   pl.BlockSpec((tm, GATE_COLS), lambda i, j: (i, 0))],
        out_shape=[jax.ShapeDtypeStruct((m, Y_COLS), BF16),
                   jax.ShapeDtypeStruct((m, GATE_COLS), F32)],
        scratch_shapes=[pltpu.VMEM((tm, D_MODEL), BF16)],
        compiler_params=_cparams("parallel", "arbitrary"),
        name="inproj",
    )(x, mod, mod, norm_g, w, wg)


def _merge_kernel(x_ref, g1_ref, mg_ref, oa_ref, ob_ref, oc_ref, od_ref, wbr_ref, wout_ref, xo_ref):
    acc = None
    for n, o_ref in enumerate((oa_ref, ob_ref, oc_ref, od_ref)):
        proj = jnp.dot(o_ref[...], wbr_ref[0, n], preferred_element_type=F32)
        gate = jax.nn.sigmoid(mg_ref[:, n * D_MODEL:(n + 1) * D_MODEL].astype(F32))
        acc = gate * proj if acc is None else acc + gate * proj
    out = jnp.dot(acc.astype(BF16), wout_ref[0], preferred_element_type=F32)
    xo_ref[...] = x_ref[...] + g1_ref[0] * out


def _merge(x, mod, y, branches, w_branch, w_out, layer, tok_per_row):
    m = x.shape[0]
    tm = 512
    tok = lambda: pl.BlockSpec((tm, BRANCH_W), lambda i: (i, 0))
    return pl.pallas_call(
        _merge_kernel,
        grid=(m // tm,),
        in_specs=[pl.BlockSpec((tm, D_MODEL), lambda i: (i, 0)),
                  pl.BlockSpec((1, 1, D_MODEL), lambda i: ((i * tm) // tok_per_row, 0, 2)),
                  pl.BlockSpec((tm, N_BRANCH * D_MODEL), lambda i: (i, 0)),
                  tok(), tok(), tok(), tok(),
                  pl.BlockSpec((1, N_BRANCH, BRANCH_W, D_MODEL), lambda i: (layer, 0, 0, 0)),
                  pl.BlockSpec((1, D_MODEL, D_MODEL), lambda i: (layer, 0, 0))],
        out_specs=pl.BlockSpec((tm, D_MODEL), lambda i: (i, 0)),
        out_shape=jax.ShapeDtypeStruct((m, D_MODEL), F32),
        compiler_params=_cparams("parallel"),
        name="merge",
    )(x, mod, y, *branches, w_branch, w_out)


def _ffn_kernel(x_ref, sh_ref, sc_ref, g2_ref, ng_ref, fg_ref, wg_ref, wu_ref, wd_ref, xo_ref, h_scr, acc_scr,
                *, final):
    f = pl.program_id(1)

    @pl.when(f == 0)
    def _():
        h = _rms(x_ref[...], ng_ref[...]) * (1.0 + sc_ref[0]) + sh_ref[0]
        h_scr[...] = h.astype(BF16)
        acc_scr[...] = jnp.zeros_like(acc_scr)

    h = h_scr[...]
    a = jnp.dot(h, wg_ref[0], preferred_element_type=F32)
    u = jnp.dot(h, wu_ref[0], preferred_element_type=F32)
    acc_scr[...] += jnp.dot((_silu(a) * u).astype(BF16), wd_ref[0], preferred_element_type=F32)

    @pl.when(f == pl.num_programs(1) - 1)
    def _():
        xn = x_ref[...] + g2_ref[0] * acc_scr[...]
        xo_ref[...] = _rms(xn, fg_ref[...]) if final else xn


def _ffn(x, mod, norm_g, final_g, wg, wu, wd, layer, tok_per_row, final):
    m = x.shape[0]
    tm, tf = 512, D_FF // 2
    mrow = lambda c: pl.BlockSpec((1, 1, D_MODEL), lambda i, f: ((i * tm) // tok_per_row, 0, c))
    vec = lambda: pl.BlockSpec((1, D_MODEL), lambda i, f: (0, 0))
    return pl.pallas_call(
        functools.partial(_ffn_kernel, final=final),
        grid=(m // tm, D_FF // tf),
        in_specs=[pl.BlockSpec((tm, D_MODEL), lambda i, f: (i, 0)),
                  mrow(3), mrow(4), mrow(5), vec(), vec(),
                  pl.BlockSpec((1, D_MODEL, tf), lambda i, f: (layer, 0, f)),
                  pl.BlockSpec((1, D_MODEL, tf), lambda i, f: (layer, 0, f)),
                  pl.BlockSpec((1, tf, D_MODEL), lambda i, f: (layer, f, 0))],
        out_specs=pl.BlockSpec((tm, D_MODEL), lambda i, f: (i, 0)),
        out_shape=jax.ShapeDtypeStruct((m, D_MODEL), F32),
        scratch_shapes=[pltpu.VMEM((tm, D_MODEL), BF16), pltpu.VMEM((tm, D_MODEL), F32)],
        compiler_params=_cparams("parallel", "arbitrary"),
        name="ffn",
    )(x, mod, mod, mod, norm_g, final_g, wg, wu, wd)


def _shift_rows(x, off):
    n = x.shape[0]
    return x if off % n == 0 else pltpu.roll(x, (-off) % n, 0)


def _bdot(a, b):
    return jnp.einsum('bij,bjk->bik', a.astype(BF16), b.astype(BF16), preferred_element_type=F32)


def _bdot_t(a, b):
    return jnp.einsum('bik,bjk->bij', a.astype(BF16), b.astype(BF16), preferred_element_type=F32)


def _bdot_tl(a, b):
    return jnp.einsum('bki,bkj->bij', a.astype(BF16), b.astype(BF16), preferred_element_type=F32)


def _unit_tri_inverse(nmat, ii, jj):
    blk16 = (ii // 16) == (jj // 16)
    blk32 = (ii // 32) == (jj // 32)
    eye = (ii == jj).astype(F32)
    d = jnp.where(blk16, nmat, 0.0)
    x = eye - d
    p = _bdot(d, d)
    for _ in range(2):
        x = x + _bdot(x, p)
        p = _bdot(p, p)
    x = x + _bdot(x, p)
    for e in (jnp.where(blk32 & ~blk16, nmat, 0.0), jnp.where(~blk32, nmat, 0.0)):
        x = x - _bdot(x, _bdot(e, x))
    return x


def _gdn_prepare(q, k, v, gcb, rowgc, g_end, betab, sgn, ii, jj):
    c = q.shape[1]
    kb = k * betab
    gram_kk = _bdot_t(kb, k)
    gram_qk = _bdot_t(q, k)
    diff = gcb[:, :, :c] - rowgc
    order = (ii - jj) * sgn
    tri, strict = order >= 0, order > 0
    decay = jnp.where(tri, jnp.exp(jnp.where(tri, diff, 0.0)), 0.0)
    tinv = _unit_tri_inverse(jnp.where(strict, gram_kk * decay, 0.0), ii, jj)
    eg = jnp.exp(gcb)
    sol = _bdot(tinv, jnp.concatenate([v * betab, kb * eg], axis=2))
    a = jnp.where(tri, gram_qk * decay, 0.0)
    return sol[:, :, :DV_A], sol[:, :, DV_A:], a, q * eg, k * jnp.exp(g_end - gcb)


def _gdn_kernel(q_ref, k_ref, v_ref, z_ref, gate_ref, cw_ref, alog_ref, dtb_ref, ng_ref, s0_ref, o_ref, sfin_ref,
                xpad, qs, ks, vs, gc_scr, bt_scr, u_scr, w_scr, qd_scr, kd_scr, a_scr, o_scr, s_scr):
    t = q_ref.shape[0]
    n_chunks = t // CHUNK_A
    row = lax.broadcasted_iota(jnp.int32, (t, LANES), 0)
    lane = lax.broadcasted_iota(jnp.int32, (t, LANES), 1)

    xpad[0:CONV_PAD, :] = jnp.zeros((CONV_PAD, BRANCH_W), F32)
    xpad[t + CONV_PAD:t + 2 * CONV_PAD, :] = jnp.zeros((CONV_PAD, BRANCH_W), F32)

    rb = min(t, CONV_ROWS)
    for seg, (x_ref, dst) in enumerate(((q_ref, qs), (k_ref, ks), (v_ref, vs))):
        xpad[CONV_PAD:t + CONV_PAD, :] = x_ref[...].astype(F32)
        for h in range(H_A):
            lanes = slice(h * LANES, (h + 1) * LANES)
            for r in range(0, t, rb):
                acc = None
                for j in range(CONV_W):
                    r0 = CONV_PAD + r + j - CONV_W // 2
                    wj = cw_ref[j:j + 1, seg * BRANCH_W + h * LANES:seg * BRANCH_W + (h + 1) * LANES]
                    term = xpad[r0:r0 + rb, lanes] * wj
                    acc = term if acc is None else acc + term
                x = _silu(acc)
                if seg == 0:
                    x = x * lax.rsqrt(jnp.sum(x * x, axis=-1, keepdims=True) + EPS) * (DK_A ** -0.5)
                elif seg == 1:
                    x = x * lax.rsqrt(jnp.sum(x * x, axis=-1, keepdims=True) + EPS)
                dst[h, r:r + rb, :] = x

    gt = gate_ref[...]
    a = gt + dtb_ref[...]
    g = -jnp.exp(alog_ref[...]) * (jnp.maximum(a, 0.0) + jnp.log1p(jnp.exp(-jnp.abs(a))))
    pos = row % CHUNK_A
    pre, suf = g, g
    step = 1
    while step < CHUNK_A:
        pre = pre + jnp.where(pos >= step, _shift_rows(pre, -step), 0.0)
        suf = suf + jnp.where(pos < CHUNK_A - step, _shift_rows(suf, step), 0.0)
        step *= 2
    gc_scr[...] = jnp.where(lane < H_A, pre, suf)
    bt_scr[...] = jax.nn.sigmoid(gt)

    def chunk_rows(n, d):
        cn = n if d == 0 else n_chunks - 1 - n
        return pl.multiple_of(cn * CHUNK_A, CHUNK_A)

    def head_cols(x, first, rows):
        return jnp.stack([jnp.broadcast_to(x[:, first + h:first + h + 1], (rows, LANES)) for h in range(H_A)], axis=0)

    groups = [(j, d) for j in range(GDN_CHUNKS_PER_ITER) for d in range(2)]
    nb = len(groups) * H_A
    ii = lax.broadcasted_iota(jnp.int32, (nb, CHUNK_A, CHUNK_A), 1)
    jj = lax.broadcasted_iota(jnp.int32, (nb, CHUNK_A, CHUNK_A), 2)
    bb = lax.broadcasted_iota(jnp.int32, (nb, CHUNK_A, CHUNK_A), 0)
    sgn = 1 - 2 * ((bb // H_A) % 2)

    def prepare_body(i, carry):
        parts = {name: [] for name in ("q", "k", "v", "gcb", "rowgc", "g_end", "betab")}
        slices = []
        for j, d in groups:
            sl = pl.ds(chunk_rows(i * GDN_CHUNKS_PER_ITER + j, d), CHUNK_A)
            slices.append(sl)
            gch = gc_scr[sl, :]
            g_row = gch[CHUNK_A - 1:CHUNK_A, :] if d == 0 else gch[0:1, :]
            parts["q"].append(qs[:, sl, :])
            parts["k"].append(ks[:, sl, :])
            parts["v"].append(vs[:, sl, :])
            parts["gcb"].append(head_cols(gch, d * H_A, CHUNK_A))
            parts["rowgc"].append(jnp.broadcast_to(gch.T[d * H_A:(d + 1) * H_A][:, None, :],
                                                   (H_A, CHUNK_A, CHUNK_A)))
            parts["g_end"].append(head_cols(g_row, d * H_A, 1))
            parts["betab"].append(head_cols(bt_scr[sl, :], 2 * H_A + d * H_A, CHUNK_A))
        args = [jnp.concatenate(parts[name], axis=0) for name in ("q", "k", "v", "gcb", "rowgc", "g_end", "betab")]
        outs = _gdn_prepare(*args, sgn, ii, jj)
        for gi, ((j, d), sl) in enumerate(zip(groups, slices)):
            for scr, val in zip((u_scr, w_scr, a_scr, qd_scr, kd_scr), outs):
                scr[d * H_A:(d + 1) * H_A, sl, :] = val[gi * H_A:(gi + 1) * H_A].astype(BF16)
        return carry

    lax.fori_loop(0, n_chunks // GDN_CHUNKS_PER_ITER, prepare_body, 0)

    for d in range(2):
        for h in range(H_A):
            s_scr[d * H_A + h] = s0_ref[0, 0, d, h]

    def scan_body(n, carry):
        slices = [pl.ds(chunk_rows(n, d), CHUNK_A) for d in range(2)]
        both = lambda scr: jnp.concatenate([scr[d * H_A:(d + 1) * H_A, slices[d], :] for d in range(2)], axis=0)
        g_end = jnp.concatenate(
            [head_cols(gc_scr[pl.ds(chunk_rows(n, d) + (CHUNK_A - 1 if d == 0 else 0), 1), :], d * H_A, 1)
             for d in range(2)], axis=0)
        s = s_scr[...]
        v_new = both(u_scr).astype(F32) - _bdot(both(w_scr), s)
        o = _bdot(both(qd_scr), s) + _bdot(both(a_scr), v_new)
        s_scr[...] = s * jnp.exp(g_end) + _bdot_tl(both(kd_scr), v_new)
        for d in range(2):
            o_scr[d * H_A:(d + 1) * H_A, slices[d], :] = o[d * H_A:(d + 1) * H_A]
        return carry

    lax.fori_loop(0, n_chunks, scan_body, 0)

    for h in range(H_A):
        sl = slice(h * LANES, (h + 1) * LANES)
        sfin_ref[0, 0, h] = s_scr[h]
        sfin_ref[0, 1, h] = s_scr[H_A + h]
        o = o_scr[h] + o_scr[H_A + h]
        o_ref[:, sl] = (_rms(o, ng_ref[...]) * _silu(z_ref[:, sl].astype(F32))).astype(o_ref.dtype)


def _gdn(y, gates, conv_w, alog_lanes, dtb_lanes, norm_g, s0, layer, bsz, t):
    blk = lambda cb: pl.BlockSpec((t, BRANCH_W), lambda b: (b, cb // H_A))
    vec = lambda: pl.BlockSpec((1, LANES), lambda b: (0, 0))
    nhd = 2 * H_A
    return pl.pallas_call(
        _gdn_kernel,
        grid=(bsz,),
        in_specs=[blk(CB_AQ), blk(CB_AK), blk(CB_AV), blk(CB_AZ),
                  pl.BlockSpec((t, LANES), lambda b: (b, 0)),
                  pl.BlockSpec((8, 3 * BRANCH_W), lambda b: (0, 0)),
                  vec(), vec(), vec(),
                  pl.BlockSpec((1, 1, 2, H_A, DK_A, DV_A), lambda b: (b, layer, 0, 0, 0, 0))],
        out_specs=[pl.BlockSpec((t, BRANCH_W), lambda b: (b, 0)),
                   pl.BlockSpec((1, 2, H_A, DK_A, DV_A), lambda b: (b, 0, 0, 0, 0))],
        out_shape=[jax.ShapeDtypeStruct((bsz * t, BRANCH_W), BF16),
                   jax.ShapeDtypeStruct((bsz, 2, H_A, DK_A, DV_A), F32)],
        scratch_shapes=[pltpu.VMEM((t + 2 * CONV_PAD, BRANCH_W), F32),
                        pltpu.VMEM((H_A, t, LANES), F32), pltpu.VMEM((H_A, t, LANES), F32),
                        pltpu.VMEM((H_A, t, LANES), F32),
                        pltpu.VMEM((t, LANES), F32), pltpu.VMEM((t, LANES), F32),
                        pltpu.VMEM((nhd, t, DV_A), BF16), pltpu.VMEM((nhd, t, DK_A), BF16),
                        pltpu.VMEM((nhd, t, DK_A), BF16), pltpu.VMEM((nhd, t, DK_A), BF16),
                        pltpu.VMEM((nhd, t, CHUNK_A), BF16),
                        pltpu.VMEM((nhd, t, DV_A), F32), pltpu.VMEM((nhd, DK_A, DV_A), F32)],
        compiler_params=_cparams("parallel"),
        name="gdn",
    )(y, y, y, y, gates, conv_w, alog_lanes, dtb_lanes, norm_g, s0)


def _boundary_rows(bc, w, fwd):
    c = bc.shape[0]
    if w >= 4:
        pieces = []
        for start in range(0, c, 2 * w):
            r = start + w - 1 if fwd else start + w
            pieces.append(jnp.broadcast_to(bc[r:r + 1, :], (2 * w, LANES)))
        return pieces[0] if len(pieces) == 1 else jnp.concatenate(pieces, axis=0)
    pos = lax.broadcasted_iota(jnp.int32, (c, LANES), 0) % (2 * w)
    target = w - 1 if fwd else w
    out = bc
    for p in range(2 * w):
        if p != target:
            out = jnp.where(pos == p, _shift_rows(bc, target - p), out)
    return out


def _hgrn_levels():
    w = HGRN_CHUNK // 2
    while w >= 1:
        yield w
        w //= 2


def _hgrn_constants():
    idx = np.arange(HGRN_CHUNK)
    t, s = idx[:, None], idx[None, :]
    tri = np.stack([t >= s, t <= s]).astype(np.float32)
    pair = []
    for w in _hgrn_levels():
        same = (t // (2 * w)) == (s // (2 * w))
        upper_t, upper_s = (t % (2 * w)) >= w, (s % (2 * w)) >= w
        pair.append(np.stack([same & upper_t & ~upper_s, same & ~upper_t & upper_s]))
    return jnp.asarray(tri, BF16), jnp.asarray(np.stack(pair, axis=1).astype(np.float32))


def _hgrn_tiles(q, k, v, logf, st, tri, pair_ref):
    c = q.shape[1]
    bc = None
    rem = logf
    for _ in range(3):
        part = rem.astype(BF16)
        rem = rem - part.astype(F32)
        r = jnp.einsum('bij,bjk->bik', tri, part, preferred_element_type=F32)
        bc = r if bc is None else bc + r
    nb = q.shape[0]
    q_lo, k_lo = q.astype(BF16), k.astype(BF16)
    amat = None
    for lvl, w in enumerate(_hgrn_levels()):
        bm = jnp.stack([_boundary_rows(bc[i], w, i % 2 == 0) for i in range(nb)], axis=0)
        e = jnp.exp2(jnp.abs(bc - bm) * (-LOG2E)).astype(BF16)
        pair = jnp.concatenate([pair_ref[:, lvl]] * (nb // 2), axis=0)
        g = _bdot_t(q_lo * e, k_lo * e) * pair
        amat = g if amat is None else amat + g
    o = _bdot(amat, v) + jnp.sum(q * k, axis=-1, keepdims=True) * v
    tot = jnp.stack([bc[i, c - 1:c, :] if i % 2 == 0 else bc[i, 0:1, :] for i in range(nb)], axis=0)
    inc = _bdot_tl(v, k * jnp.exp(tot - bc))
    decay = jnp.exp(tot)
    states = [st]
    for j in range(nb // 2):
        states.append(states[-1] * decay[2 * j:2 * j + 2] + inc[2 * j:2 * j + 2])
    o = o + _bdot_t(q * jnp.exp(bc), jnp.concatenate(states[:-1], axis=0))
    return o, states[-1]


def _hgrn_kernel(q_ref, ff_ref, fr_ref, i_ref, g_ref, lbf_ref, lbr_ref, ng_ref, s0_ref, tri_ref, pair_ref,
                 o_ref, sfin_ref, of_scr, or_scr):
    t = q_ref.shape[0]
    n_tiles = t // HGRN_CHUNK
    tri = jnp.concatenate([tri_ref[...]] * HGRN_UNROLL, axis=0)

    def load(sl, f_ref, lb_ref):
        lb = lb_ref[...]
        f = jnp.maximum(lb + (1.0 - lb) * jax.nn.sigmoid(f_ref[sl, :].astype(F32)), F_FLOOR)
        return _silu(q_ref[sl, :].astype(F32)), 1.0 - f, i_ref[sl, :].astype(F32), jnp.log(f)

    def body(i, st):
        slices, loaded = [], []
        for j in range(HGRN_UNROLL):
            n = i * HGRN_UNROLL + j
            slf = pl.ds(pl.multiple_of(n * HGRN_CHUNK, HGRN_CHUNK), HGRN_CHUNK)
            slr = pl.ds(pl.multiple_of((n_tiles - 1 - n) * HGRN_CHUNK, HGRN_CHUNK), HGRN_CHUNK)
            slices += [slf, slr]
            loaded += [load(slf, ff_ref, lbf_ref), load(slr, fr_ref, lbr_ref)]
        o, st = _hgrn_tiles(*[jnp.stack(parts, axis=0) for parts in zip(*loaded)], st, tri, pair_ref)
        for idx, sl in enumerate(slices):
            (of_scr if idx % 2 == 0 else or_scr)[sl, :] = o[idx]
        return st

    st = lax.fori_loop(0, n_tiles // HGRN_UNROLL, body,
                       jnp.stack([s0_ref[0, 0, 0, 0].T, s0_ref[0, 0, 1, 0].T], axis=0))
    sfin_ref[0, 0, 0] = st[0].T
    sfin_ref[0, 1, 0] = st[1].T
    o = of_scr[...] + or_scr[...]
    o_ref[...] = (_rms(o, ng_ref[...]) * _silu(g_ref[...].astype(F32))).astype(o_ref.dtype)


def _hgrn(y, lb, norm_g, s0, layer, bsz, t):
    col = lambda cb: pl.BlockSpec((t, LANES), lambda b, h: (b, cb + h))
    tri, pair = _hgrn_constants()
    return pl.pallas_call(
        _hgrn_kernel,
        grid=(bsz, H_C),
        in_specs=[col(CB_CQ), col(CB_CFF), col(CB_CFB), col(CB_CI), col(CB_CG),
                  pl.BlockSpec((1, LANES), lambda b, h: (0, h)),
                  pl.BlockSpec((1, LANES), lambda b, h: (0, H_C + h)),
                  pl.BlockSpec((1, LANES), lambda b, h: (0, 0)),
                  pl.BlockSpec((1, 1, 2, 1, DK_C, DV_C), lambda b, h: (b, layer, 0, h, 0, 0)),
                  pl.BlockSpec(tri.shape, lambda b, h: (0, 0, 0)),
                  pl.BlockSpec(pair.shape, lambda b, h: (0, 0, 0, 0))],
        out_specs=[pl.BlockSpec((t, LANES), lambda b, h: (b, h)),
                   pl.BlockSpec((1, 2, 1, DK_C, DV_C), lambda b, h: (b, 0, h, 0, 0))],
        out_shape=[jax.ShapeDtypeStruct((bsz * t, BRANCH_W), BF16),
                   jax.ShapeDtypeStruct((bsz, 2, H_C, DK_C, DV_C), F32)],
        scratch_shapes=[pltpu.VMEM((t, LANES), F32), pltpu.VMEM((t, LANES), F32)],
        compiler_params=_cparams("parallel", "parallel"),
        name="hgrn",
    )(y, y, y, y, y, lb, lb, norm_g, s0, tri, pair)


def _diff_lambda(lam_ref, lam_init):
    lp = lam_ref[...]
    return (jnp.exp(jnp.sum(lp[0:1] * lp[1:2], axis=-1, keepdims=True))
            - jnp.exp(jnp.sum(lp[2:3] * lp[3:4], axis=-1, keepdims=True)) + lam_init)


def _diff_core(problems, lam, ng, lam_init):
    scores = []
    for q, k, _ in problems:
        lane = lax.broadcasted_iota(jnp.int32, q.shape, 1)
        zero = jnp.zeros_like(q)
        scores.append((_dot_t(jnp.where(lane < DK_B, q, zero), k), _dot_t(jnp.where(lane >= DK_B, q, zero), k)))
    outs = []
    for (s1, s2), (_, _, v) in zip(scores, problems):
        e1, r1 = _softmax_parts(s1)
        e2, r2 = _softmax_parts(s2)
        outs.append(_dot(e1, v) * r1 - _dot(e2, v) * (lam * r2))
    return [_rms(o, ng) * (1.0 - lam_init) for o in outs]


def _diff_ctx_kernel(lam_ref, q_ref, k_ref, v_ref, ng_ref, o_ref, *, lam_init):
    lam = _diff_lambda(lam_ref, lam_init)
    heads = [slice(h * LANES, (h + 1) * LANES) for h in range(H_B)]
    outs = _diff_core([((q_ref[:, sl].astype(F32) * DIFF_QSCALE).astype(BF16), k_ref[:, sl], v_ref[:, sl])
                       for sl in heads], lam, ng_ref[...], lam_init)
    for sl, o in zip(heads, outs):
        o_ref[:, sl] = o.astype(o_ref.dtype)


def _diff_ctx(y, lam_p, norm_g, lam_init, bsz, t):
    blk = lambda cb: pl.BlockSpec((t, BRANCH_W), lambda b: (b, cb // H_B))
    return pl.pallas_call(
        functools.partial(_diff_ctx_kernel, lam_init=lam_init),
        grid=(bsz,),
        in_specs=[pl.BlockSpec((4, DK_B), lambda b: (0, 0)), blk(CB_BQ), blk(CB_BK), blk(CB_BV),
                  pl.BlockSpec((1, LANES), lambda b: (0, 0))],
        out_specs=pl.BlockSpec((t, BRANCH_W), lambda b: (b, 0)),
        out_shape=jax.ShapeDtypeStruct((bsz * t, BRANCH_W), BF16),
        compiler_params=_cparams("parallel"),
        name="diff_ctx",
    )(lam_p, y, y, y, norm_g)


def _rope(x, cos, sin_signed):
    lane = lax.broadcasted_iota(jnp.int32, x.shape, 1)
    rot = jnp.where((lane % DK_B) < DK_B // 2, pltpu.roll(x, LANES - DK_B // 2, 1), pltpu.roll(x, DK_B // 2, 1))
    return x * cos + rot * sin_signed


def _diff_lat_kernel(lam_ref, q_ref, k_ref, v_ref, ck_ref, cv_ref, cosq_ref, sinq_ref, cos_ref, sin_ref, ng_ref,
                     o_ref, k_scr, v_scr, *, lam_init):
    past = ck_ref.shape[3]

    @pl.when(pl.program_id(2) == 0)
    def _():
        k_scr[:past, :] = ck_ref[0, 0, 0].astype(BF16)
        v_scr[:past, :] = cv_ref[0, 0, 0].astype(BF16)
        k_scr[past:, :] = _rope(k_ref[...].astype(F32), cos_ref[...], sin_ref[...]).astype(BF16)
        v_scr[past:, :] = v_ref[...]

    lam = _diff_lambda(lam_ref, lam_init)
    q = (_rope(q_ref[...].astype(F32), cosq_ref[...], sinq_ref[...]) * DIFF_QSCALE).astype(BF16)
    k, v = k_scr[...], v_scr[...]
    half = q.shape[0] // 2
    outs = _diff_core([(q[:half], k, v), (q[half:], k, v)], lam, ng_ref[...], lam_init)
    o_ref[:half, :] = outs[0].astype(o_ref.dtype)
    o_ref[half:, :] = outs[1].astype(o_ref.dtype)


def _diff_lat(y, ck, cv, cos, sin_signed, lam_p, norm_g, lam_init, layer, bsz, t):
    tq = 512
    nq = t // tq
    past = ck.shape[3]
    full = lambda cb: pl.BlockSpec((t, LANES), lambda b, h, i: (b, cb + h))
    cache = lambda: pl.BlockSpec((1, 1, 1, past, LANES), lambda b, h, i: (b, layer, h, 0, 0))
    tab_q = lambda: pl.BlockSpec((tq, LANES), lambda b, h, i: (i, 0))
    tab = lambda: pl.BlockSpec((t, LANES), lambda b, h, i: (0, 0))
    return pl.pallas_call(
        functools.partial(_diff_lat_kernel, lam_init=lam_init),
        grid=(bsz, H_B, nq),
        in_specs=[pl.BlockSpec((4, DK_B), lambda b, h, i: (0, 0)),
                  pl.BlockSpec((tq, LANES), lambda b, h, i: (b * nq + i, CB_BQ + h)),
                  full(CB_BK), full(CB_BV), cache(), cache(), tab_q(), tab_q(), tab(), tab(),
                  pl.BlockSpec((1, LANES), lambda b, h, i: (0, 0))],
        out_specs=pl.BlockSpec((tq, LANES), lambda b, h, i: (b * nq + i, h)),
        out_shape=jax.ShapeDtypeStruct((bsz * t, BRANCH_W), BF16),
        scratch_shapes=[pltpu.VMEM((past + t, LANES), BF16), pltpu.VMEM((past + t, LANES), BF16)],
        compiler_params=_cparams("parallel", "parallel", "arbitrary"),
        name="diff_lat",
    )(lam_p, y, y, y, ck, cv, cos, sin_signed, cos, sin_signed, norm_g)


def _pair_attend(q, keys, vals, bias=None):
    lane = lax.broadcasted_iota(jnp.int32, q.shape, 1)
    zero = jnp.zeros_like(q)
    scores = [[_dot_t(jnp.where((lane < DH_D) == (par == 0), q, zero), kk) for kk in keys] for par in range(2)]
    outs = []
    for par in range(2):
        s = scores[par]
        if bias is not None:
            s[-1] = jnp.where(bias[par] > 0.5 * MASK_VALUE, s[-1] + bias[par], MASK_VALUE)
        m = s[0].max(axis=-1, keepdims=True)
        for x in s[1:]:
            m = jnp.maximum(m, x.max(axis=-1, keepdims=True))
        e = [jnp.exp2(x - m) for x in s]
        den = e[0].sum(axis=-1, keepdims=True)
        for x in e[1:]:
            den = den + x.sum(axis=-1, keepdims=True)
        o = _dot(e[0], vals[0])
        for x, vv in zip(e[1:], vals[1:]):
            o = o + _dot(x, vv)
        outs.append(o * (1.0 / den))
    return jnp.where(lane < DH_D, outs[0], outs[1])


def _na_ctx_kernel(q_ref, k_ref, v_ref, o_ref):
    for p in range(H_D // 2):
        sl = slice(p * LANES, (p + 1) * LANES)
        q = (q_ref[:, sl].astype(F32) * NA_QSCALE).astype(BF16)
        o_ref[:, sl] = _pair_attend(q, [k_ref[:, sl]], [v_ref[:, sl]]).astype(o_ref.dtype)


def _na_ctx(y, bsz, t):
    blk = lambda cb: pl.BlockSpec((t, BRANCH_W), lambda b: (b, cb // 4))
    return pl.pallas_call(
        _na_ctx_kernel,
        grid=(bsz,),
        in_specs=[blk(CB_DQ), blk(CB_DK), blk(CB_DV)],
        out_specs=pl.BlockSpec((t, BRANCH_W), lambda b: (b, 0)),
        out_shape=jax.ShapeDtypeStruct((bsz * t, BRANCH_W), BF16),
        compiler_params=_cparams("parallel"),
        name="na_ctx",
    )(y, y, y)


def _na_bias_kernel(rpb_ref, o_ref):
    hd = pl.program_id(0)
    qc = lax.broadcasted_iota(jnp.int32, (GRID_W, LANES), 0)
    lane = lax.broadcasted_iota(jnp.int32, (GRID_W, LANES), 1)
    kc = lane % GRID_W
    dc = jnp.clip(kc - qc + WIN_C - 1, 0, 2 * WIN_C - 2)
    c0 = jnp.clip(qc - WIN_C // 2, 0, GRID_W - WIN_C)
    col_ok = (kc >= c0) & (kc < c0 + WIN_C)
    n_dr = 2 * WIN_R - 1
    tables = []
    for dr in range(n_dr):
        acc = jnp.zeros((GRID_W, LANES), F32)
        for d in range(2 * WIN_C - 1):
            acc = jnp.where(dc == d, rpb_ref[hd, dr * (2 * WIN_C - 1) + d], acc)
        tables.append(jnp.where(col_ok, acc * LOG2E, MASK_VALUE))
    masked = jnp.full((GRID_W, LANES), MASK_VALUE, F32)
    for i in range(n_dr + 1):
        lo = tables[i - 1] if i >= 1 else masked
        hi = tables[i] if i < n_dr else masked
        o_ref[0, i] = jnp.where(lane < GRID_W, lo, hi)


def _na_bias(rpb):
    n_dr = 2 * WIN_R - 1
    return pl.pallas_call(
        _na_bias_kernel,
        grid=(H_D,),
        in_specs=[pl.BlockSpec(memory_space=pltpu.SMEM)],
        out_specs=pl.BlockSpec((1, n_dr + 1, GRID_W, LANES), lambda h: (h, 0, 0, 0)),
        out_shape=jax.ShapeDtypeStruct((H_D, n_dr + 1, GRID_W, LANES), F32),
        compiler_params=_cparams("parallel"),
        name="na_bias",
    )(rpb.reshape(H_D, n_dr * (2 * WIN_C - 1)))


def _na_lat_kernel(q_ref, k_ref, v_ref, ck_ref, cv_ref, tab_ref, o_ref, *, rows):
    wr = min(WIN_R, rows)
    lane = lax.broadcasted_iota(jnp.int32, (GRID_W, LANES), 1)
    masked = jnp.full((GRID_W, LANES), MASK_VALUE, F32)
    ck, cv = ck_ref[0, 0, 0], cv_ref[0, 0, 0]
    first_row = lambda r: min(max(r - wr // 2, 0), rows - wr)
    for g in range(rows // NA_QROWS):
        starts = [first_row(g * NA_QROWS + qi) for qi in range(NA_QROWS)]
        kp0, kp1 = min(starts) // 2, (max(starts) + wr + 1) // 2
        k_lat = k_ref[2 * kp0 * GRID_W:2 * kp1 * GRID_W, :]
        v_lat = v_ref[2 * kp0 * GRID_W:2 * kp1 * GRID_W, :]
        bias = []
        for par in range(2):
            blocks = []
            for qi in range(NA_QROWS):
                r = g * NA_QROWS + qi
                r0 = first_row(r)
                tiles = []
                for kp in range(kp0, kp1):
                    ok0 = r0 <= 2 * kp < r0 + wr
                    ok1 = r0 <= 2 * kp + 1 < r0 + wr
                    if not (ok0 or ok1):
                        tiles.append(masked)
                        continue
                    tile = tab_ref[par, 2 * kp - r + WIN_R]
                    if not ok0:
                        tile = jnp.where(lane >= GRID_W, tile, MASK_VALUE)
                    if not ok1:
                        tile = jnp.where(lane < GRID_W, tile, MASK_VALUE)
                    tiles.append(tile)
                blocks.append(jnp.concatenate(tiles, axis=1))
            bias.append(jnp.concatenate(blocks, axis=0))
        sl = slice(g * NA_QROWS * GRID_W, (g + 1) * NA_QROWS * GRID_W)
        q = (q_ref[sl, :].astype(F32) * NA_QSCALE).astype(BF16)
        o_ref[sl, :] = _pair_attend(q, [ck, k_lat], [cv, v_lat], bias).astype(o_ref.dtype)


def _na_lat(y, ck, cv, tables, layer, bsz, t):
    rows = t // GRID_W
    past = ck.shape[3]
    n_tab = tables.shape[1]
    col = lambda cb: pl.BlockSpec((t, LANES), lambda b, p: (b, cb + p))
    cache = lambda: pl.BlockSpec((1, 1, 1, past, LANES), lambda b, p: (b, layer, p, 0, 0))
    return pl.pallas_call(
        functools.partial(_na_lat_kernel, rows=rows),
        grid=(bsz, H_D // 2),
        in_specs=[col(CB_DQ), col(CB_DK), col(CB_DV), cache(), cache(),
                  pl.BlockSpec((2, n_tab, GRID_W, LANES), lambda b, p: (p, 0, 0, 0))],
        out_specs=pl.BlockSpec((t, LANES), lambda b, p: (b, p)),
        out_shape=jax.ShapeDtypeStruct((bsz * t, BRANCH_W), BF16),
        compiler_params=_cparams("parallel", "parallel"),
        name="na_lat",
    )(y, y, y, ck, cv, tables)


def _axial_rope_tables(n_tok):
    tok = jnp.arange(n_tok)
    n_freq = DK_B // 4
    inv = ROPE_BASE ** (-jnp.arange(n_freq, dtype=F32) / n_freq)
    ang = jnp.concatenate([(tok // GRID_W).astype(F32)[:, None] * inv,
                           (tok % GRID_W).astype(F32)[:, None] * inv], axis=-1)
    cos, sin = jnp.cos(ang), jnp.sin(ang)
    return jnp.tile(cos, (1, 4)), jnp.tile(jnp.concatenate([-sin, sin], axis=-1), (1, 2))


def _permute_w_in(w):
    a_end = 4 * BRANCH_W
    g_end = a_end + 4 * H_A
    mg_start = w.shape[-1] - N_BRANCH * D_MODEL
    main = jnp.concatenate([w[..., mg_start:], w[..., :a_end], w[..., g_end:mg_start]], axis=-1)
    gate = jnp.pad(w[..., a_end:g_end], ((0, 0), (0, 0), (0, GATE_COLS - 4 * H_A)))
    return main.astype(BF16), gate.astype(BF16)


def _heads_major(y3, cb, n_heads, width):
    b, t, _ = y3.shape
    x = y3[:, :, cb * LANES: cb * LANES + n_heads * width].astype(F32)
    return x.reshape(b, t, n_heads, width).transpose(0, 2, 1, 3)


def kernel(x_prompt, x_sample, c, state_gdn, cache_diff_k, cache_diff_v, state_hgrn, cache_na_k, cache_na_v, c_ctx,
           w_ada, b_ada, norm1_g, w_in, gdn_conv_w, gdn_A_log, gdn_dt_bias, gdn_norm_g, diff_lambda, diff_norm_g,
           hgrn_lb_logits, hgrn_norm_g, na_rpb, w_branch, w_out, norm2_g, w_ffn_gate, w_ffn_up, w_ffn_down,
           final_norm_g):
    bp, tp, _ = x_prompt.shape
    bs, ts, _ = x_sample.shape
    past = cache_diff_k.shape[3]

    cond = jnp.zeros((16, D_MODEL), F32).at[:bs].set(c).at[bs].set(c_ctx)
    mods = _adaln(cond, w_ada, b_ada)
    probs = jax.nn.softmax(hgrn_lb_logits.astype(F32), axis=0)
    lb_all = jnp.cumsum(probs, axis=0) - probs[0:1]
    cos, sin_signed = _axial_rope_tables(ts)
    pair = lambda a: a.reshape(bs, DEPTH, H_D // 2, 2, past, DH_D).transpose(0, 1, 2, 4, 3, 5).reshape(
        bs, DEPTH, H_D // 2, past, LANES).astype(BF16)
    na_ck, na_cv = pair(cache_na_k), pair(cache_na_v)
    zeros_gdn = jnp.zeros((bp, 1, 2, H_A, DK_A, DV_A), F32)
    zeros_hgrn = jnp.zeros((bp, 1, 2, H_C, DK_C, DV_C), F32)
    w_main, w_gate = _permute_w_in(w_in)
    wbr, wo = w_branch.astype(BF16), w_out.astype(BF16)
    wfg, wfu, wfd = w_ffn_gate.astype(BF16), w_ffn_up.astype(BF16), w_ffn_down.astype(BF16)

    xp = x_prompt.reshape(bp * tp, D_MODEL)
    xs = x_sample.reshape(bs * ts, D_MODEL)
    new_ctx = []
    for l in range(DEPTH):
        lam_init = 0.8 - 0.6 * math.exp(-0.3 * l)
        conv_w = jnp.zeros((8, 3 * BRANCH_W), F32).at[:CONV_W].set(gdn_conv_w[l].T)
        lanes = lambda p: jnp.pad(p.astype(F32).reshape(1, 2 * H_A), ((0, 0), (0, LANES - 2 * H_A)))
        alog, dtb = lanes(gdn_A_log[l]), lanes(gdn_dt_bias[l])
        lb = lb_all[l].reshape(1, 2 * H_C * DK_C)
        vecs = dict(n1=norm1_g[l].reshape(1, D_MODEL), n2=norm2_g[l].reshape(1, D_MODEL),
                    fin=final_norm_g.reshape(1, D_MODEL), gdn=gdn_norm_g[l].reshape(1, LANES),
                    diff=diff_norm_g[l].reshape(1, LANES), hgrn=hgrn_norm_g[l].reshape(1, LANES))
        tables = _na_bias(na_rpb[l])
        final = l == DEPTH - 1

        def dense_tail(x, mod, y, branches, tok_per_row):
            x = _merge(x, mod, y, branches, wbr, wo, l, tok_per_row)
            return _ffn(x, mod, vecs["n2"], vecs["fin"], wfg, wfu, wfd, l, tok_per_row, final)

        mod = mods[l, bs:bs + 1].reshape(1, 1, 6 * D_MODEL)
        y, gates = _inproj(xp, mod, vecs["n1"], w_main, w_gate, l, bp * tp)
        o_a, st_gdn = _gdn(y, gates, conv_w, alog, dtb, vecs["gdn"], zeros_gdn, 0, bp, tp)
        o_b = _diff_ctx(y, diff_lambda[l], vecs["diff"], lam_init, bp, tp)
        o_c, st_hgrn = _hgrn(y, lb, vecs["hgrn"], zeros_hgrn, 0, bp, tp)
        o_d = _na_ctx(y, bp, tp)
        xp = dense_tail(xp, mod, y, (o_a, o_b, o_c, o_d), bp * tp)
        y3 = y.reshape(bp, tp, Y_COLS)
        new_ctx.append((st_gdn, _heads_major(y3, CB_BK, H_B, 2 * DK_B), _heads_major(y3, CB_BV, H_B, DV_B),
                        st_hgrn, _heads_major(y3, CB_DK, H_D, DH_D), _heads_major(y3, CB_DV, H_D, DH_D)))

        mod = mods[l, :bs].reshape(bs, 1, 6 * D_MODEL)
        y, gates = _inproj(xs, mod, vecs["n1"], w_main, w_gate, l, ts)
        o_a, _ = _gdn(y, gates, conv_w, alog, dtb, vecs["gdn"], state_gdn, l, bs, ts)
        o_b = _diff_lat(y, cache_diff_k, cache_diff_v, cos, sin_signed, diff_lambda[l], vecs["diff"], lam_init,
                        l, bs, ts)
        o_c, _ = _hgrn(y, lb, vecs["hgrn"], state_hgrn, l, bs, ts)
        o_d = _na_lat(y, na_ck, na_cv, tables, l, bs, ts)
        xs = dense_tail(xs, mod, y, (o_a, o_b, o_c, o_d), ts)

    stack = lambda i: jnp.stack([n[i] for n in new_ctx], axis=1)
    return (xp.reshape(bp, tp, D_MODEL), xs.reshape(bs, ts, D_MODEL), stack(0), stack(1), stack(2), stack(3),
            stack(4), stack(5))
```

```python
import functools
import math

import jax
import jax.numpy as jnp
import numpy as np
from jax import lax
from jax.experimental import pallas as pl
from jax.experimental.pallas import tpu as pltpu

F32 = jnp.float32
BF16 = jnp.bfloat16

D_MODEL = 1024
DEPTH = 2
GRID_W = 64
N_BRANCH = 4
H_A, DK_A, DV_A, CONV_W, CHUNK_A = 4, 128, 128, 5, 64
H_B, DK_B, DV_B = 4, 64, 128
H_C, DK_C, DV_C = 4, 128, 128
H_D, DH_D, WIN_R, WIN_C = 8, 64, 8, 16
BRANCH_W = 512
ROPE_BASE = 10000.0
D_FF = 2816
EPS = 1e-6
MASK_VALUE = -1e30
F_FLOOR = 1e-30
LOG2E = math.log2(math.e)
DIFF_QSCALE = DK_B ** -0.5 * LOG2E
NA_QSCALE = DH_D ** -0.5 * LOG2E

LANES = 128
VMEM_LIMIT_BYTES = 56 * 1024 * 1024

Y_COLS = 11776
CB_MG = 0
CB_AQ, CB_AK, CB_AV, CB_AZ = 32, 36, 40, 44
CB_BQ, CB_BK, CB_BV = 48, 52, 56
CB_CQ, CB_CFF, CB_CFB, CB_CI, CB_CG = 60, 64, 68, 72, 76
CB_DQ, CB_DK, CB_DV = 80, 84, 88
GATE_COLS = LANES

GDN_CHUNKS_PER_ITER = 4
CONV_PAD = 8
CONV_ROWS = 128
HGRN_CHUNK = 128
HGRN_UNROLL = 2
NA_QROWS = 4


def _cparams(*sem):
    return pltpu.CompilerParams(dimension_semantics=sem, vmem_limit_bytes=VMEM_LIMIT_BYTES)


def _dot(a, b):
    return jnp.dot(a.astype(BF16), b.astype(BF16), preferred_element_type=F32)


def _dot_t(a, b):
    return lax.dot_general(a.astype(BF16), b.astype(BF16), (((1,), (1,)), ((), ())), preferred_element_type=F32)


def _silu(x):
    return x * jax.nn.sigmoid(x)


def _rms(x, g):
    return x * lax.rsqrt(jnp.mean(x * x, axis=-1, keepdims=True) + EPS) * g


def _softmax_parts(s):
    m = jnp.max(s, axis=-1, keepdims=True)
    e = jnp.exp2(s - m)
    return e, 1.0 / jnp.sum(e, axis=-1, keepdims=True)


def _adaln_kernel(c_ref, w_ref, b_ref, o_ref):
    s = _silu(c_ref[...])
    o_ref[0] = jnp.dot(s, w_ref[0], precision=lax.Precision.HIGHEST, preferred_element_type=F32) + b_ref[0]


def _adaln(cond, w_ada, b_ada):
    rows = cond.shape[0]
    tn = 1024
    n_out = w_ada.shape[-1]
    return pl.pallas_call(
        _adaln_kernel,
        grid=(DEPTH, n_out // tn),
        in_specs=[pl.BlockSpec((rows, D_MODEL), lambda l, j: (0, 0)),
                  pl.BlockSpec((1, D_MODEL, tn), lambda l, j: (l, 0, j)),
                  pl.BlockSpec((1, 1, tn), lambda l, j: (l, 0, j))],
        out_specs=pl.BlockSpec((1, rows, tn), lambda l, j: (l, 0, j)),
        out_shape=jax.ShapeDtypeStruct((DEPTH, rows, n_out), F32),
        compiler_params=_cparams("parallel", "parallel"),
        name="adaln",
    )(cond, w_ada, b_ada.reshape(DEPTH, 1, n_out))


def _inproj_kernel(x_ref, sh_ref, sc_ref, g_ref, w_ref, wg_ref, y_ref, gate_ref, h_scr):
    @pl.when(pl.program_id(1) == 0)
    def _():
        h = _rms(x_ref[...], g_ref[...]) * (1.0 + sc_ref[0]) + sh_ref[0]
        hb = h.astype(BF16)
        h_scr[...] = hb
        gate_ref[...] = jnp.dot(hb, wg_ref[0], preferred_element_type=F32)

    y_ref[...] = jnp.dot(h_scr[...], w_ref[0], preferred_element_type=F32).astype(y_ref.dtype)


def _inproj(x, mod, norm_g, w, wg, layer, tok_per_row):
    m = x.shape[0]
    tm, tn = min(1024, m), Y_COLS // 4
    mrow = lambda c: pl.BlockSpec((1, 1, D_MODEL), lambda i, j: ((i * tm) // tok_per_row, 0, c))
    return pl.pallas_call(
        _inproj_kernel,
        grid=(m // tm, Y_COLS // tn),
        in_specs=[pl.BlockSpec((tm, D_MODEL), lambda i, j: (i, 0)),
                  mrow(0), mrow(1),
                  pl.BlockSpec((1, D_MODEL), lambda i, j: (0, 0)),
                  pl.BlockSpec((1, D_MODEL, tn), lambda i, j: (layer, 0, j)),
                  pl.BlockSpec((1, D_MODEL, GATE_COLS), lambda i, j: (layer, 0, 0))],
        out_specs=[pl.BlockSpec((tm, tn), lambda i, j: (i, j)),
                   pl.BlockSpec((tm, GATE_COLS), lambda i, j: (i, 0))],
        out_shape=[jax.ShapeDtypeStruct((m, Y_COLS), BF16),
                   jax.ShapeDtypeStruct((m, GATE_COLS), F32)],
        scratch_shapes=[pltpu.VMEM((tm, D_MODEL), BF16)],
        compiler_params=_cparams("parallel", "arbitrary"),
        name="inproj",
    )(x, mod, mod, norm_g, w, wg)


def _merge_kernel(x_ref, g1_ref, mg_ref, oa_ref, ob_ref, oc_ref, od_ref, wbr_ref, wout_ref, xo_ref):
    acc = None
    for n, o_ref in enumerate((oa_ref, ob_ref, oc_ref, od_ref)):
        proj = jnp.dot(o_ref[...], wbr_ref[0, n], preferred_element_type=F32)
        gate = jax.nn.sigmoid(mg_ref[:, n * D_MODEL:(n + 1) * D_MODEL].astype(F32))
        acc = gate * proj if acc is None else acc + gate * proj
    out = jnp.dot(acc.astype(BF16), wout_ref[0], preferred_element_type=F32)
    xo_ref[...] = x_ref[...] + g1_ref[0] * out


def _merge(x, mod, y, branches, w_branch, w_out, layer, tok_per_row):
    m = x.shape[0]
    tm = 512
    tok = lambda: pl.BlockSpec((tm, BRANCH_W), lambda i: (i, 0))
    return pl.pallas_call(
        _merge_kernel,
        grid=(m // tm,),
        in_specs=[pl.BlockSpec((tm, D_MODEL), lambda i: (i, 0)),
                  pl.BlockSpec((1, 1, D_MODEL), lambda i: ((i * tm) // tok_per_row, 0, 2)),
                  pl.BlockSpec((tm, N_BRANCH * D_MODEL), lambda i: (i, 0)),
                  tok(), tok(), tok(), tok(),
                  pl.BlockSpec((1, N_BRANCH, BRANCH_W, D_MODEL), lambda i: (layer, 0, 0, 0)),
                  pl.BlockSpec((1, D_MODEL, D_MODEL), lambda i: (layer, 0, 0))],
        out_specs=pl.BlockSpec((tm, D_MODEL), lambda i: (i, 0)),
        out_shape=jax.ShapeDtypeStruct((m, D_MODEL), F32),
        compiler_params=_cparams("parallel"),
        name="merge",
    )(x, mod, y, *branches, w_branch, w_out)


def _ffn_kernel(x_ref, sh_ref, sc_ref, g2_ref, ng_ref, fg_ref, wg_ref, wu_ref, wd_ref, xo_ref, h_scr, acc_scr,
                *, final):
    f = pl.program_id(1)

    @pl.when(f == 0)
    def _():
        h = _rms(x_ref[...], ng_ref[...]) * (1.0 + sc_ref[0]) + sh_ref[0]
        h_scr[...] = h.astype(BF16)
        acc_scr[...] = jnp.zeros_like(acc_scr)

    h = h_scr[...]
    a = jnp.dot(h, wg_ref[0], preferred_element_type=F32)
    u = jnp.dot(h, wu_ref[0], preferred_element_type=F32)
    acc_scr[...] += jnp.dot((_silu(a) * u).astype(BF16), wd_ref[0], preferred_element_type=F32)

    @pl.when(f == pl.num_programs(1) - 1)
    def _():
        xn = x_ref[...] + g2_ref[0] * acc_scr[...]
        xo_ref[...] = _rms(xn, fg_ref[...]) if final else xn


def _ffn(x, mod, norm_g, final_g, wg, wu, wd, layer, tok_per_row, final):
    m = x.shape[0]
    tm, tf = min(1024, m), D_FF // 2
    mrow = lambda c: pl.BlockSpec((1, 1, D_MODEL), lambda i, f: ((i * tm) // tok_per_row, 0, c))
    vec = lambda: pl.BlockSpec((1, D_MODEL), lambda i, f: (0, 0))
    return pl.pallas_call(
        functools.partial(_ffn_kernel, final=final),
        grid=(m // tm, D_FF // tf),
        in_specs=[pl.BlockSpec((tm, D_MODEL), lambda i, f: (i, 0)),
                  mrow(3), mrow(4), mrow(5), vec(), vec(),
                  pl.BlockSpec((1, D_MODEL, tf), lambda i, f: (layer, 0, f)),
                  pl.BlockSpec((1, D_MODEL, tf), lambda i, f: (layer, 0, f)),
                  pl.BlockSpec((1, tf, D_MODEL), lambda i, f: (layer, f, 0))],
        out_specs=pl.BlockSpec((tm, D_MODEL), lambda i, f: (i, 0)),
        out_shape=jax.ShapeDtypeStruct((m, D_MODEL), F32),
        scratch_shapes=[pltpu.VMEM((tm, D_MODEL), BF16), pltpu.VMEM((tm, D_MODEL), F32)],
        compiler_params=_cparams("parallel", "arbitrary"),
        name="ffn",
    )(x, mod, mod, mod, norm_g, final_g, wg, wu, wd)


def _shift_rows(x, off):
    n = x.shape[0]
    return x if off % n == 0 else pltpu.roll(x, (-off) % n, 0)


def _bdot(a, b):
    return jnp.einsum('bij,bjk->bik', a.astype(BF16), b.astype(BF16), preferred_element_type=F32)


def _bdot_t(a, b):
    return jnp.einsum('bik,bjk->bij', a.astype(BF16), b.astype(BF16), preferred_element_type=F32)


def _bdot_tl(a, b):
    return jnp.einsum('bki,bkj->bij', a.astype(BF16), b.astype(BF16), preferred_element_type=F32)


def _unit_tri_inverse(nmat, ii, jj):
    blk16 = (ii // 16) == (jj // 16)
    blk32 = (ii // 32) == (jj // 32)
    eye = (ii == jj).astype(F32)
    d = jnp.where(blk16, nmat, 0.0)
    x = eye - d
    p = _bdot(d, d)
    for _ in range(2):
        x = x + _bdot(x, p)
        p = _bdot(p, p)
    x = x + _bdot(x, p)
    for e in (jnp.where(blk32 & ~blk16, nmat, 0.0), jnp.where(~blk32, nmat, 0.0)):
        x = x - _bdot(x, _bdot(e, x))
    return x


def _gdn_prepare(q, k, v, gcb, rowgc, g_end, betab, sgn, ii, jj):
    c = q.shape[1]
    kb = k * betab
    gram_kk = _bdot_t(kb, k)
    gram_qk = _bdot_t(q, k)
    diff = gcb[:, :, :c] - rowgc
    order = (ii - jj) * sgn
    tri, strict = order >= 0, order > 0
    decay = jnp.where(tri, jnp.exp(jnp.where(tri, diff, 0.0)), 0.0)
    tinv = _unit_tri_inverse(jnp.where(strict, gram_kk * decay, 0.0), ii, jj)
    eg = jnp.exp(gcb)
    sol = _bdot(tinv, jnp.concatenate([v * betab, kb * eg], axis=2))
    a = jnp.where(tri, gram_qk * decay, 0.0)
    return sol[:, :, :DV_A], sol[:, :, DV_A:], a, q * eg, k * jnp.exp(g_end - gcb)


def _gdn_kernel(q_ref, k_ref, v_ref, z_ref, gate_ref, cw_ref, alog_ref, dtb_ref, ng_ref, s0_ref, o_ref, sfin_ref,
                xpad, qs, ks, vs, gc_scr, bt_scr, u_scr, w_scr, qd_scr, kd_scr, a_scr, o_scr, s_scr):
    t = q_ref.shape[0]
    n_chunks = t // CHUNK_A
    row = lax.broadcasted_iota(jnp.int32, (t, LANES), 0)
    lane = lax.broadcasted_iota(jnp.int32, (t, LANES), 1)

    xpad[0:CONV_PAD, :] = jnp.zeros((CONV_PAD, BRANCH_W), F32)
    xpad[t + CONV_PAD:t + 2 * CONV_PAD, :] = jnp.zeros((CONV_PAD, BRANCH_W), F32)

    rb = min(t, CONV_ROWS)
    for seg, (x_ref, dst) in enumerate(((q_ref, qs), (k_ref, ks), (v_ref, vs))):
        xpad[CONV_PAD:t + CONV_PAD, :] = x_ref[...].astype(F32)
        for h in range(H_A):
            lanes = slice(h * LANES, (h + 1) * LANES)
            for r in range(0, t, rb):
                acc = None
                for j in range(CONV_W):
                    r0 = CONV_PAD + r + j - CONV_W // 2
                    wj = cw_ref[j:j + 1, seg * BRANCH_W + h * LANES:seg * BRANCH_W + (h + 1) * LANES]
                    term = xpad[r0:r0 + rb, lanes] * wj
                    acc = term if acc is None else acc + term
                x = _silu(acc)
                if seg == 0:
                    x = x * lax.rsqrt(jnp.sum(x * x, axis=-1, keepdims=True) + EPS) * (DK_A ** -0.5)
                elif seg == 1:
                    x = x * lax.rsqrt(jnp.sum(x * x, axis=-1, keepdims=True) + EPS)
                dst[h, r:r + rb, :] = x

    gt = gate_ref[...]
    a = gt + dtb_ref[...]
    g = -jnp.exp(alog_ref[...]) * (jnp.maximum(a, 0.0) + jnp.log1p(jnp.exp(-jnp.abs(a))))
    pos = row % CHUNK_A
    pre, suf = g, g
    step = 1
    while step < CHUNK_A:
        pre = pre + jnp.where(pos >= step, _shift_rows(pre, -step), 0.0)
        suf = suf + jnp.where(pos < CHUNK_A - step, _shift_rows(suf, step), 0.0)
        step *= 2
    gc_scr[...] = jnp.where(lane < H_A, pre, suf)
    bt_scr[...] = jax.nn.sigmoid(gt)

    def chunk_rows(n, d):
        cn = n if d == 0 else n_chunks - 1 - n
        return pl.multiple_of(cn * CHUNK_A, CHUNK_A)

    def head_cols(x, first, rows):
        return jnp.stack([jnp.broadcast_to(x[:, first + h:first + h + 1], (rows, LANES)) for h in range(H_A)], axis=0)

    groups = [(j, d) for j in range(GDN_CHUNKS_PER_ITER) for d in range(2)]
    nb = len(groups) * H_A
    ii = lax.broadcasted_iota(jnp.int32, (nb, CHUNK_A, CHUNK_A), 1)
    jj = lax.broadcasted_iota(jnp.int32, (nb, CHUNK_A, CHUNK_A), 2)
    bb = lax.broadcasted_iota(jnp.int32, (nb, CHUNK_A, CHUNK_A), 0)
    sgn = 1 - 2 * ((bb // H_A) % 2)

    def prepare_body(i, carry):
        parts = {name: [] for name in ("q", "k", "v", "gcb", "rowgc", "g_end", "betab")}
        slices = []
        for j, d in groups:
            sl = pl.ds(chunk_rows(i * GDN_CHUNKS_PER_ITER + j, d), CHUNK_A)
            slices.append(sl)
            gch = gc_scr[sl, :]
            g_row = gch[CHUNK_A - 1:CHUNK_A, :] if d == 0 else gch[0:1, :]
            parts["q"].append(qs[:, sl, :])
            parts["k"].append(ks[:, sl, :])
            parts["v"].append(vs[:, sl, :])
            parts["gcb"].append(head_cols(gch, d * H_A, CHUNK_A))
            parts["rowgc"].append(jnp.broadcast_to(gch.T[d * H_A:(d + 1) * H_A][:, None, :],
                                                   (H_A, CHUNK_A, CHUNK_A)))
            parts["g_end"].append(head_cols(g_row, d * H_A, 1))
            parts["betab"].append(head_cols(bt_scr[sl, :], 2 * H_A + d * H_A, CHUNK_A))
        args = [jnp.concatenate(parts[name], axis=0) for name in ("q", "k", "v", "gcb", "rowgc", "g_end", "betab")]
        outs = _gdn_prepare(*args, sgn, ii, jj)
        for gi, ((j, d), sl) in enumerate(zip(groups, slices)):
            for scr, val in zip((u_scr, w_scr, a_scr, qd_scr, kd_scr), outs):
                scr[d * H_A:(d + 1) * H_A, sl, :] = val[gi * H_A:(gi + 1) * H_A].astype(BF16)
        return carry

    lax.fori_loop(0, n_chunks // GDN_CHUNKS_PER_ITER, prepare_body, 0)

    for d in range(2):
        for h in range(H_A):
            s_scr[d * H_A + h] = s0_ref[0, 0, d, h]

    def scan_body(n, carry):
        slices = [pl.ds(chunk_rows(n, d), CHUNK_A) for d in range(2)]
        both = lambda scr: jnp.concatenate([scr[d * H_A:(d + 1) * H_A, slices[d], :] for d in range(2)], axis=0)
        g_end = jnp.concatenate(
            [head_cols(gc_scr[pl.ds(chunk_rows(n, d) + (CHUNK_A - 1 if d == 0 else 0), 1), :], d * H_A, 1)
             for d in range(2)], axis=0)
        s = s_scr[...]
        v_new = both(u_scr).astype(F32) - _bdot(both(w_scr), s)
        o = _bdot(both(qd_scr), s) + _bdot(both(a_scr), v_new)
        s_scr[...] = s * jnp.exp(g_end) + _bdot_tl(both(kd_scr), v_new)
        for d in range(2):
            o_scr[d * H_A:(d + 1) * H_A, slices[d], :] = o[d * H_A:(d + 1) * H_A]
        return carry

    lax.fori_loop(0, n_chunks, scan_body, 0)

    for h in range(H_A):
        sl = slice(h * LANES, (h + 1) * LANES)
        sfin_ref[0, 0, h] = s_scr[h]
        sfin_ref[0, 1, h] = s_scr[H_A + h]
        o = o_scr[h] + o_scr[H_A + h]
        o_ref[:, sl] = (_rms(o, ng_ref[...]) * _silu(z_ref[:, sl].astype(F32))).astype(o_ref.dtype)


def _gdn(y, gates, conv_w, alog_lanes, dtb_lanes, norm_g, s0, layer, bsz, t):
    blk = lambda cb: pl.BlockSpec((t, BRANCH_W), lambda b: (b, cb // H_A))
    vec = lambda: pl.BlockSpec((1, LANES), lambda b: (0, 0))
    nhd = 2 * H_A
    return pl.pallas_call(
        _gdn_kernel,
        grid=(bsz,),
        in_specs=[blk(CB_AQ), blk(CB_AK), blk(CB_AV), blk(CB_AZ),
                  pl.BlockSpec((t, LANES), lambda b: (b, 0)),
                  pl.BlockSpec((8, 3 * BRANCH_W), lambda b: (0, 0)),
                  vec(), vec(), vec(),
                  pl.BlockSpec((1, 1, 2, H_A, DK_A, DV_A), lambda b: (b, layer, 0, 0, 0, 0))],
        out_specs=[pl.BlockSpec((t, BRANCH_W), lambda b: (b, 0)),
                   pl.BlockSpec((1, 2, H_A, DK_A, DV_A), lambda b: (b, 0, 0, 0, 0))],
        out_shape=[jax.ShapeDtypeStruct((bsz * t, BRANCH_W), BF16),
                   jax.ShapeDtypeStruct((bsz, 2, H_A, DK_A, DV_A), F32)],
        scratch_shapes=[pltpu.VMEM((t + 2 * CONV_PAD, BRANCH_W), F32),
                        pltpu.VMEM((H_A, t, LANES), F32), pltpu.VMEM((H_A, t, LANES), F32),
                        pltpu.VMEM((H_A, t, LANES), F32),
                        pltpu.VMEM((t, LANES), F32), pltpu.VMEM((t, LANES), F32),
                        pltpu.VMEM((nhd, t, DV_A), BF16), pltpu.VMEM((nhd, t, DK_A), BF16),
                        pltpu.VMEM((nhd, t, DK_A), BF16), pltpu.VMEM((nhd, t, DK_A), BF16),
                        pltpu.VMEM((nhd, t, CHUNK_A), BF16),
                        pltpu.VMEM((nhd, t, DV_A), F32), pltpu.VMEM((nhd, DK_A, DV_A), F32)],
        compiler_params=_cparams("parallel"),
        name="gdn",
    )(y, y, y, y, gates, conv_w, alog_lanes, dtb_lanes, norm_g, s0)


def _boundary_rows(bc, w, fwd):
    c = bc.shape[0]
    if w >= 4:
        pieces = []
        for start in range(0, c, 2 * w):
            r = start + w - 1 if fwd else start + w
            pieces.append(jnp.broadcast_to(bc[r:r + 1, :], (2 * w, LANES)))
        return pieces[0] if len(pieces) == 1 else jnp.concatenate(pieces, axis=0)
    pos = lax.broadcasted_iota(jnp.int32, (c, LANES), 0) % (2 * w)
    target = w - 1 if fwd else w
    out = bc
    for p in range(2 * w):
        if p != target:
            out = jnp.where(pos == p, _shift_rows(bc, target - p), out)
    return out


def _hgrn_levels():
    w = HGRN_CHUNK // 2
    while w >= 1:
        yield w
        w //= 2


def _hgrn_constants():
    idx = np.arange(HGRN_CHUNK)
    t, s = idx[:, None], idx[None, :]
    tri = np.stack([t >= s, t <= s]).astype(np.float32)
    pair = []
    for w in _hgrn_levels():
        same = (t // (2 * w)) == (s // (2 * w))
        upper_t, upper_s = (t % (2 * w)) >= w, (s % (2 * w)) >= w
        pair.append(np.stack([same & upper_t & ~upper_s, same & ~upper_t & upper_s]))
    return jnp.asarray(tri, BF16), jnp.asarray(np.stack(pair, axis=1).astype(np.float32))


def _hgrn_tiles(q, k, v, logf, st, tri, pair_ref):
    c = q.shape[1]
    bc = None
    rem = logf
    for _ in range(3):
        part = rem.astype(BF16)
        rem = rem - part.astype(F32)
        r = jnp.einsum('bij,bjk->bik', tri, part, preferred_element_type=F32)
        bc = r if bc is None else bc + r
    nb = q.shape[0]
    q_lo, k_lo = q.astype(BF16), k.astype(BF16)
    amat = None
    for lvl, w in enumerate(_hgrn_levels()):
        bm = jnp.stack([_boundary_rows(bc[i], w, i % 2 == 0) for i in range(nb)], axis=0)
        e = jnp.exp2(jnp.abs(bc - bm) * (-LOG2E)).astype(BF16)
        pair = jnp.concatenate([pair_ref[:, lvl]] * (nb // 2), axis=0)
        g = _bdot_t(q_lo * e, k_lo * e) * pair
        amat = g if amat is None else amat + g
    o = _bdot(amat, v) + jnp.sum(q * k, axis=-1, keepdims=True) * v
    tot = jnp.stack([bc[i, c - 1:c, :] if i % 2 == 0 else bc[i, 0:1, :] for i in range(nb)], axis=0)
    inc = _bdot_tl(v, k * jnp.exp(tot - bc))
    decay = jnp.exp(tot)
    states = [st]
    for j in range(nb // 2):
        states.append(states[-1] * decay[2 * j:2 * j + 2] + inc[2 * j:2 * j + 2])
    o = o + _bdot_t(q * jnp.exp(bc), jnp.concatenate(states[:-1], axis=0))
    return o, states[-1]


def _hgrn_kernel(q_ref, ff_ref, fr_ref, i_ref, g_ref, lbf_ref, lbr_ref, ng_ref, s0_ref, tri_ref, pair_ref,
                 o_ref, sfin_ref, of_scr, or_scr):
    t = q_ref.shape[0]
    n_tiles = t // HGRN_CHUNK
    tri = jnp.concatenate([tri_ref[...]] * HGRN_UNROLL, axis=0)

    def load(sl, f_ref, lb_ref):
        lb = lb_ref[...]
        f = jnp.maximum(lb + (1.0 - lb) * jax.nn.sigmoid(f_ref[sl, :].astype(F32)), F_FLOOR)
        return _silu(q_ref[sl, :].astype(F32)), 1.0 - f, i_ref[sl, :].astype(F32), jnp.log(f)

    def body(i, st):
        slices, loaded = [], []
        for j in range(HGRN_UNROLL):
            n = i * HGRN_UNROLL + j
            slf = pl.ds(pl.multiple_of(n * HGRN_CHUNK, HGRN_CHUNK), HGRN_CHUNK)
            slr = pl.ds(pl.multiple_of((n_tiles - 1 - n) * HGRN_CHUNK, HGRN_CHUNK), HGRN_CHUNK)
            slices += [slf, slr]
            loaded += [load(slf, ff_ref, lbf_ref), load(slr, fr_ref, lbr_ref)]
        o, st = _hgrn_tiles(*[jnp.stack(parts, axis=0) for parts in zip(*loaded)], st, tri, pair_ref)
        for idx, sl in enumerate(slices):
            (of_scr if idx % 2 == 0 else or_scr)[sl, :] = o[idx]
        return st

    st = lax.fori_loop(0, n_tiles // HGRN_UNROLL, body,
                       jnp.stack([s0_ref[0, 0, 0, 0].T, s0_ref[0, 0, 1, 0].T], axis=0))
    sfin_ref[0, 0, 0] = st[0].T
    sfin_ref[0, 1, 0] = st[1].T
    o = of_scr[...] + or_scr[...]
    o_ref[...] = (_rms(o, ng_ref[...]) * _silu(g_ref[...].astype(F32))).astype(o_ref.dtype)


def _hgrn(y, lb, norm_g, s0, layer, bsz, t):
    col = lambda cb: pl.BlockSpec((t, LANES), lambda b, h: (b, cb + h))
    tri, pair = _hgrn_constants()
    return pl.pallas_call(
        _hgrn_kernel,
        grid=(bsz, H_C),
        in_specs=[col(CB_CQ), col(CB_CFF), col(CB_CFB), col(CB_CI), col(CB_CG),
                  pl.BlockSpec((1, LANES), lambda b, h: (0, h)),
                  pl.BlockSpec((1, LANES), lambda b, h: (0, H_C + h)),
                  pl.BlockSpec((1, LANES), lambda b, h: (0, 0)),
                  pl.BlockSpec((1, 1, 2, 1, DK_C, DV_C), lambda b, h: (b, layer, 0, h, 0, 0)),
                  pl.BlockSpec(tri.shape, lambda b, h: (0, 0, 0)),
                  pl.BlockSpec(pair.shape, lambda b, h: (0, 0, 0, 0))],
        out_specs=[pl.BlockSpec((t, LANES), lambda b, h: (b, h)),
                   pl.BlockSpec((1, 2, 1, DK_C, DV_C), lambda b, h: (b, 0, h, 0, 0))],
        out_shape=[jax.ShapeDtypeStruct((bsz * t, BRANCH_W), BF16),
                   jax.ShapeDtypeStruct((bsz, 2, H_C, DK_C, DV_C), F32)],
        scratch_shapes=[pltpu.VMEM((t, LANES), F32), pltpu.VMEM((t, LANES), F32)],
        compiler_params=_cparams("parallel", "parallel"),
        name="hgrn",
    )(y, y, y, y, y, lb, lb, norm_g, s0, tri, pair)


def _diff_lambda(lam_ref, lam_init):
    lp = lam_ref[...]
    return (jnp.exp(jnp.sum(lp[0:1] * lp[1:2], axis=-1, keepdims=True))
            - jnp.exp(jnp.sum(lp[2:3] * lp[3:4], axis=-1, keepdims=True)) + lam_init)


def _diff_core(problems, lam, ng, lam_init):
    scores = []
    for q, k, _ in problems:
        lane = lax.broadcasted_iota(jnp.int32, q.shape, 1)
        zero = jnp.zeros_like(q)
        scores.append((_dot_t(jnp.where(lane < DK_B, q, zero), k), _dot_t(jnp.where(lane >= DK_B, q, zero), k)))
    outs = []
    for (s1, s2), (_, _, v) in zip(scores, problems):
        e1, r1 = _softmax_parts(s1)
        e2, r2 = _softmax_parts(s2)
        outs.append(_dot(e1, v) * r1 - _dot(e2, v) * (lam * r2))
    return [_rms(o, ng) * (1.0 - lam_init) for o in outs]


def _diff_ctx_kernel(lam_ref, q_ref, k_ref, v_ref, ng_ref, o_ref, *, lam_init):
    lam = _diff_lambda(lam_ref, lam_init)
    heads = [slice(h * LANES, (h + 1) * LANES) for h in range(H_B)]
    outs = _diff_core([((q_ref[:, sl].astype(F32) * DIFF_QSCALE).astype(BF16), k_ref[:, sl], v_ref[:, sl])
                       for sl in heads], lam, ng_ref[...], lam_init)
    for sl, o in zip(heads, outs):
        o_ref[:, sl] = o.astype(o_ref.dtype)


def _diff_ctx(y, lam_p, norm_g, lam_init, bsz, t):
    blk = lambda cb: pl.BlockSpec((t, BRANCH_W), lambda b: (b, cb // H_B))
    return pl.pallas_call(
        functools.partial(_diff_ctx_kernel, lam_init=lam_init),
        grid=(bsz,),
        in_specs=[pl.BlockSpec((4, DK_B), lambda b: (0, 0)), blk(CB_BQ), blk(CB_BK), blk(CB_BV),
                  pl.BlockSpec((1, LANES), lambda b: (0, 0))],
        out_specs=pl.BlockSpec((t, BRANCH_W), lambda b: (b, 0)),
        out_shape=jax.ShapeDtypeStruct((bsz * t, BRANCH_W), BF16),
        compiler_params=_cparams("parallel"),
        name="diff_ctx",
    )(lam_p, y, y, y, norm_g)


def _rope(x, cos, sin_signed):
    lane = lax.broadcasted_iota(jnp.int32, x.shape, 1)
    rot = jnp.where((lane % DK_B) < DK_B // 2, pltpu.roll(x, LANES - DK_B // 2, 1), pltpu.roll(x, DK_B // 2, 1))
    return x * cos + rot * sin_signed


def _diff_lat_kernel(lam_ref, q_ref, k_ref, v_ref, ck_ref, cv_ref, cosq_ref, sinq_ref, cos_ref, sin_ref, ng_ref,
                     o_ref, k_scr, v_scr, *, lam_init):
    past = ck_ref.shape[3]

    @pl.when(pl.program_id(2) == 0)
    def _():
        k_scr[:past, :] = ck_ref[0, 0, 0].astype(BF16)
        v_scr[:past, :] = cv_ref[0, 0, 0].astype(BF16)
        k_scr[past:, :] = _rope(k_ref[...].astype(F32), cos_ref[...], sin_ref[...]).astype(BF16)
        v_scr[past:, :] = v_ref[...]

    lam = _diff_lambda(lam_ref, lam_init)
    q = (_rope(q_ref[...].astype(F32), cosq_ref[...], sinq_ref[...]) * DIFF_QSCALE).astype(BF16)
    k, v = k_scr[...], v_scr[...]
    half = q.shape[0] // 2
    outs = _diff_core([(q[:half], k, v), (q[half:], k, v)], lam, ng_ref[...], lam_init)
    o_ref[:half, :] = outs[0].astype(o_ref.dtype)
    o_ref[half:, :] = outs[1].astype(o_ref.dtype)


def _diff_lat(y, ck, cv, cos, sin_signed, lam_p, norm_g, lam_init, layer, bsz, t):
    tq = 512
    nq = t // tq
    past = ck.shape[3]
    full = lambda cb: pl.BlockSpec((t, LANES), lambda b, h, i: (b, cb + h))
    cache = lambda: pl.BlockSpec((1, 1, 1, past, LANES), lambda b, h, i: (b, layer, h, 0, 0))
    tab_q = lambda: pl.BlockSpec((tq, LANES), lambda b, h, i: (i, 0))
    tab = lambda: pl.BlockSpec((t, LANES), lambda b, h, i: (0, 0))
    return pl.pallas_call(
        functools.partial(_diff_lat_kernel, lam_init=lam_init),
        grid=(bsz, H_B, nq),
        in_specs=[pl.BlockSpec((4, DK_B), lambda b, h, i: (0, 0)),
                  pl.BlockSpec((tq, LANES), lambda b, h, i: (b * nq + i, CB_BQ + h)),
                  full(CB_BK), full(CB_BV), cache(), cache(), tab_q(), tab_q(), tab(), tab(),
                  pl.BlockSpec((1, LANES), lambda b, h, i: (0, 0))],
        out_specs=pl.BlockSpec((tq, LANES), lambda b, h, i: (b * nq + i, h)),
        out_shape=jax.ShapeDtypeStruct((bsz * t, BRANCH_W), BF16),
        scratch_shapes=[pltpu.VMEM((past + t, LANES), BF16), pltpu.VMEM((past + t, LANES), BF16)],
        compiler_params=_cparams("parallel", "parallel", "arbitrary"),
        name="diff_lat",
    )(lam_p, y, y, y, ck, cv, cos, sin_signed, cos, sin_signed, norm_g)


def _pair_attend(q, keys, vals, bias=None):
    lane = lax.broadcasted_iota(jnp.int32, q.shape, 1)
    zero = jnp.zeros_like(q)
    scores = [[_dot_t(jnp.where((lane < DH_D) == (par == 0), q, zero), kk) for kk in keys] for par in range(2)]
    outs = []
    for par in range(2):
        s = scores[par]
        if bias is not None:
            s[-1] = jnp.where(bias[par] > 0.5 * MASK_VALUE, s[-1] + bias[par], MASK_VALUE)
        m = s[0].max(axis=-1, keepdims=True)
        for x in s[1:]:
            m = jnp.maximum(m, x.max(axis=-1, keepdims=True))
        e = [jnp.exp2(x - m) for x in s]
        den = e[0].sum(axis=-1, keepdims=True)
        for x in e[1:]:
            den = den + x.sum(axis=-1, keepdims=True)
        o = _dot(e[0], vals[0])
        for x, vv in zip(e[1:], vals[1:]):
            o = o + _dot(x, vv)
        outs.append(o * (1.0 / den))
    return jnp.where(lane < DH_D, outs[0], outs[1])


def _na_ctx_kernel(q_ref, k_ref, v_ref, o_ref):
    for p in range(H_D // 2):
        sl = slice(p * LANES, (p + 1) * LANES)
        q = (q_ref[:, sl].astype(F32) * NA_QSCALE).astype(BF16)
        o_ref[:, sl] = _pair_attend(q, [k_ref[:, sl]], [v_ref[:, sl]]).astype(o_ref.dtype)


def _na_ctx(y, bsz, t):
    blk = lambda cb: pl.BlockSpec((t, BRANCH_W), lambda b: (b, cb // 4))
    return pl.pallas_call(
        _na_ctx_kernel,
        grid=(bsz,),
        in_specs=[blk(CB_DQ), blk(CB_DK), blk(CB_DV)],
        out_specs=pl.BlockSpec((t, BRANCH_W), lambda b: (b, 0)),
        out_shape=jax.ShapeDtypeStruct((bsz * t, BRANCH_W), BF16),
        compiler_params=_cparams("parallel"),
        name="na_ctx",
    )(y, y, y)


def _na_bias_kernel(rpb_ref, o_ref):
    hd = pl.program_id(0)
    qc = lax.broadcasted_iota(jnp.int32, (GRID_W, LANES), 0)
    lane = lax.broadcasted_iota(jnp.int32, (GRID_W, LANES), 1)
    kc = lane % GRID_W
    dc = jnp.clip(kc - qc + WIN_C - 1, 0, 2 * WIN_C - 2)
    c0 = jnp.clip(qc - WIN_C // 2, 0, GRID_W - WIN_C)
    col_ok = (kc >= c0) & (kc < c0 + WIN_C)
    n_dr = 2 * WIN_R - 1
    tables = []
    for dr in range(n_dr):
        acc = jnp.zeros((GRID_W, LANES), F32)
        for d in range(2 * WIN_C - 1):
            acc = jnp.where(dc == d, rpb_ref[hd, dr * (2 * WIN_C - 1) + d], acc)
        tables.append(jnp.where(col_ok, acc * LOG2E, MASK_VALUE))
    masked = jnp.full((GRID_W, LANES), MASK_VALUE, F32)
    for i in range(n_dr + 1):
        lo = tables[i - 1] if i >= 1 else masked
        hi = tables[i] if i < n_dr else masked
        o_ref[0, i] = jnp.where(lane < GRID_W, lo, hi)


def _na_bias(rpb):
    n_dr = 2 * WIN_R - 1
    return pl.pallas_call(
        _na_bias_kernel,
        grid=(H_D,),
        in_specs=[pl.BlockSpec(memory_space=pltpu.SMEM)],
        out_specs=pl.BlockSpec((1, n_dr + 1, GRID_W, LANES), lambda h: (h, 0, 0, 0)),
        out_shape=jax.ShapeDtypeStruct((H_D, n_dr + 1, GRID_W, LANES), F32),
        compiler_params=_cparams("parallel"),
        name="na_bias",
    )(rpb.reshape(H_D, n_dr * (2 * WIN_C - 1)))


def _na_lat_kernel(q_ref, k_ref, v_ref, ck_ref, cv_ref, tab_ref, o_ref, *, rows):
    wr = min(WIN_R, rows)
    lane = lax.broadcasted_iota(jnp.int32, (GRID_W, LANES), 1)
    masked = jnp.full((GRID_W, LANES), MASK_VALUE, F32)
    ck = jnp.concatenate([ck_ref[0, 0, 0], ck_ref[0, 0, 1]], axis=1).astype(BF16)
    cv = jnp.concatenate([cv_ref[0, 0, 0], cv_ref[0, 0, 1]], axis=1).astype(BF16)
    first_row = lambda r: min(max(r - wr // 2, 0), rows - wr)
    for g in range(rows // NA_QROWS):
        starts = [first_row(g * NA_QROWS + qi) for qi in range(NA_QROWS)]
        kp0, kp1 = min(starts) // 2, (max(starts) + wr + 1) // 2
        k_lat = k_ref[2 * kp0 * GRID_W:2 * kp1 * GRID_W, :]
        v_lat = v_ref[2 * kp0 * GRID_W:2 * kp1 * GRID_W, :]
        bias = []
        for par in range(2):
            blocks = []
            for qi in range(NA_QROWS):
                r = g * NA_QROWS + qi
                r0 = first_row(r)
                tiles = []
                for kp in range(kp0, kp1):
                    ok0 = r0 <= 2 * kp < r0 + wr
                    ok1 = r0 <= 2 * kp + 1 < r0 + wr
                    if not (ok0 or ok1):
                        tiles.append(masked)
                        continue
                    tile = tab_ref[par, 2 * kp - r + WIN_R]
                    if not ok0:
                        tile = jnp.where(lane >= GRID_W, tile, MASK_VALUE)
                    if not ok1:
                        tile = jnp.where(lane < GRID_W, tile, MASK_VALUE)
                    tiles.append(tile)
                blocks.append(jnp.concatenate(tiles, axis=1))
            bias.append(jnp.concatenate(blocks, axis=0))
        sl = slice(g * NA_QROWS * GRID_W, (g + 1) * NA_QROWS * GRID_W)
        q = (q_ref[sl, :].astype(F32) * NA_QSCALE).astype(BF16)
        o_ref[sl, :] = _pair_attend(q, [ck, k_lat], [cv, v_lat], bias).astype(o_ref.dtype)


def _na_lat(y, ck, cv, tables, layer, bsz, t):
    rows = t // GRID_W
    past = ck.shape[3]
    n_tab = tables.shape[1]
    col = lambda cb: pl.BlockSpec((t, LANES), lambda b, p: (b, cb + p))
    cache = lambda: pl.BlockSpec((1, 1, 2, past, DH_D), lambda b, p: (b, layer, p, 0, 0))
    return pl.pallas_call(
        functools.partial(_na_lat_kernel, rows=rows),
        grid=(bsz, H_D // 2),
        in_specs=[col(CB_DQ), col(CB_DK), col(CB_DV), cache(), cache(),
                  pl.BlockSpec((2, n_tab, GRID_W, LANES), lambda b, p: (p, 0, 0, 0))],
        out_specs=pl.BlockSpec((t, LANES), lambda b, p: (b, p)),
        out_shape=jax.ShapeDtypeStruct((bsz * t, BRANCH_W), BF16),
        compiler_params=_cparams("parallel", "parallel"),
        name="na_lat",
    )(y, y, y, ck, cv, tables)


def _axial_rope_tables(n_tok):
    tok = jnp.arange(n_tok)
    n_freq = DK_B // 4
    inv = ROPE_BASE ** (-jnp.arange(n_freq, dtype=F32) / n_freq)
    ang = jnp.concatenate([(tok // GRID_W).astype(F32)[:, None] * inv,
                           (tok % GRID_W).astype(F32)[:, None] * inv], axis=-1)
    cos, sin = jnp.cos(ang), jnp.sin(ang)
    return jnp.tile(cos, (1, 4)), jnp.tile(jnp.concatenate([-sin, sin], axis=-1), (1, 2))


def _permute_w_in(w):
    a_end = 4 * BRANCH_W
    g_end = a_end + 4 * H_A
    mg_start = w.shape[-1] - N_BRANCH * D_MODEL
    main = jnp.concatenate([w[..., mg_start:], w[..., :a_end], w[..., g_end:mg_start]], axis=-1)
    gate = jnp.pad(w[..., a_end:g_end], ((0, 0), (0, 0), (0, GATE_COLS - 4 * H_A)))
    return main.astype(BF16), gate.astype(BF16)


def _ctx_cache_kernel(*refs):
    ins, (bk_out, bv_out, dk_out, dv_out) = refs[:-4], refs[-4:]
    for layer in range(DEPTH):
        bk, bv, dk, dv = ins[4 * layer:4 * layer + 4]

        @pl.when(pl.program_id(1) == layer)
        def _():
            for h in range(H_B):
                sl = slice(h * LANES, (h + 1) * LANES)
                bk_out[0, 0, h] = bk[:, sl].astype(F32)
                bv_out[0, 0, h] = bv[:, sl].astype(F32)
            for h in range(H_D):
                sl = slice(h * DH_D, (h + 1) * DH_D)
                dk_out[0, 0, h] = dk[:, sl].astype(F32)
                dv_out[0, 0, h] = dv[:, sl].astype(F32)


def _ctx_caches(ys, bsz, t):
    blk = lambda cb: pl.BlockSpec((t, BRANCH_W), lambda b, l: (b, cb // 4))
    out = lambda heads, width: pl.BlockSpec((1, 1, heads, t, width), lambda b, l: (b, l, 0, 0, 0))
    shape = lambda heads, width: jax.ShapeDtypeStruct((bsz, DEPTH, heads, t, width), F32)
    return pl.pallas_call(
        _ctx_cache_kernel,
        grid=(bsz, DEPTH),
        in_specs=[blk(cb) for _ in ys for cb in (CB_BK, CB_BV, CB_DK, CB_DV)],
        out_specs=[out(H_B, 2 * DK_B), out(H_B, DV_B), out(H_D, DH_D), out(H_D, DH_D)],
        out_shape=[shape(H_B, 2 * DK_B), shape(H_B, DV_B), shape(H_D, DH_D), shape(H_D, DH_D)],
        compiler_params=_cparams("parallel", "parallel"),
        name="ctx_caches",
    )(*[y for y in ys for _ in range(4)])


def kernel(x_prompt, x_sample, c, state_gdn, cache_diff_k, cache_diff_v, state_hgrn, cache_na_k, cache_na_v, c_ctx,
           w_ada, b_ada, norm1_g, w_in, gdn_conv_w, gdn_A_log, gdn_dt_bias, gdn_norm_g, diff_lambda, diff_norm_g,
           hgrn_lb_logits, hgrn_norm_g, na_rpb, w_branch, w_out, norm2_g, w_ffn_gate, w_ffn_up, w_ffn_down,
           final_norm_g):
    bp, tp, _ = x_prompt.shape
    bs, ts, _ = x_sample.shape

    cond = jnp.zeros((16, D_MODEL), F32).at[:bs].set(c).at[bs].set(c_ctx)
    mods = _adaln(cond, w_ada, b_ada)
    probs = jax.nn.softmax(hgrn_lb_logits.astype(F32), axis=0)
    lb_all = jnp.cumsum(probs, axis=0) - probs[0:1]
    cos, sin_signed = _axial_rope_tables(ts)
    zeros_gdn = jnp.zeros((bp, 1, 2, H_A, DK_A, DV_A), F32)
    zeros_hgrn = jnp.zeros((bp, 1, 2, H_C, DK_C, DV_C), F32)
    w_main, w_gate = _permute_w_in(w_in)
    wbr, wo = w_branch.astype(BF16), w_out.astype(BF16)
    wfg, wfu, wfd = w_ffn_gate.astype(BF16), w_ffn_up.astype(BF16), w_ffn_down.astype(BF16)

    xp = x_prompt.reshape(bp * tp, D_MODEL)
    xs = x_sample.reshape(bs * ts, D_MODEL)
    new_ctx = []
    for l in range(DEPTH):
        lam_init = 0.8 - 0.6 * math.exp(-0.3 * l)
        conv_w = jnp.zeros((8, 3 * BRANCH_W), F32).at[:CONV_W].set(gdn_conv_w[l].T)
        lanes = lambda p: jnp.pad(p.astype(F32).reshape(1, 2 * H_A), ((0, 0), (0, LANES - 2 * H_A)))
        alog, dtb = lanes(gdn_A_log[l]), lanes(gdn_dt_bias[l])
        lb = lb_all[l].reshape(1, 2 * H_C * DK_C)
        vecs = dict(n1=norm1_g[l].reshape(1, D_MODEL), n2=norm2_g[l].reshape(1, D_MODEL),
                    fin=final_norm_g.reshape(1, D_MODEL), gdn=gdn_norm_g[l].reshape(1, LANES),
                    diff=diff_norm_g[l].reshape(1, LANES), hgrn=hgrn_norm_g[l].reshape(1, LANES))
        tables = _na_bias(na_rpb[l])
        final = l == DEPTH - 1

        def dense_tail(x, mod, y, branches, tok_per_row):
            x = _merge(x, mod, y, branches, wbr, wo, l, tok_per_row)
            return _ffn(x, mod, vecs["n2"], vecs["fin"], wfg, wfu, wfd, l, tok_per_row, final)

        mod = mods[l, bs:bs + 1].reshape(1, 1, 6 * D_MODEL)
        y, gates = _inproj(xp, mod, vecs["n1"], w_main, w_gate, l, bp * tp)
        o_a, st_gdn = _gdn(y, gates, conv_w, alog, dtb, vecs["gdn"], zeros_gdn, 0, bp, tp)
        o_b = _diff_ctx(y, diff_lambda[l], vecs["diff"], lam_init, bp, tp)
        o_c, st_hgrn = _hgrn(y, lb, vecs["hgrn"], zeros_hgrn, 0, bp, tp)
        o_d = _na_ctx(y, bp, tp)
        xp = dense_tail(xp, mod, y, (o_a, o_b, o_c, o_d), bp * tp)
        new_ctx.append((st_gdn, st_hgrn, y))

        mod = mods[l, :bs].reshape(bs, 1, 6 * D_MODEL)
        y, gates = _inproj(xs, mod, vecs["n1"], w_main, w_gate, l, ts)
        o_a, _ = _gdn(y, gates, conv_w, alog, dtb, vecs["gdn"], state_gdn, l, bs, ts)
        o_b = _diff_lat(y, cache_diff_k, cache_diff_v, cos, sin_signed, diff_lambda[l], vecs["diff"], lam_init,
                        l, bs, ts)
        o_c, _ = _hgrn(y, lb, vecs["hgrn"], state_hgrn, l, bs, ts)
        o_d = _na_lat(y, cache_na_k, cache_na_v, tables, l, bs, ts)
        xs = dense_tail(xs, mod, y, (o_a, o_b, o_c, o_d), ts)

    stack = lambda i: jnp.stack([n[i] for n in new_ctx], axis=1)
    diff_k, diff_v, na_k, na_v = _ctx_caches([n[2] for n in new_ctx], bp, tp)
    return (xp.reshape(bp, tp, D_MODEL), xs.reshape(bs, ts, D_MODEL), stack(0), diff_k, diff_v, stack(1),
            na_k, na_v)
```

```python
import functools
import math

import jax
import jax.numpy as jnp
import numpy as np
from jax import lax
from jax.experimental import pallas as pl
from jax.experimental.pallas import tpu as pltpu

F32 = jnp.float32
BF16 = jnp.bfloat16

D_MODEL = 1024
DEPTH = 2
GRID_W = 64
N_BRANCH = 4
H_A, DK_A, DV_A, CONV_W, CHUNK_A = 4, 128, 128, 5, 64
H_B, DK_B, DV_B = 4, 64, 128
H_C, DK_C, DV_C = 4, 128, 128
H_D, DH_D, WIN_R, WIN_C = 8, 64, 8, 16
BRANCH_W = 512
ROPE_BASE = 10000.0
D_FF = 2816
EPS = 1e-6
MASK_VALUE = -1e30
F_FLOOR = 1e-30
LOG2E = math.log2(math.e)
DIFF_QSCALE = DK_B ** -0.5 * LOG2E
NA_QSCALE = DH_D ** -0.5 * LOG2E

LANES = 128
VMEM_LIMIT_BYTES = 56 * 1024 * 1024

Y_COLS = 11776
CB_MG = 0
CB_AQ, CB_AK, CB_AV, CB_AZ = 32, 36, 40, 44
CB_BQ, CB_BK, CB_BV = 48, 52, 56
CB_CQ, CB_CFF, CB_CFB, CB_CI, CB_CG = 60, 64, 68, 72, 76
CB_DQ, CB_DK, CB_DV = 80, 84, 88
GATE_COLS = LANES

GDN_CHUNKS_PER_ITER = 4
CONV_PAD = 8
CONV_ROWS = 128
HGRN_CHUNK = 128
HGRN_UNROLL = 2
NA_QROWS = 4


def _cparams(*sem):
    return pltpu.CompilerParams(dimension_semantics=sem, vmem_limit_bytes=VMEM_LIMIT_BYTES)


def _dot(a, b):
    return jnp.dot(a.astype(BF16), b.astype(BF16), preferred_element_type=F32)


def _dot_t(a, b):
    return lax.dot_general(a.astype(BF16), b.astype(BF16), (((1,), (1,)), ((), ())), preferred_element_type=F32)


def _silu(x):
    return x * jax.nn.sigmoid(x)


def _rms(x, g):
    return x * lax.rsqrt(jnp.mean(x * x, axis=-1, keepdims=True) + EPS) * g


def _softmax_parts(s):
    m = jnp.max(s, axis=-1, keepdims=True)
    e = jnp.exp2(s - m)
    return e, 1.0 / jnp.sum(e, axis=-1, keepdims=True)


def _adaln_kernel(c_ref, w_ref, b_ref, o_ref):
    s = _silu(c_ref[...])
    o_ref[0] = jnp.dot(s, w_ref[0], precision=lax.Precision.HIGHEST, preferred_element_type=F32) + b_ref[0]


def _adaln(cond, w_ada, b_ada):
    rows = cond.shape[0]
    tn = 1024
    n_out = w_ada.shape[-1]
    return pl.pallas_call(
        _adaln_kernel,
        grid=(DEPTH, n_out // tn),
        in_specs=[pl.BlockSpec((rows, D_MODEL), lambda l, j: (0, 0)),
                  pl.BlockSpec((1, D_MODEL, tn), lambda l, j: (l, 0, j)),
                  pl.BlockSpec((1, 1, tn), lambda l, j: (l, 0, j))],
        out_specs=pl.BlockSpec((1, rows, tn), lambda l, j: (l, 0, j)),
        out_shape=jax.ShapeDtypeStruct((DEPTH, rows, n_out), F32),
        compiler_params=_cparams("parallel", "parallel"),
        name="adaln",
    )(cond, w_ada, b_ada.reshape(DEPTH, 1, n_out))


def _inproj_kernel(x_ref, sh_ref, sc_ref, g_ref, w_ref, wg_ref, y_ref, gate_ref, h_scr):
    @pl.when(pl.program_id(1) == 0)
    def _():
        h = _rms(x_ref[...], g_ref[...]) * (1.0 + sc_ref[0]) + sh_ref[0]
        hb = h.astype(BF16)
        h_scr[...] = hb
        gate_ref[...] = jnp.dot(hb, wg_ref[0], preferred_element_type=F32)

    y_ref[...] = jnp.dot(h_scr[...], w_ref[0], preferred_element_type=F32).astype(y_ref.dtype)


def _inproj(x, mod, norm_g, w, wg, layer, tok_per_row):
    m = x.shape[0]
    tm, tn = min(1024, m), Y_COLS // 4
    mrow = lambda c: pl.BlockSpec((1, 1, D_MODEL), lambda i, j: ((i * tm) // tok_per_row, 0, c))
    return pl.pallas_call(
        _inproj_kernel,
        grid=(m // tm, Y_COLS // tn),
        in_specs=[pl.BlockSpec((tm, D_MODEL), lambda i, j: (i, 0)),
                  mrow(0), mrow(1),
                  pl.BlockSpec((1, D_MODEL), lambda i, j: (0, 0)),
                  pl.BlockSpec((1, D_MODEL, tn), lambda i, j: (layer, 0, j)),
                  pl.BlockSpec((1, D_MODEL, GATE_COLS), lambda i, j: (layer, 0, 0))],
        out_specs=[pl.BlockSpec((tm, tn), lambda i, j: (i, j)),
                   pl.BlockSpec((tm, GATE_COLS), lambda i, j: (i, 0))],
        out_shape=[jax.ShapeDtypeStruct((m, Y_COLS), BF16),
                   jax.ShapeDtypeStruct((m, GATE_COLS), F32)],
        scratch_shapes=[pltpu.VMEM((tm, D_MODEL), BF16)],
        compiler_params=_cparams("parallel", "arbitrary"),
        name="inproj",
    )(x, mod, mod, norm_g, w, wg)


def _merge_kernel(x_ref, g1_ref, mg_ref, oa_ref, ob_ref, oc_ref, od_ref, wbr_ref, wout_ref, xo_ref):
    acc = None
    for n, o_ref in enumerate((oa_ref, ob_ref, oc_ref, od_ref)):
        proj = jnp.dot(o_ref[...], wbr_ref[0, n], preferred_element_type=F32)
        gate = jax.nn.sigmoid(mg_ref[:, n * D_MODEL:(n + 1) * D_MODEL].astype(F32))
        acc = gate * proj if acc is None else acc + gate * proj
    out = jnp.dot(acc.astype(BF16), wout_ref[0], preferred_element_type=F32)
    xo_ref[...] = x_ref[...] + g1_ref[0] * out


def _merge(x, mod, y, branches, w_branch, w_out, layer, tok_per_row):
    m = x.shape[0]
    tm = 512
    tok = lambda: pl.BlockSpec((tm, BRANCH_W), lambda i: (i, 0))
    return pl.pallas_call(
        _merge_kernel,
        grid=(m // tm,),
        in_specs=[pl.BlockSpec((tm, D_MODEL), lambda i: (i, 0)),
                  pl.BlockSpec((1, 1, D_MODEL), lambda i: ((i * tm) // tok_per_row, 0, 2)),
                  pl.BlockSpec((tm, N_BRANCH * D_MODEL), lambda i: (i, 0)),
                  tok(), tok(), tok(), tok(),
                  pl.BlockSpec((1, N_BRANCH, BRANCH_W, D_MODEL), lambda i: (layer, 0, 0, 0)),
                  pl.BlockSpec((1, D_MODEL, D_MODEL), lambda i: (layer, 0, 0))],
        out_specs=pl.BlockSpec((tm, D_MODEL), lambda i: (i, 0)),
        out_shape=jax.ShapeDtypeStruct((m, D_MODEL), F32),
        compiler_params=_cparams("parallel"),
        name="merge",
    )(x, mod, y, *branches, w_branch, w_out)


def _ffn_kernel(x_ref, sh_ref, sc_ref, g2_ref, ng_ref, fg_ref, wg_ref, wu_ref, wd_ref, xo_ref, h_scr, acc_scr,
                *, final):
    f = pl.program_id(1)

    @pl.when(f == 0)
    def _():
        h = _rms(x_ref[...], ng_ref[...]) * (1.0 + sc_ref[0]) + sh_ref[0]
        h_scr[...] = h.astype(BF16)
        acc_scr[...] = jnp.zeros_like(acc_scr)

    h = h_scr[...]
    a = jnp.dot(h, wg_ref[0], preferred_element_type=F32)
    u = jnp.dot(h, wu_ref[0], preferred_element_type=F32)
    acc_scr[...] += jnp.dot((_silu(a) * u).astype(BF16), wd_ref[0], preferred_element_type=F32)

    @pl.when(f == pl.num_programs(1) - 1)
    def _():
        xn = x_ref[...] + g2_ref[0] * acc_scr[...]
        xo_ref[...] = _rms(xn, fg_ref[...]) if final else xn


def _ffn(x, mod, norm_g, final_g, wg, wu, wd, layer, tok_per_row, final):
    m = x.shape[0]
    tm, tf = min(1024, m), D_FF // 2
    mrow = lambda c: pl.BlockSpec((1, 1, D_MODEL), lambda i, f: ((i * tm) // tok_per_row, 0, c))
    vec = lambda: pl.BlockSpec((1, D_MODEL), lambda i, f: (0, 0))
    return pl.pallas_call(
        functools.partial(_ffn_kernel, final=final),
        grid=(m // tm, D_FF // tf),
        in_specs=[pl.BlockSpec((tm, D_MODEL), lambda i, f: (i, 0)),
                  mrow(3), mrow(4), mrow(5), vec(), vec(),
                  pl.BlockSpec((1, D_MODEL, tf), lambda i, f: (layer, 0, f)),
                  pl.BlockSpec((1, D_MODEL, tf), lambda i, f: (layer, 0, f)),
                  pl.BlockSpec((1, tf, D_MODEL), lambda i, f: (layer, f, 0))],
        out_specs=pl.BlockSpec((tm, D_MODEL), lambda i, f: (i, 0)),
        out_shape=jax.ShapeDtypeStruct((m, D_MODEL), F32),
        scratch_shapes=[pltpu.VMEM((tm, D_MODEL), BF16), pltpu.VMEM((tm, D_MODEL), F32)],
        compiler_params=_cparams("parallel", "arbitrary"),
        name="ffn",
    )(x, mod, mod, mod, norm_g, final_g, wg, wu, wd)


def _shift_rows(x, off):
    n = x.shape[0]
    return x if off % n == 0 else pltpu.roll(x, (-off) % n, 0)


def _bdot(a, b):
    return jnp.einsum('bij,bjk->bik', a.astype(BF16), b.astype(BF16), preferred_element_type=F32)


def _bdot_t(a, b):
    return jnp.einsum('bik,bjk->bij', a.astype(BF16), b.astype(BF16), preferred_element_type=F32)


def _bdot_tl(a, b):
    return jnp.einsum('bki,bkj->bij', a.astype(BF16), b.astype(BF16), preferred_element_type=F32)


def _unit_tri_inverse(nmat, ii, jj):
    blk16 = (ii // 16) == (jj // 16)
    blk32 = (ii // 32) == (jj // 32)
    eye = (ii == jj).astype(F32)
    d = jnp.where(blk16, nmat, 0.0)
    x = eye - d
    p = _bdot(d, d)
    for _ in range(2):
        x = x + _bdot(x, p)
        p = _bdot(p, p)
    x = x + _bdot(x, p)
    for e in (jnp.where(blk32 & ~blk16, nmat, 0.0), jnp.where(~blk32, nmat, 0.0)):
        x = x - _bdot(x, _bdot(e, x))
    return x


def _gdn_prepare(q, k, v, gcb, rowgc, g_end, betab, sgn, ii, jj):
    c = q.shape[1]
    kb = k * betab
    gram_kk = _bdot_t(kb, k)
    gram_qk = _bdot_t(q, k)
    diff = gcb[:, :, :c] - rowgc
    order = (ii - jj) * sgn
    tri, strict = order >= 0, order > 0
    decay = jnp.where(tri, jnp.exp(jnp.where(tri, diff, 0.0)), 0.0)
    tinv = _unit_tri_inverse(jnp.where(strict, gram_kk * decay, 0.0), ii, jj)
    eg = jnp.exp(gcb)
    sol = _bdot(tinv, jnp.concatenate([v * betab, kb * eg], axis=2))
    a = jnp.where(tri, gram_qk * decay, 0.0)
    return sol[:, :, :DV_A], sol[:, :, DV_A:], a, q * eg, k * jnp.exp(g_end - gcb)


def _gdn_kernel(q_ref, k_ref, v_ref, z_ref, gate_ref, cw_ref, alog_ref, dtb_ref, ng_ref, s0_ref, o_ref, sfin_ref,
                xpad, qs, ks, vs, gc_scr, bt_scr, u_scr, w_scr, qd_scr, kd_scr, a_scr, o_scr, s_scr):
    t = q_ref.shape[0]
    n_chunks = t // CHUNK_A
    row = lax.broadcasted_iota(jnp.int32, (t, LANES), 0)
    lane = lax.broadcasted_iota(jnp.int32, (t, LANES), 1)

    xpad[0:CONV_PAD, :] = jnp.zeros((CONV_PAD, BRANCH_W), F32)
    xpad[t + CONV_PAD:t + 2 * CONV_PAD, :] = jnp.zeros((CONV_PAD, BRANCH_W), F32)

    rb = min(t, CONV_ROWS)
    for seg, (x_ref, dst) in enumerate(((q_ref, qs), (k_ref, ks), (v_ref, vs))):
        xpad[CONV_PAD:t + CONV_PAD, :] = x_ref[...].astype(F32)
        for h in range(H_A):
            lanes = slice(h * LANES, (h + 1) * LANES)
            for r in range(0, t, rb):
                acc = None
                for j in range(CONV_W):
                    r0 = CONV_PAD + r + j - CONV_W // 2
                    wj = cw_ref[j:j + 1, seg * BRANCH_W + h * LANES:seg * BRANCH_W + (h + 1) * LANES]
                    term = xpad[r0:r0 + rb, lanes] * wj
                    acc = term if acc is None else acc + term
                x = _silu(acc)
                if seg == 0:
                    x = x * lax.rsqrt(jnp.sum(x * x, axis=-1, keepdims=True) + EPS) * (DK_A ** -0.5)
                elif seg == 1:
                    x = x * lax.rsqrt(jnp.sum(x * x, axis=-1, keepdims=True) + EPS)
                dst[h, r:r + rb, :] = x

    gt = gate_ref[...]
    a = gt + dtb_ref[...]
    g = -jnp.exp(alog_ref[...]) * (jnp.maximum(a, 0.0) + jnp.log1p(jnp.exp(-jnp.abs(a))))
    pos = row % CHUNK_A
    pre, suf = g, g
    step = 1
    while step < CHUNK_A:
        pre = pre + jnp.where(pos >= step, _shift_rows(pre, -step), 0.0)
        suf = suf + jnp.where(pos < CHUNK_A - step, _shift_rows(suf, step), 0.0)
        step *= 2
    gc_scr[...] = jnp.where(lane < H_A, pre, suf)
    bt_scr[...] = jax.nn.sigmoid(gt)

    def chunk_rows(n, d):
        cn = n if d == 0 else n_chunks - 1 - n
        return pl.multiple_of(cn * CHUNK_A, CHUNK_A)

    def head_cols(x, first, rows):
        return jnp.stack([jnp.broadcast_to(x[:, first + h:first + h + 1], (rows, LANES)) for h in range(H_A)], axis=0)

    groups = [(j, d) for j in range(GDN_CHUNKS_PER_ITER) for d in range(2)]
    nb = len(groups) * H_A
    ii = lax.broadcasted_iota(jnp.int32, (nb, CHUNK_A, CHUNK_A), 1)
    jj = lax.broadcasted_iota(jnp.int32, (nb, CHUNK_A, CHUNK_A), 2)
    bb = lax.broadcasted_iota(jnp.int32, (nb, CHUNK_A, CHUNK_A), 0)
    sgn = 1 - 2 * ((bb // H_A) % 2)

    def prepare_body(i, carry):
        parts = {name: [] for name in ("q", "k", "v", "gcb", "rowgc", "g_end", "betab")}
        slices = []
        for j, d in groups:
            sl = pl.ds(chunk_rows(i * GDN_CHUNKS_PER_ITER + j, d), CHUNK_A)
            slices.append(sl)
            gch = gc_scr[sl, :]
            g_row = gch[CHUNK_A - 1:CHUNK_A, :] if d == 0 else gch[0:1, :]
            parts["q"].append(qs[:, sl, :])
            parts["k"].append(ks[:, sl, :])
            parts["v"].append(vs[:, sl, :])
            parts["gcb"].append(head_cols(gch, d * H_A, CHUNK_A))
            parts["rowgc"].append(jnp.broadcast_to(gch.T[d * H_A:(d + 1) * H_A][:, None, :],
                                                   (H_A, CHUNK_A, CHUNK_A)))
            parts["g_end"].append(head_cols(g_row, d * H_A, 1))
            parts["betab"].append(head_cols(bt_scr[sl, :], 2 * H_A + d * H_A, CHUNK_A))
        args = [jnp.concatenate(parts[name], axis=0) for name in ("q", "k", "v", "gcb", "rowgc", "g_end", "betab")]
        outs = _gdn_prepare(*args, sgn, ii, jj)
        for gi, ((j, d), sl) in enumerate(zip(groups, slices)):
            for scr, val in zip((u_scr, w_scr, a_scr, qd_scr, kd_scr), outs):
                scr[d * H_A:(d + 1) * H_A, sl, :] = val[gi * H_A:(gi + 1) * H_A].astype(BF16)
        return carry

    lax.fori_loop(0, n_chunks // GDN_CHUNKS_PER_ITER, prepare_body, 0)

    for d in range(2):
        for h in range(H_A):
            s_scr[d * H_A + h] = s0_ref[0, 0, d, h]

    def scan_body(n, carry):
        slices = [pl.ds(chunk_rows(n, d), CHUNK_A) for d in range(2)]
        both = lambda scr: jnp.concatenate([scr[d * H_A:(d + 1) * H_A, slices[d], :] for d in range(2)], axis=0)
        g_end = jnp.concatenate(
            [head_cols(gc_scr[pl.ds(chunk_rows(n, d) + (CHUNK_A - 1 if d == 0 else 0), 1), :], d * H_A, 1)
             for d in range(2)], axis=0)
        s = s_scr[...]
        v_new = both(u_scr).astype(F32) - _bdot(both(w_scr), s)
        o = _bdot(both(qd_scr), s) + _bdot(both(a_scr), v_new)
        s_scr[...] = s * jnp.exp(g_end) + _bdot_tl(both(kd_scr), v_new)
        for d in range(2):
            o_scr[d * H_A:(d + 1) * H_A, slices[d], :] = o[d * H_A:(d + 1) * H_A]
        return carry

    lax.fori_loop(0, n_chunks, scan_body, 0)

    for h in range(H_A):
        sl = slice(h * LANES, (h + 1) * LANES)
        sfin_ref[0, 0, h] = s_scr[h]
        sfin_ref[0, 1, h] = s_scr[H_A + h]
        o = o_scr[h] + o_scr[H_A + h]
        o_ref[:, sl] = (_rms(o, ng_ref[...]) * _silu(z_ref[:, sl].astype(F32))).astype(o_ref.dtype)


def _gdn(y, gates, conv_w, alog_lanes, dtb_lanes, norm_g, s0, layer, bsz, t):
    blk = lambda cb: pl.BlockSpec((t, BRANCH_W), lambda b: (b, cb // H_A))
    vec = lambda: pl.BlockSpec((1, LANES), lambda b: (0, 0))
    nhd = 2 * H_A
    return pl.pallas_call(
        _gdn_kernel,
        grid=(bsz,),
        in_specs=[blk(CB_AQ), blk(CB_AK), blk(CB_AV), blk(CB_AZ),
                  pl.BlockSpec((t, LANES), lambda b: (b, 0)),
                  pl.BlockSpec((8, 3 * BRANCH_W), lambda b: (0, 0)),
                  vec(), vec(), vec(),
                  pl.BlockSpec((1, 1, 2, H_A, DK_A, DV_A), lambda b: (b, layer, 0, 0, 0, 0))],
        out_specs=[pl.BlockSpec((t, BRANCH_W), lambda b: (b, 0)),
                   pl.BlockSpec((1, 2, H_A, DK_A, DV_A), lambda b: (b, 0, 0, 0, 0))],
        out_shape=[jax.ShapeDtypeStruct((bsz * t, BRANCH_W), BF16),
                   jax.ShapeDtypeStruct((bsz, 2, H_A, DK_A, DV_A), F32)],
        scratch_shapes=[pltpu.VMEM((t + 2 * CONV_PAD, BRANCH_W), F32),
                        pltpu.VMEM((H_A, t, LANES), F32), pltpu.VMEM((H_A, t, LANES), F32),
                        pltpu.VMEM((H_A, t, LANES), F32),
                        pltpu.VMEM((t, LANES), F32), pltpu.VMEM((t, LANES), F32),
                        pltpu.VMEM((nhd, t, DV_A), BF16), pltpu.VMEM((nhd, t, DK_A), BF16),
                        pltpu.VMEM((nhd, t, DK_A), BF16), pltpu.VMEM((nhd, t, DK_A), BF16),
                        pltpu.VMEM((nhd, t, CHUNK_A), BF16),
                        pltpu.VMEM((nhd, t, DV_A), F32), pltpu.VMEM((nhd, DK_A, DV_A), F32)],
        compiler_params=_cparams("parallel"),
        name="gdn",
    )(y, y, y, y, gates, conv_w, alog_lanes, dtb_lanes, norm_g, s0)


def _boundary_rows(bc, w, fwd):
    c = bc.shape[0]
    if w >= 4:
        pieces = []
        for start in range(0, c, 2 * w):
            r = start + w - 1 if fwd else start + w
            pieces.append(jnp.broadcast_to(bc[r:r + 1, :], (2 * w, LANES)))
        return pieces[0] if len(pieces) == 1 else jnp.concatenate(pieces, axis=0)
    pos = lax.broadcasted_iota(jnp.int32, (c, LANES), 0) % (2 * w)
    target = w - 1 if fwd else w
    out = bc
    for p in range(2 * w):
        if p != target:
            out = jnp.where(pos == p, _shift_rows(bc, target - p), out)
    return out


def _hgrn_levels():
    w = HGRN_CHUNK // 2
    while w >= 1:
        yield w
        w //= 2


def _hgrn_constants():
    idx = np.arange(HGRN_CHUNK)
    t, s = idx[:, None], idx[None, :]
    tri = np.stack([t >= s, t <= s]).astype(np.float32)
    pair = []
    for w in _hgrn_levels():
        same = (t // (2 * w)) == (s // (2 * w))
        upper_t, upper_s = (t % (2 * w)) >= w, (s % (2 * w)) >= w
        pair.append(np.stack([same & upper_t & ~upper_s, same & ~upper_t & upper_s]))
    return jnp.asarray(tri, BF16), jnp.asarray(np.stack(pair, axis=1).astype(np.float32))


def _hgrn_tiles(q, k, v, logf, st, tri, pair_ref):
    c = q.shape[1]
    bc = None
    rem = logf
    for _ in range(3):
        part = rem.astype(BF16)
        rem = rem - part.astype(F32)
        r = jnp.einsum('bij,bjk->bik', tri, part, preferred_element_type=F32)
        bc = r if bc is None else bc + r
    nb = q.shape[0]
    q_lo, k_lo = q.astype(BF16), k.astype(BF16)
    amat = None
    for lvl, w in enumerate(_hgrn_levels()):
        bm = jnp.stack([_boundary_rows(bc[i], w, i % 2 == 0) for i in range(nb)], axis=0)
        e = jnp.exp2(jnp.abs(bc - bm) * (-LOG2E)).astype(BF16)
        pair = jnp.concatenate([pair_ref[:, lvl]] * (nb // 2), axis=0)
        g = _bdot_t(q_lo * e, k_lo * e) * pair
        amat = g if amat is None else amat + g
    o = _bdot(amat, v) + jnp.sum(q * k, axis=-1, keepdims=True) * v
    tot = jnp.stack([bc[i, c - 1:c, :] if i % 2 == 0 else bc[i, 0:1, :] for i in range(nb)], axis=0)
    inc = _bdot_tl(v, k * jnp.exp(tot - bc))
    decay = jnp.exp(tot)
    states = [st]
    for j in range(nb // 2):
        states.append(states[-1] * decay[2 * j:2 * j + 2] + inc[2 * j:2 * j + 2])
    o = o + _bdot_t(q * jnp.exp(bc), jnp.concatenate(states[:-1], axis=0))
    return o, states[-1]


def _hgrn_kernel(q_ref, ff_ref, fr_ref, i_ref, g_ref, lbf_ref, lbr_ref, ng_ref, s0_ref, tri_ref, pair_ref,
                 o_ref, sfin_ref, of_scr, or_scr):
    t = q_ref.shape[0]
    n_tiles = t // HGRN_CHUNK
    tri = jnp.concatenate([tri_ref[...]] * HGRN_UNROLL, axis=0)

    def load(sl, f_ref, lb_ref):
        lb = lb_ref[...]
        f = jnp.maximum(lb + (1.0 - lb) * jax.nn.sigmoid(f_ref[sl, :].astype(F32)), F_FLOOR)
        return _silu(q_ref[sl, :].astype(F32)), 1.0 - f, i_ref[sl, :].astype(F32), jnp.log(f)

    def body(i, st):
        slices, loaded = [], []
        for j in range(HGRN_UNROLL):
            n = i * HGRN_UNROLL + j
            slf = pl.ds(pl.multiple_of(n * HGRN_CHUNK, HGRN_CHUNK), HGRN_CHUNK)
            slr = pl.ds(pl.multiple_of((n_tiles - 1 - n) * HGRN_CHUNK, HGRN_CHUNK), HGRN_CHUNK)
            slices += [slf, slr]
            loaded += [load(slf, ff_ref, lbf_ref), load(slr, fr_ref, lbr_ref)]
        o, st = _hgrn_tiles(*[jnp.stack(parts, axis=0) for parts in zip(*loaded)], st, tri, pair_ref)
        for idx, sl in enumerate(slices):
            (of_scr if idx % 2 == 0 else or_scr)[sl, :] = o[idx]
        return st

    st = lax.fori_loop(0, n_tiles // HGRN_UNROLL, body,
                       jnp.stack([s0_ref[0, 0, 0, 0].T, s0_ref[0, 0, 1, 0].T], axis=0))
    sfin_ref[0, 0, 0] = st[0].T
    sfin_ref[0, 1, 0] = st[1].T
    o = of_scr[...] + or_scr[...]
    o_ref[...] = (_rms(o, ng_ref[...]) * _silu(g_ref[...].astype(F32))).astype(o_ref.dtype)


def _hgrn(y, lb, norm_g, s0, layer, bsz, t):
    col = lambda cb: pl.BlockSpec((t, LANES), lambda b, h: (b, cb + h))
    tri, pair = _hgrn_constants()
    return pl.pallas_call(
        _hgrn_kernel,
        grid=(bsz, H_C),
        in_specs=[col(CB_CQ), col(CB_CFF), col(CB_CFB), col(CB_CI), col(CB_CG),
                  pl.BlockSpec((1, LANES), lambda b, h: (0, h)),
                  pl.BlockSpec((1, LANES), lambda b, h: (0, H_C + h)),
                  pl.BlockSpec((1, LANES), lambda b, h: (0, 0)),
                  pl.BlockSpec((1, 1, 2, 1, DK_C, DV_C), lambda b, h: (b, layer, 0, h, 0, 0)),
                  pl.BlockSpec(tri.shape, lambda b, h: (0, 0, 0)),
                  pl.BlockSpec(pair.shape, lambda b, h: (0, 0, 0, 0))],
        out_specs=[pl.BlockSpec((t, LANES), lambda b, h: (b, h)),
                   pl.BlockSpec((1, 2, 1, DK_C, DV_C), lambda b, h: (b, 0, h, 0, 0))],
        out_shape=[jax.ShapeDtypeStruct((bsz * t, BRANCH_W), BF16),
                   jax.ShapeDtypeStruct((bsz, 2, H_C, DK_C, DV_C), F32)],
        scratch_shapes=[pltpu.VMEM((t, LANES), F32), pltpu.VMEM((t, LANES), F32)],
        compiler_params=_cparams("parallel", "parallel"),
        name="hgrn",
    )(y, y, y, y, y, lb, lb, norm_g, s0, tri, pair)


def _diff_lambda(lam_ref, lam_init):
    lp = lam_ref[...]
    return (jnp.exp(jnp.sum(lp[0:1] * lp[1:2], axis=-1, keepdims=True))
            - jnp.exp(jnp.sum(lp[2:3] * lp[3:4], axis=-1, keepdims=True)) + lam_init)


def _diff_core(problems, lam, ng, lam_init):
    scores = []
    for q, k, _ in problems:
        lane = lax.broadcasted_iota(jnp.int32, q.shape, 1)
        zero = jnp.zeros_like(q)
        scores.append((_dot_t(jnp.where(lane < DK_B, q, zero), k), _dot_t(jnp.where(lane >= DK_B, q, zero), k)))
    outs = []
    for (s1, s2), (_, _, v) in zip(scores, problems):
        e1, r1 = _softmax_parts(s1)
        e2, r2 = _softmax_parts(s2)
        outs.append(_dot(e1, v) * r1 - _dot(e2, v) * (lam * r2))
    return [_rms(o, ng) * (1.0 - lam_init) for o in outs]


def _diff_ctx_kernel(lam_ref, q_ref, k_ref, v_ref, ng_ref, o_ref, *, lam_init):
    lam = _diff_lambda(lam_ref, lam_init)
    heads = [slice(h * LANES, (h + 1) * LANES) for h in range(H_B)]
    outs = _diff_core([((q_ref[:, sl].astype(F32) * DIFF_QSCALE).astype(BF16), k_ref[:, sl], v_ref[:, sl])
                       for sl in heads], lam, ng_ref[...], lam_init)
    for sl, o in zip(heads, outs):
        o_ref[:, sl] = o.astype(o_ref.dtype)


def _diff_ctx(y, lam_p, norm_g, lam_init, bsz, t):
    blk = lambda cb: pl.BlockSpec((t, BRANCH_W), lambda b: (b, cb // H_B))
    return pl.pallas_call(
        functools.partial(_diff_ctx_kernel, lam_init=lam_init),
        grid=(bsz,),
        in_specs=[pl.BlockSpec((4, DK_B), lambda b: (0, 0)), blk(CB_BQ), blk(CB_BK), blk(CB_BV),
                  pl.BlockSpec((1, LANES), lambda b: (0, 0))],
        out_specs=pl.BlockSpec((t, BRANCH_W), lambda b: (b, 0)),
        out_shape=jax.ShapeDtypeStruct((bsz * t, BRANCH_W), BF16),
        compiler_params=_cparams("parallel"),
        name="diff_ctx",
    )(lam_p, y, y, y, norm_g)


def _rope(x, cos, sin_signed):
    lane = lax.broadcasted_iota(jnp.int32, x.shape, 1)
    rot = jnp.where((lane % DK_B) < DK_B // 2, pltpu.roll(x, LANES - DK_B // 2, 1), pltpu.roll(x, DK_B // 2, 1))
    return x * cos + rot * sin_signed


def _diff_lat_kernel(lam_ref, q_ref, k_ref, v_ref, ck_ref, cv_ref, cosq_ref, sinq_ref, cos_ref, sin_ref, ng_ref,
                     o_ref, k_scr, v_scr, *, lam_init):
    past = ck_ref.shape[3]

    @pl.when(pl.program_id(2) == 0)
    def _():
        k_scr[:past, :] = ck_ref[0, 0, 0].astype(BF16)
        v_scr[:past, :] = cv_ref[0, 0, 0].astype(BF16)
        k_scr[past:, :] = _rope(k_ref[...].astype(F32), cos_ref[...], sin_ref[...]).astype(BF16)
        v_scr[past:, :] = v_ref[...]

    lam = _diff_lambda(lam_ref, lam_init)
    q = (_rope(q_ref[...].astype(F32), cosq_ref[...], sinq_ref[...]) * DIFF_QSCALE).astype(BF16)
    k, v = k_scr[...], v_scr[...]
    half = q.shape[0] // 2
    outs = _diff_core([(q[:half], k, v), (q[half:], k, v)], lam, ng_ref[...], lam_init)
    o_ref[:half, :] = outs[0].astype(o_ref.dtype)
    o_ref[half:, :] = outs[1].astype(o_ref.dtype)


def _diff_lat(y, ck, cv, cos, sin_signed, lam_p, norm_g, lam_init, layer, bsz, t):
    tq = 512
    nq = t // tq
    past = ck.shape[3]
    full = lambda cb: pl.BlockSpec((t, LANES), lambda b, h, i: (b, cb + h))
    cache = lambda: pl.BlockSpec((1, 1, 1, past, LANES), lambda b, h, i: (b, layer, h, 0, 0))
    tab_q = lambda: pl.BlockSpec((tq, LANES), lambda b, h, i: (i, 0))
    tab = lambda: pl.BlockSpec((t, LANES), lambda b, h, i: (0, 0))
    return pl.pallas_call(
        functools.partial(_diff_lat_kernel, lam_init=lam_init),
        grid=(bsz, H_B, nq),
        in_specs=[pl.BlockSpec((4, DK_B), lambda b, h, i: (0, 0)),
                  pl.BlockSpec((tq, LANES), lambda b, h, i: (b * nq + i, CB_BQ + h)),
                  full(CB_BK), full(CB_BV), cache(), cache(), tab_q(), tab_q(), tab(), tab(),
                  pl.BlockSpec((1, LANES), lambda b, h, i: (0, 0))],
        out_specs=pl.BlockSpec((tq, LANES), lambda b, h, i: (b * nq + i, h)),
        out_shape=jax.ShapeDtypeStruct((bsz * t, BRANCH_W), BF16),
        scratch_shapes=[pltpu.VMEM((past + t, LANES), BF16), pltpu.VMEM((past + t, LANES), BF16)],
        compiler_params=_cparams("parallel", "parallel", "arbitrary"),
        name="diff_lat",
    )(lam_p, y, y, y, ck, cv, cos, sin_signed, cos, sin_signed, norm_g)


def _pair_attend(q, keys, vals, bias=None):
    lane = lax.broadcasted_iota(jnp.int32, q.shape, 1)
    zero = jnp.zeros_like(q)
    scores = [[_dot_t(jnp.where((lane < DH_D) == (par == 0), q, zero), kk) for kk in keys] for par in range(2)]
    outs = []
    for par in range(2):
        s = scores[par]
        if bias is not None:
            s[-1] = jnp.where(bias[par] > 0.5 * MASK_VALUE, s[-1] + bias[par], MASK_VALUE)
        m = s[0].max(axis=-1, keepdims=True)
        for x in s[1:]:
            m = jnp.maximum(m, x.max(axis=-1, keepdims=True))
        e = [jnp.exp2(x - m) for x in s]
        den = e[0].sum(axis=-1, keepdims=True)
        for x in e[1:]:
            den = den + x.sum(axis=-1, keepdims=True)
        o = _dot(e[0], vals[0])
        for x, vv in zip(e[1:], vals[1:]):
            o = o + _dot(x, vv)
        outs.append(o * (1.0 / den))
    return jnp.where(lane < DH_D, outs[0], outs[1])


def _na_ctx_kernel(q_ref, k_ref, v_ref, o_ref):
    for p in range(H_D // 2):
        sl = slice(p * LANES, (p + 1) * LANES)
        q = (q_ref[:, sl].astype(F32) * NA_QSCALE).astype(BF16)
        o_ref[:, sl] = _pair_attend(q, [k_ref[:, sl]], [v_ref[:, sl]]).astype(o_ref.dtype)


def _na_ctx(y, bsz, t):
    blk = lambda cb: pl.BlockSpec((t, BRANCH_W), lambda b: (b, cb // 4))
    return pl.pallas_call(
        _na_ctx_kernel,
        grid=(bsz,),
        in_specs=[blk(CB_DQ), blk(CB_DK), blk(CB_DV)],
        out_specs=pl.BlockSpec((t, BRANCH_W), lambda b: (b, 0)),
        out_shape=jax.ShapeDtypeStruct((bsz * t, BRANCH_W), BF16),
        compiler_params=_cparams("parallel"),
        name="na_ctx",
    )(y, y, y)


def _na_bias_kernel(rpb_ref, o_ref):
    hd = pl.program_id(0)
    qc = lax.broadcasted_iota(jnp.int32, (GRID_W, LANES), 0)
    lane = lax.broadcasted_iota(jnp.int32, (GRID_W, LANES), 1)
    kc = lane % GRID_W
    dc = jnp.clip(kc - qc + WIN_C - 1, 0, 2 * WIN_C - 2)
    c0 = jnp.clip(qc - WIN_C // 2, 0, GRID_W - WIN_C)
    col_ok = (kc >= c0) & (kc < c0 + WIN_C)
    n_dr = 2 * WIN_R - 1
    tables = []
    for dr in range(n_dr):
        acc = jnp.zeros((GRID_W, LANES), F32)
        for d in range(2 * WIN_C - 1):
            acc = jnp.where(dc == d, rpb_ref[hd, dr * (2 * WIN_C - 1) + d], acc)
        tables.append(jnp.where(col_ok, acc * LOG2E, MASK_VALUE))
    masked = jnp.full((GRID_W, LANES), MASK_VALUE, F32)
    for i in range(n_dr + 1):
        lo = tables[i - 1] if i >= 1 else masked
        hi = tables[i] if i < n_dr else masked
        o_ref[0, i] = jnp.where(lane < GRID_W, lo, hi)


def _na_bias(rpb):
    n_dr = 2 * WIN_R - 1
    return pl.pallas_call(
        _na_bias_kernel,
        grid=(H_D,),
        in_specs=[pl.BlockSpec(memory_space=pltpu.SMEM)],
        out_specs=pl.BlockSpec((1, n_dr + 1, GRID_W, LANES), lambda h: (h, 0, 0, 0)),
        out_shape=jax.ShapeDtypeStruct((H_D, n_dr + 1, GRID_W, LANES), F32),
        compiler_params=_cparams("parallel"),
        name="na_bias",
    )(rpb.reshape(H_D, n_dr * (2 * WIN_C - 1)))


def _na_lat_kernel(q_ref, k_ref, v_ref, ck_ref, cv_ref, tab_ref, o_ref, *, rows):
    wr = min(WIN_R, rows)
    lane = lax.broadcasted_iota(jnp.int32, (GRID_W, LANES), 1)
    masked = jnp.full((GRID_W, LANES), MASK_VALUE, F32)
    ck = jnp.concatenate([ck_ref[0, 0, 0], ck_ref[0, 0, 1]], axis=1).astype(BF16)
    cv = jnp.concatenate([cv_ref[0, 0, 0], cv_ref[0, 0, 1]], axis=1).astype(BF16)
    first_row = lambda r: min(max(r - wr // 2, 0), rows - wr)
    for g in range(rows // NA_QROWS):
        starts = [first_row(g * NA_QROWS + qi) for qi in range(NA_QROWS)]
        kp0, kp1 = min(starts) // 2, (max(starts) + wr + 1) // 2
        k_lat = k_ref[2 * kp0 * GRID_W:2 * kp1 * GRID_W, :]
        v_lat = v_ref[2 * kp0 * GRID_W:2 * kp1 * GRID_W, :]
        bias = []
        for par in range(2):
            blocks = []
            for qi in range(NA_QROWS):
                r = g * NA_QROWS + qi
                r0 = first_row(r)
                tiles = []
                for kp in range(kp0, kp1):
                    ok0 = r0 <= 2 * kp < r0 + wr
                    ok1 = r0 <= 2 * kp + 1 < r0 + wr
                    if not (ok0 or ok1):
                        tiles.append(masked)
                        continue
                    tile = tab_ref[par, 2 * kp - r + WIN_R]
                    if not ok0:
                        tile = jnp.where(lane >= GRID_W, tile, MASK_VALUE)
                    if not ok1:
                        tile = jnp.where(lane < GRID_W, tile, MASK_VALUE)
                    tiles.append(tile)
                blocks.append(jnp.concatenate(tiles, axis=1))
            bias.append(jnp.concatenate(blocks, axis=0))
        sl = slice(g * NA_QROWS * GRID_W, (g + 1) * NA_QROWS * GRID_W)
        q = (q_ref[sl, :].astype(F32) * NA_QSCALE).astype(BF16)
        o_ref[sl, :] = _pair_attend(q, [ck, k_lat], [cv, v_lat], bias).astype(o_ref.dtype)


def _na_lat(y, ck, cv, tables, layer, bsz, t):
    rows = t // GRID_W
    past = ck.shape[3]
    n_tab = tables.shape[1]
    col = lambda cb: pl.BlockSpec((t, LANES), lambda b, p: (b, cb + p))
    cache = lambda: pl.BlockSpec((1, 1, 2, past, DH_D), lambda b, p: (b, layer, p, 0, 0))
    return pl.pallas_call(
        functools.partial(_na_lat_kernel, rows=rows),
        grid=(bsz, H_D // 2),
        in_specs=[col(CB_DQ), col(CB_DK), col(CB_DV), cache(), cache(),
                  pl.BlockSpec((2, n_tab, GRID_W, LANES), lambda b, p: (p, 0, 0, 0))],
        out_specs=pl.BlockSpec((t, LANES), lambda b, p: (b, p)),
        out_shape=jax.ShapeDtypeStruct((bsz * t, BRANCH_W), BF16),
        compiler_params=_cparams("parallel", "parallel"),
        name="na_lat",
    )(y, y, y, ck, cv, tables)


def _axial_rope_tables(n_tok):
    tok = jnp.arange(n_tok)
    n_freq = DK_B // 4
    inv = ROPE_BASE ** (-jnp.arange(n_freq, dtype=F32) / n_freq)
    ang = jnp.concatenate([(tok // GRID_W).astype(F32)[:, None] * inv,
                           (tok % GRID_W).astype(F32)[:, None] * inv], axis=-1)
    cos, sin = jnp.cos(ang), jnp.sin(ang)
    return jnp.tile(cos, (1, 4)), jnp.tile(jnp.concatenate([-sin, sin], axis=-1), (1, 2))


W_TILE = 512
W_A_END = 4 * BRANCH_W
W_GATES = 4 * H_A
W_MG_START = W_A_END + W_GATES + 11 * BRANCH_W


def _permute_w_in_kernel(wide_ref, next_ref, o_ref):
    j = pl.program_id(1)
    n_mg, n_a = N_BRANCH * D_MODEL // W_TILE, W_A_END // W_TILE
    aligned = (j >= n_mg) & (j < n_mg + n_a)

    @pl.when(aligned)
    def _():
        o_ref[0] = wide_ref[0].astype(BF16)

    @pl.when(~aligned)
    def _():
        o_ref[0] = jnp.concatenate([wide_ref[0][:, W_GATES:], next_ref[0][:, :W_GATES]], axis=1).astype(BF16)


def _permute_w_in(w):
    n_mg = N_BRANCH * D_MODEL // W_TILE
    mg_tile0 = (W_MG_START - W_GATES) // W_TILE
    lane_tiles = W_TILE // LANES
    wide = lambda l, j: (l, 0, jnp.where(j < n_mg, mg_tile0 + j, j - n_mg))
    nxt = lambda l, j: (l, 0, lane_tiles * (jnp.where(j < n_mg, mg_tile0 + j, j - n_mg) + 1))
    main = pl.pallas_call(
        _permute_w_in_kernel,
        grid=(DEPTH, Y_COLS // W_TILE),
        in_specs=[pl.BlockSpec((1, D_MODEL, W_TILE), wide), pl.BlockSpec((1, D_MODEL, LANES), nxt)],
        out_specs=pl.BlockSpec((1, D_MODEL, W_TILE), lambda l, j: (l, 0, j)),
        out_shape=jax.ShapeDtypeStruct((DEPTH, D_MODEL, Y_COLS), BF16),
        compiler_params=_cparams("parallel", "parallel"),
        name="permute_w_in",
    )(w, w)
    gate = jnp.pad(w[..., W_A_END:W_A_END + W_GATES], ((0, 0), (0, 0), (0, GATE_COLS - W_GATES)))
    return main, gate.astype(BF16)


def _ctx_cache_kernel(*refs):
    ins, (bk_out, bv_out, dk_out, dv_out) = refs[:-4], refs[-4:]
    for layer in range(DEPTH):
        bk, bv, dk, dv = ins[4 * layer:4 * layer + 4]

        @pl.when(pl.program_id(1) == layer)
        def _():
            for h in range(H_B):
                sl = slice(h * LANES, (h + 1) * LANES)
                bk_out[0, 0, h] = bk[:, sl].astype(F32)
                bv_out[0, 0, h] = bv[:, sl].astype(F32)
            for h in range(H_D):
                sl = slice(h * DH_D, (h + 1) * DH_D)
                dk_out[0, 0, h] = dk[:, sl].astype(F32)
                dv_out[0, 0, h] = dv[:, sl].astype(F32)


def _ctx_caches(ys, bsz, t):
    blk = lambda cb: pl.BlockSpec((t, BRANCH_W), lambda b, l: (b, cb // 4))
    out = lambda heads, width: pl.BlockSpec((1, 1, heads, t, width), lambda b, l: (b, l, 0, 0, 0))
    shape = lambda heads, width: jax.ShapeDtypeStruct((bsz, DEPTH, heads, t, width), F32)
    return pl.pallas_call(
        _ctx_cache_kernel,
        grid=(bsz, DEPTH),
        in_specs=[blk(cb) for _ in ys for cb in (CB_BK, CB_BV, CB_DK, CB_DV)],
        out_specs=[out(H_B, 2 * DK_B), out(H_B, DV_B), out(H_D, DH_D), out(H_D, DH_D)],
        out_shape=[shape(H_B, 2 * DK_B), shape(H_B, DV_B), shape(H_D, DH_D), shape(H_D, DH_D)],
        compiler_params=_cparams("parallel", "parallel"),
        name="ctx_caches",
    )(*[y for y in ys for _ in range(4)])


def kernel(x_prompt, x_sample, c, state_gdn, cache_diff_k, cache_diff_v, state_hgrn, cache_na_k, cache_na_v, c_ctx,
           w_ada, b_ada, norm1_g, w_in, gdn_conv_w, gdn_A_log, gdn_dt_bias, gdn_norm_g, diff_lambda, diff_norm_g,
           hgrn_lb_logits, hgrn_norm_g, na_rpb, w_branch, w_out, norm2_g, w_ffn_gate, w_ffn_up, w_ffn_down,
           final_norm_g):
    bp, tp, _ = x_prompt.shape
    bs, ts, _ = x_sample.shape

    cond = jnp.zeros((16, D_MODEL), F32).at[:bs].set(c).at[bs].set(c_ctx)
    mods = _adaln(cond, w_ada, b_ada)
    probs = jax.nn.softmax(hgrn_lb_logits.astype(F32), axis=0)
    lb_all = jnp.cumsum(probs, axis=0) - probs[0:1]
    cos, sin_signed = _axial_rope_tables(ts)
    zeros_gdn = jnp.zeros((bp, 1, 2, H_A, DK_A, DV_A), F32)
    zeros_hgrn = jnp.zeros((bp, 1, 2, H_C, DK_C, DV_C), F32)
    w_main, w_gate = _permute_w_in(w_in)
    wbr, wo = w_branch.astype(BF16), w_out.astype(BF16)
    wfg, wfu, wfd = w_ffn_gate.astype(BF16), w_ffn_up.astype(BF16), w_ffn_down.astype(BF16)

    xp = x_prompt.reshape(bp * tp, D_MODEL)
    xs = x_sample.reshape(bs * ts, D_MODEL)
    new_ctx = []
    for l in range(DEPTH):
        lam_init = 0.8 - 0.6 * math.exp(-0.3 * l)
        conv_w = jnp.zeros((8, 3 * BRANCH_W), F32).at[:CONV_W].set(gdn_conv_w[l].T)
        lanes = lambda p: jnp.pad(p.astype(F32).reshape(1, 2 * H_A), ((0, 0), (0, LANES - 2 * H_A)))
        alog, dtb = lanes(gdn_A_log[l]), lanes(gdn_dt_bias[l])
        lb = lb_all[l].reshape(1, 2 * H_C * DK_C)
        vecs = dict(n1=norm1_g[l].reshape(1, D_MODEL), n2=norm2_g[l].reshape(1, D_MODEL),
                    fin=final_norm_g.reshape(1, D_MODEL), gdn=gdn_norm_g[l].reshape(1, LANES),
                    diff=diff_norm_g[l].reshape(1, LANES), hgrn=hgrn_norm_g[l].reshape(1, LANES))
        tables = _na_bias(na_rpb[l])
        final = l == DEPTH - 1

        def dense_tail(x, mod, y, branches, tok_per_row):
            x = _merge(x, mod, y, branches, wbr, wo, l, tok_per_row)
            return _ffn(x, mod, vecs["n2"], vecs["fin"], wfg, wfu, wfd, l, tok_per_row, final)

        mod = mods[l, bs:bs + 1].reshape(1, 1, 6 * D_MODEL)
        y, gates = _inproj(xp, mod, vecs["n1"], w_main, w_gate, l, bp * tp)
        o_a, st_gdn = _gdn(y, gates, conv_w, alog, dtb, vecs["gdn"], zeros_gdn, 0, bp, tp)
        o_b = _diff_ctx(y, diff_lambda[l], vecs["diff"], lam_init, bp, tp)
        o_c, st_hgrn = _hgrn(y, lb, vecs["hgrn"], zeros_hgrn, 0, bp, tp)
        o_d = _na_ctx(y, bp, tp)
        xp = dense_tail(xp, mod, y, (o_a, o_b, o_c, o_d), bp * tp)
        new_ctx.append((st_gdn, st_hgrn, y))

        mod = mods[l, :bs].reshape(bs, 1, 6 * D_MODEL)
        y, gates = _inproj(xs, mod, vecs["n1"], w_main, w_gate, l, ts)
        o_a, _ = _gdn(y, gates, conv_w, alog, dtb, vecs["gdn"], state_gdn, l, bs, ts)
        o_b = _diff_lat(y, cache_diff_k, cache_diff_v, cos, sin_signed, diff_lambda[l], vecs["diff"], lam_init,
                        l, bs, ts)
        o_c, _ = _hgrn(y, lb, vecs["hgrn"], state_hgrn, l, bs, ts)
        o_d = _na_lat(y, cache_na_k, cache_na_v, tables, l, bs, ts)
        xs = dense_tail(xs, mod, y, (o_a, o_b, o_c, o_d), ts)

    stack = lambda i: jnp.stack([n[i] for n in new_ctx], axis=1)
    diff_k, diff_v, na_k, na_v = _ctx_caches([n[2] for n in new_ctx], bp, tp)
    return (xp.reshape(bp, tp, D_MODEL), xs.reshape(bs, ts, D_MODEL), stack(0), diff_k, diff_v, stack(1),
            na_k, na_v)
```

```python
import functools
import math

import jax
import jax.numpy as jnp
import numpy as np
from jax import lax
from jax.experimental import pallas as pl
from jax.experimental.pallas import tpu as pltpu

F32 = jnp.float32
BF16 = jnp.bfloat16

D_MODEL = 1024
DEPTH = 2
GRID_W = 64
N_BRANCH = 4
H_A, DK_A, DV_A, CONV_W, CHUNK_A = 4, 128, 128, 5, 64
H_B, DK_B, DV_B = 4, 64, 128
H_C, DK_C, DV_C = 4, 128, 128
H_D, DH_D, WIN_R, WIN_C = 8, 64, 8, 16
BRANCH_W = 512
ROPE_BASE = 10000.0
D_FF = 2816
EPS = 1e-6
MASK_VALUE = -1e30
F_FLOOR = 1e-30
LOG2E = math.log2(math.e)
DIFF_QSCALE = DK_B ** -0.5 * LOG2E
NA_QSCALE = DH_D ** -0.5 * LOG2E

LANES = 128
VMEM_LIMIT_BYTES = 56 * 1024 * 1024

Y_COLS = 11776
CB_MG = 0
CB_AQ, CB_AK, CB_AV, CB_AZ = 32, 36, 40, 44
CB_BQ, CB_BK, CB_BV = 48, 52, 56
CB_CQ, CB_CFF, CB_CFB, CB_CI, CB_CG = 60, 64, 68, 72, 76
CB_DQ, CB_DK, CB_DV = 80, 84, 88
GATE_COLS = LANES

GDN_CHUNKS_PER_ITER = 4
CONV_PAD = 8
CONV_ROWS = 128
HGRN_CHUNK = 128
HGRN_UNROLL = 4
NA_QROWS = 4


def _cparams(*sem):
    return pltpu.CompilerParams(dimension_semantics=sem, vmem_limit_bytes=VMEM_LIMIT_BYTES)


def _dot(a, b):
    return jnp.dot(a.astype(BF16), b.astype(BF16), preferred_element_type=F32)


def _dot_t(a, b):
    return lax.dot_general(a.astype(BF16), b.astype(BF16), (((1,), (1,)), ((), ())), preferred_element_type=F32)


def _silu(x):
    return x * jax.nn.sigmoid(x)


def _rms(x, g):
    return x * lax.rsqrt(jnp.mean(x * x, axis=-1, keepdims=True) + EPS) * g


def _softmax_parts(s):
    m = jnp.max(s, axis=-1, keepdims=True)
    e = jnp.exp2(s - m)
    return e, 1.0 / jnp.sum(e, axis=-1, keepdims=True)


def _adaln_kernel(c_ref, w_ref, b_ref, o_ref):
    s = _silu(c_ref[...])
    o_ref[0] = jnp.dot(s, w_ref[0], precision=lax.Precision.HIGHEST, preferred_element_type=F32) + b_ref[0]


def _adaln(cond, w_ada, b_ada):
    rows = cond.shape[0]
    tn = 1024
    n_out = w_ada.shape[-1]
    return pl.pallas_call(
        _adaln_kernel,
        grid=(DEPTH, n_out // tn),
        in_specs=[pl.BlockSpec((rows, D_MODEL), lambda l, j: (0, 0)),
                  pl.BlockSpec((1, D_MODEL, tn), lambda l, j: (l, 0, j)),
                  pl.BlockSpec((1, 1, tn), lambda l, j: (l, 0, j))],
        out_specs=pl.BlockSpec((1, rows, tn), lambda l, j: (l, 0, j)),
        out_shape=jax.ShapeDtypeStruct((DEPTH, rows, n_out), F32),
        compiler_params=_cparams("parallel", "parallel"),
        name="adaln",
    )(cond, w_ada, b_ada.reshape(DEPTH, 1, n_out))


def _inproj_kernel(x_ref, sh_ref, sc_ref, g_ref, w_ref, wg_ref, y_ref, gate_ref, h_scr):
    @pl.when(pl.program_id(1) == 0)
    def _():
        h = _rms(x_ref[...], g_ref[...]) * (1.0 + sc_ref[0]) + sh_ref[0]
        hb = h.astype(BF16)
        h_scr[...] = hb
        gate_ref[...] = jnp.dot(hb, wg_ref[0], preferred_element_type=F32)

    y_ref[...] = jnp.dot(h_scr[...], w_ref[0], preferred_element_type=F32).astype(y_ref.dtype)


def _inproj(x, mod, norm_g, w, wg, layer, tok_per_row):
    m = x.shape[0]
    tm, tn = min(1024, m), Y_COLS // 4
    mrow = lambda c: pl.BlockSpec((1, 1, D_MODEL), lambda i, j: ((i * tm) // tok_per_row, 0, c))
    return pl.pallas_call(
        _inproj_kernel,
        grid=(m // tm, Y_COLS // tn),
        in_specs=[pl.BlockSpec((tm, D_MODEL), lambda i, j: (i, 0)),
                  mrow(0), mrow(1),
                  pl.BlockSpec((1, D_MODEL), lambda i, j: (0, 0)),
                  pl.BlockSpec((1, D_MODEL, tn), lambda i, j: (layer, 0, j)),
                  pl.BlockSpec((1, D_MODEL, GATE_COLS), lambda i, j: (layer, 0, 0))],
        out_specs=[pl.BlockSpec((tm, tn), lambda i, j: (i, j)),
                   pl.BlockSpec((tm, GATE_COLS), lambda i, j: (i, 0))],
        out_shape=[jax.ShapeDtypeStruct((m, Y_COLS), BF16),
                   jax.ShapeDtypeStruct((m, GATE_COLS), F32)],
        scratch_shapes=[pltpu.VMEM((tm, D_MODEL), BF16)],
        compiler_params=_cparams("parallel", "arbitrary"),
        name="inproj",
    )(x, mod, mod, norm_g, w, wg)


def _merge_kernel(x_ref, g1_ref, mg_ref, oa_ref, ob_ref, oc_ref, od_ref, wbr_ref, wout_ref, xo_ref):
    acc = None
    for n, o_ref in enumerate((oa_ref, ob_ref, oc_ref, od_ref)):
        proj = jnp.dot(o_ref[...], wbr_ref[0, n], preferred_element_type=F32)
        gate = jax.nn.sigmoid(mg_ref[:, n * D_MODEL:(n + 1) * D_MODEL].astype(F32))
        acc = gate * proj if acc is None else acc + gate * proj
    out = jnp.dot(acc.astype(BF16), wout_ref[0], preferred_element_type=F32)
    xo_ref[...] = x_ref[...] + g1_ref[0] * out


def _merge(x, mod, y, branches, w_branch, w_out, layer, tok_per_row):
    m = x.shape[0]
    tm = 512
    tok = lambda: pl.BlockSpec((tm, BRANCH_W), lambda i: (i, 0))
    return pl.pallas_call(
        _merge_kernel,
        grid=(m // tm,),
        in_specs=[pl.BlockSpec((tm, D_MODEL), lambda i: (i, 0)),
                  pl.BlockSpec((1, 1, D_MODEL), lambda i: ((i * tm) // tok_per_row, 0, 2)),
                  pl.BlockSpec((tm, N_BRANCH * D_MODEL), lambda i: (i, 0)),
                  tok(), tok(), tok(), tok(),
                  pl.BlockSpec((1, N_BRANCH, BRANCH_W, D_MODEL), lambda i: (layer, 0, 0, 0)),
                  pl.BlockSpec((1, D_MODEL, D_MODEL), lambda i: (layer, 0, 0))],
        out_specs=pl.BlockSpec((tm, D_MODEL), lambda i: (i, 0)),
        out_shape=jax.ShapeDtypeStruct((m, D_MODEL), F32),
        compiler_params=_cparams("parallel"),
        name="merge",
    )(x, mod, y, *branches, w_branch, w_out)


def _ffn_kernel(x_ref, sh_ref, sc_ref, g2_ref, ng_ref, fg_ref, wg_ref, wu_ref, wd_ref, xo_ref, h_scr, acc_scr,
                *, final):
    f = pl.program_id(1)

    @pl.when(f == 0)
    def _():
        h = _rms(x_ref[...], ng_ref[...]) * (1.0 + sc_ref[0]) + sh_ref[0]
        h_scr[...] = h.astype(BF16)
        acc_scr[...] = jnp.zeros_like(acc_scr)

    h = h_scr[...]
    a = jnp.dot(h, wg_ref[0], preferred_element_type=F32)
    u = jnp.dot(h, wu_ref[0], preferred_element_type=F32)
    acc_scr[...] += jnp.dot((_silu(a) * u).astype(BF16), wd_ref[0], preferred_element_type=F32)

    @pl.when(f == pl.num_programs(1) - 1)
    def _():
        xn = x_ref[...] + g2_ref[0] * acc_scr[...]
        xo_ref[...] = _rms(xn, fg_ref[...]) if final else xn


def _ffn(x, mod, norm_g, final_g, wg, wu, wd, layer, tok_per_row, final):
    m = x.shape[0]
    tm, tf = min(1024, m), D_FF // 2
    mrow = lambda c: pl.BlockSpec((1, 1, D_MODEL), lambda i, f: ((i * tm) // tok_per_row, 0, c))
    vec = lambda: pl.BlockSpec((1, D_MODEL), lambda i, f: (0, 0))
    return pl.pallas_call(
        functools.partial(_ffn_kernel, final=final),
        grid=(m // tm, D_FF // tf),
        in_specs=[pl.BlockSpec((tm, D_MODEL), lambda i, f: (i, 0)),
                  mrow(3), mrow(4), mrow(5), vec(), vec(),
                  pl.BlockSpec((1, D_MODEL, tf), lambda i, f: (layer, 0, f)),
                  pl.BlockSpec((1, D_MODEL, tf), lambda i, f: (layer, 0, f)),
                  pl.BlockSpec((1, tf, D_MODEL), lambda i, f: (layer, f, 0))],
        out_specs=pl.BlockSpec((tm, D_MODEL), lambda i, f: (i, 0)),
        out_shape=jax.ShapeDtypeStruct((m, D_MODEL), F32),
        scratch_shapes=[pltpu.VMEM((tm, D_MODEL), BF16), pltpu.VMEM((tm, D_MODEL), F32)],
        compiler_params=_cparams("parallel", "arbitrary"),
        name="ffn",
    )(x, mod, mod, mod, norm_g, final_g, wg, wu, wd)


def _shift_rows(x, off):
    n = x.shape[0]
    return x if off % n == 0 else pltpu.roll(x, (-off) % n, 0)


def _bdot(a, b):
    return jnp.einsum('bij,bjk->bik', a.astype(BF16), b.astype(BF16), preferred_element_type=F32)


def _bdot_t(a, b):
    return jnp.einsum('bik,bjk->bij', a.astype(BF16), b.astype(BF16), preferred_element_type=F32)


def _bdot_tl(a, b):
    return jnp.einsum('bki,bkj->bij', a.astype(BF16), b.astype(BF16), preferred_element_type=F32)


def _unit_tri_inverse(nmat, ii, jj):
    blk16 = (ii // 16) == (jj // 16)
    blk32 = (ii // 32) == (jj // 32)
    eye = (ii == jj).astype(F32)
    d = jnp.where(blk16, nmat, 0.0)
    x = eye - d
    p = _bdot(d, d)
    for _ in range(2):
        x = x + _bdot(x, p)
        p = _bdot(p, p)
    x = x + _bdot(x, p)
    for e in (jnp.where(blk32 & ~blk16, nmat, 0.0), jnp.where(~blk32, nmat, 0.0)):
        x = x - _bdot(x, _bdot(e, x))
    return x


def _gdn_prepare(q, k, v, gcb, rowgc, g_end, betab, sgn, ii, jj):
    c = q.shape[1]
    kb = k * betab
    gram_kk = _bdot_t(kb, k)
    gram_qk = _bdot_t(q, k)
    diff = gcb[:, :, :c] - rowgc
    order = (ii - jj) * sgn
    tri, strict = order >= 0, order > 0
    decay = jnp.where(tri, jnp.exp(jnp.where(tri, diff, 0.0)), 0.0)
    tinv = _unit_tri_inverse(jnp.where(strict, gram_kk * decay, 0.0), ii, jj)
    eg = jnp.exp(gcb)
    sol = _bdot(tinv, jnp.concatenate([v * betab, kb * eg], axis=2))
    a = jnp.where(tri, gram_qk * decay, 0.0)
    return sol[:, :, :DV_A], sol[:, :, DV_A:], a, q * eg, k * jnp.exp(g_end - gcb)


def _gdn_kernel(q_ref, k_ref, v_ref, z_ref, gate_ref, cw_ref, alog_ref, dtb_ref, ng_ref, s0_ref, o_ref, sfin_ref,
                xpad, qs, ks, vs, gc_scr, bt_scr, u_scr, w_scr, qd_scr, kd_scr, a_scr, o_scr, s_scr):
    t = q_ref.shape[0]
    n_chunks = t // CHUNK_A
    row = lax.broadcasted_iota(jnp.int32, (t, LANES), 0)
    lane = lax.broadcasted_iota(jnp.int32, (t, LANES), 1)

    xpad[0:CONV_PAD, :] = jnp.zeros((CONV_PAD, BRANCH_W), F32)
    xpad[t + CONV_PAD:t + 2 * CONV_PAD, :] = jnp.zeros((CONV_PAD, BRANCH_W), F32)

    rb = min(t, CONV_ROWS)
    for seg, (x_ref, dst) in enumerate(((q_ref, qs), (k_ref, ks), (v_ref, vs))):
        xpad[CONV_PAD:t + CONV_PAD, :] = x_ref[...].astype(F32)
        for h in range(H_A):
            lanes = slice(h * LANES, (h + 1) * LANES)
            for r in range(0, t, rb):
                acc = None
                for j in range(CONV_W):
                    r0 = CONV_PAD + r + j - CONV_W // 2
                    wj = cw_ref[j:j + 1, seg * BRANCH_W + h * LANES:seg * BRANCH_W + (h + 1) * LANES]
                    term = xpad[r0:r0 + rb, lanes] * wj
                    acc = term if acc is None else acc + term
                x = _silu(acc)
                if seg == 0:
                    x = x * lax.rsqrt(jnp.sum(x * x, axis=-1, keepdims=True) + EPS) * (DK_A ** -0.5)
                elif seg == 1:
                    x = x * lax.rsqrt(jnp.sum(x * x, axis=-1, keepdims=True) + EPS)
                dst[h, r:r + rb, :] = x

    gt = gate_ref[...]
    a = gt + dtb_ref[...]
    g = -jnp.exp(alog_ref[...]) * (jnp.maximum(a, 0.0) + jnp.log1p(jnp.exp(-jnp.abs(a))))
    pos = row % CHUNK_A
    pre, suf = g, g
    step = 1
    while step < CHUNK_A:
        pre = pre + jnp.where(pos >= step, _shift_rows(pre, -step), 0.0)
        suf = suf + jnp.where(pos < CHUNK_A - step, _shift_rows(suf, step), 0.0)
        step *= 2
    gc_scr[...] = jnp.where(lane < H_A, pre, suf)
    bt_scr[...] = jax.nn.sigmoid(gt)

    def chunk_rows(n, d):
        cn = n if d == 0 else n_chunks - 1 - n
        return pl.multiple_of(cn * CHUNK_A, CHUNK_A)

    def head_cols(x, first, rows):
        return jnp.stack([jnp.broadcast_to(x[:, first + h:first + h + 1], (rows, LANES)) for h in range(H_A)], axis=0)

    groups = [(j, d) for j in range(GDN_CHUNKS_PER_ITER) for d in range(2)]
    nb = len(groups) * H_A
    ii = lax.broadcasted_iota(jnp.int32, (nb, CHUNK_A, CHUNK_A), 1)
    jj = lax.broadcasted_iota(jnp.int32, (nb, CHUNK_A, CHUNK_A), 2)
    bb = lax.broadcasted_iota(jnp.int32, (nb, CHUNK_A, CHUNK_A), 0)
    sgn = 1 - 2 * ((bb // H_A) % 2)

    def prepare_body(i, carry):
        parts = {name: [] for name in ("q", "k", "v", "gcb", "rowgc", "g_end", "betab")}
        slices = []
        for j, d in groups:
            sl = pl.ds(chunk_rows(i * GDN_CHUNKS_PER_ITER + j, d), CHUNK_A)
            slices.append(sl)
            gch = gc_scr[sl, :]
            g_row = gch[CHUNK_A - 1:CHUNK_A, :] if d == 0 else gch[0:1, :]
            parts["q"].append(qs[:, sl, :])
            parts["k"].append(ks[:, sl, :])
            parts["v"].append(vs[:, sl, :])
            parts["gcb"].append(head_cols(gch, d * H_A, CHUNK_A))
            parts["rowgc"].append(jnp.broadcast_to(gch.T[d * H_A:(d + 1) * H_A][:, None, :],
                                                   (H_A, CHUNK_A, CHUNK_A)))
            parts["g_end"].append(head_cols(g_row, d * H_A, 1))
            parts["betab"].append(head_cols(bt_scr[sl, :], 2 * H_A + d * H_A, CHUNK_A))
        args = [jnp.concatenate(parts[name], axis=0) for name in ("q", "k", "v", "gcb", "rowgc", "g_end", "betab")]
        outs = _gdn_prepare(*args, sgn, ii, jj)
        for gi, ((j, d), sl) in enumerate(zip(groups, slices)):
            for scr, val in zip((u_scr, w_scr, a_scr, qd_scr, kd_scr), outs):
                scr[d * H_A:(d + 1) * H_A, sl, :] = val[gi * H_A:(gi + 1) * H_A].astype(BF16)
        return carry

    lax.fori_loop(0, n_chunks // GDN_CHUNKS_PER_ITER, prepare_body, 0)

    for d in range(2):
        for h in range(H_A):
            s_scr[d * H_A + h] = s0_ref[0, 0, d, h]

    def scan_body(n, carry):
        slices = [pl.ds(chunk_rows(n, d), CHUNK_A) for d in range(2)]
        both = lambda scr: jnp.concatenate([scr[d * H_A:(d + 1) * H_A, slices[d], :] for d in range(2)], axis=0)
        g_end = jnp.concatenate(
            [head_cols(gc_scr[pl.ds(chunk_rows(n, d) + (CHUNK_A - 1 if d == 0 else 0), 1), :], d * H_A, 1)
             for d in range(2)], axis=0)
        s = s_scr[...]
        v_new = both(u_scr).astype(F32) - _bdot(both(w_scr), s)
        o = _bdot(both(qd_scr), s) + _bdot(both(a_scr), v_new)
        s_scr[...] = s * jnp.exp(g_end) + _bdot_tl(both(kd_scr), v_new)
        for d in range(2):
            o_scr[d * H_A:(d + 1) * H_A, slices[d], :] = o[d * H_A:(d + 1) * H_A]
        return carry

    lax.fori_loop(0, n_chunks, scan_body, 0)

    for h in range(H_A):
        sl = slice(h * LANES, (h + 1) * LANES)
        sfin_ref[0, 0, h] = s_scr[h]
        sfin_ref[0, 1, h] = s_scr[H_A + h]
        o = o_scr[h] + o_scr[H_A + h]
        o_ref[:, sl] = (_rms(o, ng_ref[...]) * _silu(z_ref[:, sl].astype(F32))).astype(o_ref.dtype)


def _gdn(y, gates, conv_w, alog_lanes, dtb_lanes, norm_g, s0, layer, bsz, t):
    blk = lambda cb: pl.BlockSpec((t, BRANCH_W), lambda b: (b, cb // H_A))
    vec = lambda: pl.BlockSpec((1, LANES), lambda b: (0, 0))
    nhd = 2 * H_A
    return pl.pallas_call(
        _gdn_kernel,
        grid=(bsz,),
        in_specs=[blk(CB_AQ), blk(CB_AK), blk(CB_AV), blk(CB_AZ),
                  pl.BlockSpec((t, LANES), lambda b: (b, 0)),
                  pl.BlockSpec((8, 3 * BRANCH_W), lambda b: (0, 0)),
                  vec(), vec(), vec(),
                  pl.BlockSpec((1, 1, 2, H_A, DK_A, DV_A), lambda b: (b, layer, 0, 0, 0, 0))],
        out_specs=[pl.BlockSpec((t, BRANCH_W), lambda b: (b, 0)),
                   pl.BlockSpec((1, 2, H_A, DK_A, DV_A), lambda b: (b, 0, 0, 0, 0))],
        out_shape=[jax.ShapeDtypeStruct((bsz * t, BRANCH_W), BF16),
                   jax.ShapeDtypeStruct((bsz, 2, H_A, DK_A, DV_A), F32)],
        scratch_shapes=[pltpu.VMEM((t + 2 * CONV_PAD, BRANCH_W), F32),
                        pltpu.VMEM((H_A, t, LANES), F32), pltpu.VMEM((H_A, t, LANES), F32),
                        pltpu.VMEM((H_A, t, LANES), F32),
                        pltpu.VMEM((t, LANES), F32), pltpu.VMEM((t, LANES), F32),
                        pltpu.VMEM((nhd, t, DV_A), BF16), pltpu.VMEM((nhd, t, DK_A), BF16),
                        pltpu.VMEM((nhd, t, DK_A), BF16), pltpu.VMEM((nhd, t, DK_A), BF16),
                        pltpu.VMEM((nhd, t, CHUNK_A), BF16),
                        pltpu.VMEM((nhd, t, DV_A), F32), pltpu.VMEM((nhd, DK_A, DV_A), F32)],
        compiler_params=_cparams("parallel"),
        name="gdn",
    )(y, y, y, y, gates, conv_w, alog_lanes, dtb_lanes, norm_g, s0)


def _boundary_rows(bc, w, fwd):
    c = bc.shape[0]
    if w >= 4:
        pieces = []
        for start in range(0, c, 2 * w):
            r = start + w - 1 if fwd else start + w
            pieces.append(jnp.broadcast_to(bc[r:r + 1, :], (2 * w, LANES)))
        return pieces[0] if len(pieces) == 1 else jnp.concatenate(pieces, axis=0)
    pos = lax.broadcasted_iota(jnp.int32, (c, LANES), 0) % (2 * w)
    target = w - 1 if fwd else w
    out = bc
    for p in range(2 * w):
        if p != target:
            out = jnp.where(pos == p, _shift_rows(bc, target - p), out)
    return out


def _hgrn_levels():
    w = HGRN_CHUNK // 2
    while w >= 1:
        yield w
        w //= 2


def _hgrn_constants():
    idx = np.arange(HGRN_CHUNK)
    t, s = idx[:, None], idx[None, :]
    tri = np.stack([t >= s, t <= s]).astype(np.float32)
    pair = []
    for w in _hgrn_levels():
        same = (t // (2 * w)) == (s // (2 * w))
        upper_t, upper_s = (t % (2 * w)) >= w, (s % (2 * w)) >= w
        pair.append(np.stack([same & upper_t & ~upper_s, same & ~upper_t & upper_s]))
    return jnp.asarray(tri, BF16), jnp.asarray(np.stack(pair, axis=1).astype(np.float32))


def _hgrn_tiles(q, k, v, logf, st, tri, pair_ref):
    c = q.shape[1]
    bc = None
    rem = logf
    for _ in range(3):
        part = rem.astype(BF16)
        rem = rem - part.astype(F32)
        r = jnp.einsum('bij,bjk->bik', tri, part, preferred_element_type=F32)
        bc = r if bc is None else bc + r
    nb = q.shape[0]
    q_lo, k_lo = q.astype(BF16), k.astype(BF16)
    amat = None
    for lvl, w in enumerate(_hgrn_levels()):
        bm = jnp.stack([_boundary_rows(bc[i], w, i % 2 == 0) for i in range(nb)], axis=0)
        e = jnp.exp2(jnp.abs(bc - bm) * (-LOG2E)).astype(BF16)
        pair = jnp.concatenate([pair_ref[:, lvl]] * (nb // 2), axis=0)
        g = _bdot_t(q_lo * e, k_lo * e) * pair
        amat = g if amat is None else amat + g
    o = _bdot(amat, v) + jnp.sum(q * k, axis=-1, keepdims=True) * v
    tot = jnp.stack([bc[i, c - 1:c, :] if i % 2 == 0 else bc[i, 0:1, :] for i in range(nb)], axis=0)
    inc = _bdot_tl(v, k * jnp.exp(tot - bc))
    decay = jnp.exp(tot)
    states = [st]
    for j in range(nb // 2):
        states.append(states[-1] * decay[2 * j:2 * j + 2] + inc[2 * j:2 * j + 2])
    o = o + _bdot_t(q * jnp.exp(bc), jnp.concatenate(states[:-1], axis=0))
    return o, states[-1]


def _hgrn_kernel(q_ref, ff_ref, fr_ref, i_ref, g_ref, lbf_ref, lbr_ref, ng_ref, s0_ref, tri_ref, pair_ref,
                 o_ref, sfin_ref, of_scr, or_scr):
    t = q_ref.shape[0]
    n_tiles = t // HGRN_CHUNK
    unroll = min(HGRN_UNROLL, n_tiles)
    tri = jnp.concatenate([tri_ref[...]] * unroll, axis=0)

    def load(sl, f_ref, lb_ref):
        lb = lb_ref[...]
        f = jnp.maximum(lb + (1.0 - lb) * jax.nn.sigmoid(f_ref[sl, :].astype(F32)), F_FLOOR)
        return _silu(q_ref[sl, :].astype(F32)), 1.0 - f, i_ref[sl, :].astype(F32), jnp.log(f)

    def body(i, st):
        slices, loaded = [], []
        for j in range(unroll):
            n = i * unroll + j
            slf = pl.ds(pl.multiple_of(n * HGRN_CHUNK, HGRN_CHUNK), HGRN_CHUNK)
            slr = pl.ds(pl.multiple_of((n_tiles - 1 - n) * HGRN_CHUNK, HGRN_CHUNK), HGRN_CHUNK)
            slices += [slf, slr]
            loaded += [load(slf, ff_ref, lbf_ref), load(slr, fr_ref, lbr_ref)]
        o, st = _hgrn_tiles(*[jnp.stack(parts, axis=0) for parts in zip(*loaded)], st, tri, pair_ref)
        for idx, sl in enumerate(slices):
            (of_scr if idx % 2 == 0 else or_scr)[sl, :] = o[idx]
        return st

    st = lax.fori_loop(0, n_tiles // unroll, body,
                       jnp.stack([s0_ref[0, 0, 0, 0].T, s0_ref[0, 0, 1, 0].T], axis=0))
    sfin_ref[0, 0, 0] = st[0].T
    sfin_ref[0, 1, 0] = st[1].T
    o = of_scr[...] + or_scr[...]
    o_ref[...] = (_rms(o, ng_ref[...]) * _silu(g_ref[...].astype(F32))).astype(o_ref.dtype)


def _hgrn(y, lb, norm_g, s0, layer, bsz, t):
    col = lambda cb: pl.BlockSpec((t, LANES), lambda b, h: (b, cb + h))
    tri, pair = _hgrn_constants()
    return pl.pallas_call(
        _hgrn_kernel,
        grid=(bsz, H_C),
        in_specs=[col(CB_CQ), col(CB_CFF), col(CB_CFB), col(CB_CI), col(CB_CG),
                  pl.BlockSpec((1, LANES), lambda b, h: (0, h)),
                  pl.BlockSpec((1, LANES), lambda b, h: (0, H_C + h)),
                  pl.BlockSpec((1, LANES), lambda b, h: (0, 0)),
                  pl.BlockSpec((1, 1, 2, 1, DK_C, DV_C), lambda b, h: (b, layer, 0, h, 0, 0)),
                  pl.BlockSpec(tri.shape, lambda b, h: (0, 0, 0)),
                  pl.BlockSpec(pair.shape, lambda b, h: (0, 0, 0, 0))],
        out_specs=[pl.BlockSpec((t, LANES), lambda b, h: (b, h)),
                   pl.BlockSpec((1, 2, 1, DK_C, DV_C), lambda b, h: (b, 0, h, 0, 0))],
        out_shape=[jax.ShapeDtypeStruct((bsz * t, BRANCH_W), BF16),
                   jax.ShapeDtypeStruct((bsz, 2, H_C, DK_C, DV_C), F32)],
        scratch_shapes=[pltpu.VMEM((t, LANES), F32), pltpu.VMEM((t, LANES), F32)],
        compiler_params=_cparams("parallel", "parallel"),
        name="hgrn",
    )(y, y, y, y, y, lb, lb, norm_g, s0, tri, pair)


def _diff_lambda(lam_ref, lam_init):
    lp = lam_ref[...]
    return (jnp.exp(jnp.sum(lp[0:1] * lp[1:2], axis=-1, keepdims=True))
            - jnp.exp(jnp.sum(lp[2:3] * lp[3:4], axis=-1, keepdims=True)) + lam_init)


def _diff_core(problems, lam, ng, lam_init):
    scores = []
    for q, k, _ in problems:
        lane = lax.broadcasted_iota(jnp.int32, q.shape, 1)
        zero = jnp.zeros_like(q)
        scores.append((_dot_t(jnp.where(lane < DK_B, q, zero), k), _dot_t(jnp.where(lane >= DK_B, q, zero), k)))
    outs = []
    for (s1, s2), (_, _, v) in zip(scores, problems):
        e1, r1 = _softmax_parts(s1)
        e2, r2 = _softmax_parts(s2)
        outs.append(_dot(e1, v) * r1 - _dot(e2, v) * (lam * r2))
    return [_rms(o, ng) * (1.0 - lam_init) for o in outs]


def _diff_ctx_kernel(lam_ref, q_ref, k_ref, v_ref, ng_ref, o_ref, *, lam_init):
    lam = _diff_lambda(lam_ref, lam_init)
    heads = [slice(h * LANES, (h + 1) * LANES) for h in range(H_B)]
    outs = _diff_core([((q_ref[:, sl].astype(F32) * DIFF_QSCALE).astype(BF16), k_ref[:, sl], v_ref[:, sl])
                       for sl in heads], lam, ng_ref[...], lam_init)
    for sl, o in zip(heads, outs):
        o_ref[:, sl] = o.astype(o_ref.dtype)


def _diff_ctx(y, lam_p, norm_g, lam_init, bsz, t):
    blk = lambda cb: pl.BlockSpec((t, BRANCH_W), lambda b: (b, cb // H_B))
    return pl.pallas_call(
        functools.partial(_diff_ctx_kernel, lam_init=lam_init),
        grid=(bsz,),
        in_specs=[pl.BlockSpec((4, DK_B), lambda b: (0, 0)), blk(CB_BQ), blk(CB_BK), blk(CB_BV),
                  pl.BlockSpec((1, LANES), lambda b: (0, 0))],
        out_specs=pl.BlockSpec((t, BRANCH_W), lambda b: (b, 0)),
        out_shape=jax.ShapeDtypeStruct((bsz * t, BRANCH_W), BF16),
        compiler_params=_cparams("parallel"),
        name="diff_ctx",
    )(lam_p, y, y, y, norm_g)


def _rope(x, cos, sin_signed):
    lane = lax.broadcasted_iota(jnp.int32, x.shape, 1)
    rot = jnp.where((lane % DK_B) < DK_B // 2, pltpu.roll(x, LANES - DK_B // 2, 1), pltpu.roll(x, DK_B // 2, 1))
    return x * cos + rot * sin_signed


def _diff_lat_kernel(lam_ref, q_ref, k_ref, v_ref, ck_ref, cv_ref, cosq_ref, sinq_ref, cos_ref, sin_ref, ng_ref,
                     o_ref, k_scr, v_scr, *, lam_init):
    past = ck_ref.shape[3]

    @pl.when(pl.program_id(2) == 0)
    def _():
        k_scr[:past, :] = ck_ref[0, 0, 0].astype(BF16)
        v_scr[:past, :] = cv_ref[0, 0, 0].astype(BF16)
        k_scr[past:, :] = _rope(k_ref[...].astype(F32), cos_ref[...], sin_ref[...]).astype(BF16)
        v_scr[past:, :] = v_ref[...]

    lam = _diff_lambda(lam_ref, lam_init)
    q = (_rope(q_ref[...].astype(F32), cosq_ref[...], sinq_ref[...]) * DIFF_QSCALE).astype(BF16)
    k, v = k_scr[...], v_scr[...]
    half = q.shape[0] // 2
    outs = _diff_core([(q[:half], k, v), (q[half:], k, v)], lam, ng_ref[...], lam_init)
    o_ref[:half, :] = outs[0].astype(o_ref.dtype)
    o_ref[half:, :] = outs[1].astype(o_ref.dtype)


def _diff_lat(y, ck, cv, cos, sin_signed, lam_p, norm_g, lam_init, layer, bsz, t):
    tq = 512
    nq = t // tq
    past = ck.shape[3]
    full = lambda cb: pl.BlockSpec((t, LANES), lambda b, h, i: (b, cb + h))
    cache = lambda: pl.BlockSpec((1, 1, 1, past, LANES), lambda b, h, i: (b, layer, h, 0, 0))
    tab_q = lambda: pl.BlockSpec((tq, LANES), lambda b, h, i: (i, 0))
    tab = lambda: pl.BlockSpec((t, LANES), lambda b, h, i: (0, 0))
    return pl.pallas_call(
        functools.partial(_diff_lat_kernel, lam_init=lam_init),
        grid=(bsz, H_B, nq),
        in_specs=[pl.BlockSpec((4, DK_B), lambda b, h, i: (0, 0)),
                  pl.BlockSpec((tq, LANES), lambda b, h, i: (b * nq + i, CB_BQ + h)),
                  full(CB_BK), full(CB_BV), cache(), cache(), tab_q(), tab_q(), tab(), tab(),
                  pl.BlockSpec((1, LANES), lambda b, h, i: (0, 0))],
        out_specs=pl.BlockSpec((tq, LANES), lambda b, h, i: (b * nq + i, h)),
        out_shape=jax.ShapeDtypeStruct((bsz * t, BRANCH_W), BF16),
        scratch_shapes=[pltpu.VMEM((past + t, LANES), BF16), pltpu.VMEM((past + t, LANES), BF16)],
        compiler_params=_cparams("parallel", "parallel", "arbitrary"),
        name="diff_lat",
    )(lam_p, y, y, y, ck, cv, cos, sin_signed, cos, sin_signed, norm_g)


def _pair_attend(q, keys, vals, bias=None):
    lane = lax.broadcasted_iota(jnp.int32, q.shape, 1)
    zero = jnp.zeros_like(q)
    scores = [[_dot_t(jnp.where((lane < DH_D) == (par == 0), q, zero), kk) for kk in keys] for par in range(2)]
    outs = []
    for par in range(2):
        s = scores[par]
        if bias is not None:
            s[-1] = jnp.where(bias[par] > 0.5 * MASK_VALUE, s[-1] + bias[par], MASK_VALUE)
        m = s[0].max(axis=-1, keepdims=True)
        for x in s[1:]:
            m = jnp.maximum(m, x.max(axis=-1, keepdims=True))
        e = [jnp.exp2(x - m) for x in s]
        den = e[0].sum(axis=-1, keepdims=True)
        for x in e[1:]:
            den = den + x.sum(axis=-1, keepdims=True)
        o = _dot(e[0], vals[0])
        for x, vv in zip(e[1:], vals[1:]):
            o = o + _dot(x, vv)
        outs.append(o * (1.0 / den))
    return jnp.where(lane < DH_D, outs[0], outs[1])


def _na_ctx_kernel(q_ref, k_ref, v_ref, o_ref):
    for p in range(H_D // 2):
        sl = slice(p * LANES, (p + 1) * LANES)
        q = (q_ref[:, sl].astype(F32) * NA_QSCALE).astype(BF16)
        o_ref[:, sl] = _pair_attend(q, [k_ref[:, sl]], [v_ref[:, sl]]).astype(o_ref.dtype)


def _na_ctx(y, bsz, t):
    blk = lambda cb: pl.BlockSpec((t, BRANCH_W), lambda b: (b, cb // 4))
    return pl.pallas_call(
        _na_ctx_kernel,
        grid=(bsz,),
        in_specs=[blk(CB_DQ), blk(CB_DK), blk(CB_DV)],
        out_specs=pl.BlockSpec((t, BRANCH_W), lambda b: (b, 0)),
        out_shape=jax.ShapeDtypeStruct((bsz * t, BRANCH_W), BF16),
        compiler_params=_cparams("parallel"),
        name="na_ctx",
    )(y, y, y)


def _na_bias_kernel(rpb_ref, o_ref):
    hd = pl.program_id(0)
    qc = lax.broadcasted_iota(jnp.int32, (GRID_W, LANES), 0)
    lane = lax.broadcasted_iota(jnp.int32, (GRID_W, LANES), 1)
    kc = lane % GRID_W
    dc = jnp.clip(kc - qc + WIN_C - 1, 0, 2 * WIN_C - 2)
    c0 = jnp.clip(qc - WIN_C // 2, 0, GRID_W - WIN_C)
    col_ok = (kc >= c0) & (kc < c0 + WIN_C)
    n_dr = 2 * WIN_R - 1
    tables = []
    for dr in range(n_dr):
        acc = jnp.zeros((GRID_W, LANES), F32)
        for d in range(2 * WIN_C - 1):
            acc = jnp.where(dc == d, rpb_ref[hd, dr * (2 * WIN_C - 1) + d], acc)
        tables.append(jnp.where(col_ok, acc * LOG2E, MASK_VALUE))
    masked = jnp.full((GRID_W, LANES), MASK_VALUE, F32)
    for i in range(n_dr + 1):
        lo = tables[i - 1] if i >= 1 else masked
        hi = tables[i] if i < n_dr else masked
        o_ref[0, i] = jnp.where(lane < GRID_W, lo, hi)


def _na_bias(rpb):
    n_dr = 2 * WIN_R - 1
    return pl.pallas_call(
        _na_bias_kernel,
        grid=(H_D,),
        in_specs=[pl.BlockSpec(memory_space=pltpu.SMEM)],
        out_specs=pl.BlockSpec((1, n_dr + 1, GRID_W, LANES), lambda h: (h, 0, 0, 0)),
        out_shape=jax.ShapeDtypeStruct((H_D, n_dr + 1, GRID_W, LANES), F32),
        compiler_params=_cparams("parallel"),
        name="na_bias",
    )(rpb.reshape(H_D, n_dr * (2 * WIN_C - 1)))


def _na_lat_kernel(q_ref, k_ref, v_ref, ck_ref, cv_ref, tab_ref, o_ref, *, rows):
    wr = min(WIN_R, rows)
    lane = lax.broadcasted_iota(jnp.int32, (GRID_W, LANES), 1)
    masked = jnp.full((GRID_W, LANES), MASK_VALUE, F32)
    ck, cv = ck_ref[0, 0, 0], cv_ref[0, 0, 0]
    first_row = lambda r: min(max(r - wr // 2, 0), rows - wr)
    for g in range(rows // NA_QROWS):
        starts = [first_row(g * NA_QROWS + qi) for qi in range(NA_QROWS)]
        kp0, kp1 = min(starts) // 2, (max(starts) + wr + 1) // 2
        k_lat = k_ref[2 * kp0 * GRID_W:2 * kp1 * GRID_W, :]
        v_lat = v_ref[2 * kp0 * GRID_W:2 * kp1 * GRID_W, :]
        bias = []
        for par in range(2):
            blocks = []
            for qi in range(NA_QROWS):
                r = g * NA_QROWS + qi
                r0 = first_row(r)
                tiles = []
                for kp in range(kp0, kp1):
                    ok0 = r0 <= 2 * kp < r0 + wr
                    ok1 = r0 <= 2 * kp + 1 < r0 + wr
                    if not (ok0 or ok1):
                        tiles.append(masked)
                        continue
                    tile = tab_ref[par, 2 * kp - r + WIN_R]
                    if not ok0:
                        tile = jnp.where(lane >= GRID_W, tile, MASK_VALUE)
                    if not ok1:
                        tile = jnp.where(lane < GRID_W, tile, MASK_VALUE)
                    tiles.append(tile)
                blocks.append(jnp.concatenate(tiles, axis=1))
            bias.append(jnp.concatenate(blocks, axis=0))
        sl = slice(g * NA_QROWS * GRID_W, (g + 1) * NA_QROWS * GRID_W)
        q = (q_ref[sl, :].astype(F32) * NA_QSCALE).astype(BF16)
        o_ref[sl, :] = _pair_attend(q, [ck, k_lat], [cv, v_lat], bias).astype(o_ref.dtype)


def _na_lat(y, ck, cv, tables, layer, bsz, t):
    rows = t // GRID_W
    past = ck.shape[3]
    n_tab = tables.shape[1]
    col = lambda cb: pl.BlockSpec((t, LANES), lambda b, p: (b, cb + p))
    cache = lambda: pl.BlockSpec((1, 1, 1, past, LANES), lambda b, p: (b, layer, p, 0, 0))
    return pl.pallas_call(
        functools.partial(_na_lat_kernel, rows=rows),
        grid=(bsz, H_D // 2),
        in_specs=[col(CB_DQ), col(CB_DK), col(CB_DV), cache(), cache(),
                  pl.BlockSpec((2, n_tab, GRID_W, LANES), lambda b, p: (p, 0, 0, 0))],
        out_specs=pl.BlockSpec((t, LANES), lambda b, p: (b, p)),
        out_shape=jax.ShapeDtypeStruct((bsz * t, BRANCH_W), BF16),
        compiler_params=_cparams("parallel", "parallel"),
        name="na_lat",
    )(y, y, y, ck, cv, tables)


def _axial_rope_tables(n_tok):
    tok = jnp.arange(n_tok)
    n_freq = DK_B // 4
    inv = ROPE_BASE ** (-jnp.arange(n_freq, dtype=F32) / n_freq)
    ang = jnp.concatenate([(tok // GRID_W).astype(F32)[:, None] * inv,
                           (tok % GRID_W).astype(F32)[:, None] * inv], axis=-1)
    cos, sin = jnp.cos(ang), jnp.sin(ang)
    return jnp.tile(cos, (1, 4)), jnp.tile(jnp.concatenate([-sin, sin], axis=-1), (1, 2))


def _permute_w_in(w):
    a_end = 4 * BRANCH_W
    g_end = a_end + 4 * H_A
    mg_start = w.shape[-1] - N_BRANCH * D_MODEL
    main = jnp.concatenate([w[..., mg_start:], w[..., :a_end], w[..., g_end:mg_start]], axis=-1)
    gate = jnp.pad(w[..., a_end:g_end], ((0, 0), (0, 0), (0, GATE_COLS - 4 * H_A)))
    return main.astype(BF16), gate.astype(BF16)


def _ctx_cache_kernel(*refs):
    ins, (bk_out, bv_out, dk_out, dv_out) = refs[:-4], refs[-4:]
    for layer in range(DEPTH):
        bk, bv, dk, dv = ins[4 * layer:4 * layer + 4]

        @pl.when(pl.program_id(1) == layer)
        def _():
            for h in range(H_B):
                sl = slice(h * LANES, (h + 1) * LANES)
                bk_out[0, 0, h] = bk[:, sl].astype(F32)
                bv_out[0, 0, h] = bv[:, sl].astype(F32)
            for h in range(H_D):
                sl = slice(h * DH_D, (h + 1) * DH_D)
                dk_out[0, 0, h] = dk[:, sl].astype(F32)
                dv_out[0, 0, h] = dv[:, sl].astype(F32)


def _ctx_caches(ys, bsz, t):
    blk = lambda cb: pl.BlockSpec((t, BRANCH_W), lambda b, l: (b, cb // 4))
    out = lambda heads, width: pl.BlockSpec((1, 1, heads, t, width), lambda b, l: (b, l, 0, 0, 0))
    shape = lambda heads, width: jax.ShapeDtypeStruct((bsz, DEPTH, heads, t, width), F32)
    return pl.pallas_call(
        _ctx_cache_kernel,
        grid=(bsz, DEPTH),
        in_specs=[blk(cb) for _ in ys for cb in (CB_BK, CB_BV, CB_DK, CB_DV)],
        out_specs=[out(H_B, 2 * DK_B), out(H_B, DV_B), out(H_D, DH_D), out(H_D, DH_D)],
        out_shape=[shape(H_B, 2 * DK_B), shape(H_B, DV_B), shape(H_D, DH_D), shape(H_D, DH_D)],
        compiler_params=_cparams("parallel", "parallel"),
        name="ctx_caches",
    )(*[y for y in ys for _ in range(4)])


def kernel(x_prompt, x_sample, c, state_gdn, cache_diff_k, cache_diff_v, state_hgrn, cache_na_k, cache_na_v, c_ctx,
           w_ada, b_ada, norm1_g, w_in, gdn_conv_w, gdn_A_log, gdn_dt_bias, gdn_norm_g, diff_lambda, diff_norm_g,
           hgrn_lb_logits, hgrn_norm_g, na_rpb, w_branch, w_out, norm2_g, w_ffn_gate, w_ffn_up, w_ffn_down,
           final_norm_g):
    bp, tp, _ = x_prompt.shape
    bs, ts, _ = x_sample.shape

    cond = jnp.zeros((16, D_MODEL), F32).at[:bs].set(c).at[bs].set(c_ctx)
    mods = _adaln(cond, w_ada, b_ada)
    probs = jax.nn.softmax(hgrn_lb_logits.astype(F32), axis=0)
    lb_all = jnp.cumsum(probs, axis=0) - probs[0:1]
    cos, sin_signed = _axial_rope_tables(ts)
    past = cache_na_k.shape[3]
    pair = lambda a: a.reshape(bs, DEPTH, H_D // 2, 2, past, DH_D).transpose(0, 1, 2, 4, 3, 5).reshape(
        bs, DEPTH, H_D // 2, past, LANES).astype(BF16)
    na_ck, na_cv = pair(cache_na_k), pair(cache_na_v)
    zeros_gdn = jnp.zeros((bp, 1, 2, H_A, DK_A, DV_A), F32)
    zeros_hgrn = jnp.zeros((bp, 1, 2, H_C, DK_C, DV_C), F32)
    w_main, w_gate = _permute_w_in(w_in)
    wbr, wo = w_branch.astype(BF16), w_out.astype(BF16)
    wfg, wfu, wfd = w_ffn_gate.astype(BF16), w_ffn_up.astype(BF16), w_ffn_down.astype(BF16)

    xp = x_prompt.reshape(bp * tp, D_MODEL)
    xs = x_sample.reshape(bs * ts, D_MODEL)
    new_ctx = []
    for l in range(DEPTH):
        lam_init = 0.8 - 0.6 * math.exp(-0.3 * l)
        conv_w = jnp.zeros((8, 3 * BRANCH_W), F32).at[:CONV_W].set(gdn_conv_w[l].T)
        lanes = lambda p: jnp.pad(p.astype(F32).reshape(1, 2 * H_A), ((0, 0), (0, LANES - 2 * H_A)))
        alog, dtb = lanes(gdn_A_log[l]), lanes(gdn_dt_bias[l])
        lb = lb_all[l].reshape(1, 2 * H_C * DK_C)
        vecs = dict(n1=norm1_g[l].reshape(1, D_MODEL), n2=norm2_g[l].reshape(1, D_MODEL),
                    fin=final_norm_g.reshape(1, D_MODEL), gdn=gdn_norm_g[l].reshape(1, LANES),
                    diff=diff_norm_g[l].reshape(1, LANES), hgrn=hgrn_norm_g[l].reshape(1, LANES))
        tables = _na_bias(na_rpb[l])
        final = l == DEPTH - 1

        def dense_tail(x, mod, y, branches, tok_per_row):
            x = _merge(x, mod, y, branches, wbr, wo, l, tok_per_row)
            return _ffn(x, mod, vecs["n2"], vecs["fin"], wfg, wfu, wfd, l, tok_per_row, final)

        mod = mods[l, bs:bs + 1].reshape(1, 1, 6 * D_MODEL)
        y, gates = _inproj(xp, mod, vecs["n1"], w_main, w_gate, l, bp * tp)
        o_a, st_gdn = _gdn(y, gates, conv_w, alog, dtb, vecs["gdn"], zeros_gdn, 0, bp, tp)
        o_b = _diff_ctx(y, diff_lambda[l], vecs["diff"], lam_init, bp, tp)
        o_c, st_hgrn = _hgrn(y, lb, vecs["hgrn"], zeros_hgrn, 0, bp, tp)
        o_d = _na_ctx(y, bp, tp)
        xp = dense_tail(xp, mod, y, (o_a, o_b, o_c, o_d), bp * tp)
        new_ctx.append((st_gdn, st_hgrn, y))

        mod = mods[l, :bs].reshape(bs, 1, 6 * D_MODEL)
        y, gates = _inproj(xs, mod, vecs["n1"], w_main, w_gate, l, ts)
        o_a, _ = _gdn(y, gates, conv_w, alog, dtb, vecs["gdn"], state_gdn, l, bs, ts)
        o_b = _diff_lat(y, cache_diff_k, cache_diff_v, cos, sin_signed, diff_lambda[l], vecs["diff"], lam_init,
                        l, bs, ts)
        o_c, _ = _hgrn(y, lb, vecs["hgrn"], state_hgrn, l, bs, ts)
        o_d = _na_lat(y, na_ck, na_cv, tables, l, bs, ts)
        xs = dense_tail(xs, mod, y, (o_a, o_b, o_c, o_d), ts)

    stack = lambda i: jnp.stack([n[i] for n in new_ctx], axis=1)
    diff_k, diff_v, na_k, na_v = _ctx_caches([n[2] for n in new_ctx], bp, tp)
    return (xp.reshape(bp, tp, D_MODEL), xs.reshape(bs, ts, D_MODEL), stack(0), diff_k, diff_v, stack(1),
            na_k, na_v)
```

```python
import functools
import math

import jax
import jax.numpy as jnp
import numpy as np
from jax import lax
from jax.experimental import pallas as pl
from jax.experimental.pallas import tpu as pltpu

F32 = jnp.float32
BF16 = jnp.bfloat16

D_MODEL = 1024
DEPTH = 2
GRID_W = 64
N_BRANCH = 4
H_A, DK_A, DV_A, CONV_W, CHUNK_A = 4, 128, 128, 5, 64
H_B, DK_B, DV_B = 4, 64, 128
H_C, DK_C, DV_C = 4, 128, 128
H_D, DH_D, WIN_R, WIN_C = 8, 64, 8, 16
BRANCH_W = 512
ROPE_BASE = 10000.0
D_FF = 2816
EPS = 1e-6
MASK_VALUE = -1e30
F_FLOOR = 1e-30
LOG2E = math.log2(math.e)
DIFF_QSCALE = DK_B ** -0.5 * LOG2E
NA_QSCALE = DH_D ** -0.5 * LOG2E

LANES = 128
VMEM_LIMIT_BYTES = 56 * 1024 * 1024

Y_COLS = 11776
CB_MG = 0
CB_AQ, CB_AK, CB_AV, CB_AZ = 32, 36, 40, 44
CB_BQ, CB_BK, CB_BV = 48, 52, 56
CB_CQ, CB_CFF, CB_CFB, CB_CI, CB_CG = 60, 64, 68, 72, 76
CB_DQ, CB_DK, CB_DV = 80, 84, 88
GATE_COLS = LANES

GDN_CHUNKS_PER_ITER = 4
CONV_PAD = 8
CONV_ROWS = 128
HGRN_CHUNK = 128
HGRN_UNROLL = 4
NA_QROWS = 4
DIFF_QROWS = 256


def _cparams(*sem):
    return pltpu.CompilerParams(dimension_semantics=sem, vmem_limit_bytes=VMEM_LIMIT_BYTES)


def _dot(a, b):
    return jnp.dot(a.astype(BF16), b.astype(BF16), preferred_element_type=F32)


def _dot_t(a, b):
    return lax.dot_general(a.astype(BF16), b.astype(BF16), (((1,), (1,)), ((), ())), preferred_element_type=F32)


def _silu(x):
    return x * jax.nn.sigmoid(x)


def _rms(x, g):
    return x * lax.rsqrt(jnp.mean(x * x, axis=-1, keepdims=True) + EPS) * g


def _softmax_parts(s):
    m = jnp.max(s, axis=-1, keepdims=True)
    e = jnp.exp2(s - m)
    return e, 1.0 / jnp.sum(e, axis=-1, keepdims=True)


def _adaln_kernel(c_ref, w_ref, b_ref, o_ref):
    s = _silu(c_ref[...])
    o_ref[0] = jnp.dot(s, w_ref[0], precision=lax.Precision.HIGHEST, preferred_element_type=F32) + b_ref[0]


def _adaln(cond, w_ada, b_ada):
    rows = cond.shape[0]
    tn = 1024
    n_out = w_ada.shape[-1]
    return pl.pallas_call(
        _adaln_kernel,
        grid=(DEPTH, n_out // tn),
        in_specs=[pl.BlockSpec((rows, D_MODEL), lambda l, j: (0, 0)),
                  pl.BlockSpec((1, D_MODEL, tn), lambda l, j: (l, 0, j)),
                  pl.BlockSpec((1, 1, tn), lambda l, j: (l, 0, j))],
        out_specs=pl.BlockSpec((1, rows, tn), lambda l, j: (l, 0, j)),
        out_shape=jax.ShapeDtypeStruct((DEPTH, rows, n_out), F32),
        compiler_params=_cparams("parallel", "parallel"),
        name="adaln",
    )(cond, w_ada, b_ada.reshape(DEPTH, 1, n_out))


def _inproj_kernel(x_ref, sh_ref, sc_ref, g_ref, w_ref, wg_ref, y_ref, gate_ref, h_scr):
    @pl.when(pl.program_id(1) == 0)
    def _():
        h = _rms(x_ref[...], g_ref[...]) * (1.0 + sc_ref[0]) + sh_ref[0]
        hb = h.astype(BF16)
        h_scr[...] = hb
        gate_ref[...] = jnp.dot(hb, wg_ref[0], preferred_element_type=F32)

    y_ref[...] = jnp.dot(h_scr[...], w_ref[0], preferred_element_type=F32).astype(y_ref.dtype)


def _inproj(x, mod, norm_g, w, wg, layer, tok_per_row):
    m = x.shape[0]
    tm, tn = min(1024, m), Y_COLS // 4
    mrow = lambda c: pl.BlockSpec((1, 1, D_MODEL), lambda i, j: ((i * tm) // tok_per_row, 0, c))
    return pl.pallas_call(
        _inproj_kernel,
        grid=(m // tm, Y_COLS // tn),
        in_specs=[pl.BlockSpec((tm, D_MODEL), lambda i, j: (i, 0)),
                  mrow(0), mrow(1),
                  pl.BlockSpec((1, D_MODEL), lambda i, j: (0, 0)),
                  pl.BlockSpec((1, D_MODEL, tn), lambda i, j: (layer, 0, j)),
                  pl.BlockSpec((1, D_MODEL, GATE_COLS), lambda i, j: (layer, 0, 0))],
        out_specs=[pl.BlockSpec((tm, tn), lambda i, j: (i, j)),
                   pl.BlockSpec((tm, GATE_COLS), lambda i, j: (i, 0))],
        out_shape=[jax.ShapeDtypeStruct((m, Y_COLS), BF16),
                   jax.ShapeDtypeStruct((m, GATE_COLS), F32)],
        scratch_shapes=[pltpu.VMEM((tm, D_MODEL), BF16)],
        compiler_params=_cparams("parallel", "arbitrary"),
        name="inproj",
    )(x, mod, mod, norm_g, w, wg)


def _merge_kernel(x_ref, g1_ref, mg_ref, oa_ref, ob_ref, oc_ref, od_ref, wbr_ref, wout_ref, xo_ref):
    acc = None
    for n, o_ref in enumerate((oa_ref, ob_ref, oc_ref, od_ref)):
        proj = jnp.dot(o_ref[...], wbr_ref[0, n], preferred_element_type=F32)
        gate = jax.nn.sigmoid(mg_ref[:, n * D_MODEL:(n + 1) * D_MODEL].astype(F32))
        acc = gate * proj if acc is None else acc + gate * proj
    out = jnp.dot(acc.astype(BF16), wout_ref[0], preferred_element_type=F32)
    xo_ref[...] = x_ref[...] + g1_ref[0] * out


def _merge(x, mod, y, branches, w_branch, w_out, layer, tok_per_row):
    m = x.shape[0]
    tm = 512
    tok = lambda: pl.BlockSpec((tm, BRANCH_W), lambda i: (i, 0))
    return pl.pallas_call(
        _merge_kernel,
        grid=(m // tm,),
        in_specs=[pl.BlockSpec((tm, D_MODEL), lambda i: (i, 0)),
                  pl.BlockSpec((1, 1, D_MODEL), lambda i: ((i * tm) // tok_per_row, 0, 2)),
                  pl.BlockSpec((tm, N_BRANCH * D_MODEL), lambda i: (i, 0)),
                  tok(), tok(), tok(), tok(),
                  pl.BlockSpec((1, N_BRANCH, BRANCH_W, D_MODEL), lambda i: (layer, 0, 0, 0)),
                  pl.BlockSpec((1, D_MODEL, D_MODEL), lambda i: (layer, 0, 0))],
        out_specs=pl.BlockSpec((tm, D_MODEL), lambda i: (i, 0)),
        out_shape=jax.ShapeDtypeStruct((m, D_MODEL), F32),
        compiler_params=_cparams("parallel"),
        name="merge",
    )(x, mod, y, *branches, w_branch, w_out)


def _ffn_kernel(x_ref, sh_ref, sc_ref, g2_ref, ng_ref, fg_ref, wg_ref, wu_ref, wd_ref, xo_ref, h_scr, acc_scr,
                *, final):
    f = pl.program_id(1)

    @pl.when(f == 0)
    def _():
        h = _rms(x_ref[...], ng_ref[...]) * (1.0 + sc_ref[0]) + sh_ref[0]
        h_scr[...] = h.astype(BF16)
        acc_scr[...] = jnp.zeros_like(acc_scr)

    h = h_scr[...]
    a = jnp.dot(h, wg_ref[0], preferred_element_type=F32)
    u = jnp.dot(h, wu_ref[0], preferred_element_type=F32)
    acc_scr[...] += jnp.dot((_silu(a) * u).astype(BF16), wd_ref[0], preferred_element_type=F32)

    @pl.when(f == pl.num_programs(1) - 1)
    def _():
        xn = x_ref[...] + g2_ref[0] * acc_scr[...]
        xo_ref[...] = _rms(xn, fg_ref[...]) if final else xn


def _ffn(x, mod, norm_g, final_g, wg, wu, wd, layer, tok_per_row, final):
    m = x.shape[0]
    tm, tf = min(1024, m), D_FF // 2
    mrow = lambda c: pl.BlockSpec((1, 1, D_MODEL), lambda i, f: ((i * tm) // tok_per_row, 0, c))
    vec = lambda: pl.BlockSpec((1, D_MODEL), lambda i, f: (0, 0))
    return pl.pallas_call(
        functools.partial(_ffn_kernel, final=final),
        grid=(m // tm, D_FF // tf),
        in_specs=[pl.BlockSpec((tm, D_MODEL), lambda i, f: (i, 0)),
                  mrow(3), mrow(4), mrow(5), vec(), vec(),
                  pl.BlockSpec((1, D_MODEL, tf), lambda i, f: (layer, 0, f)),
                  pl.BlockSpec((1, D_MODEL, tf), lambda i, f: (layer, 0, f)),
                  pl.BlockSpec((1, tf, D_MODEL), lambda i, f: (layer, f, 0))],
        out_specs=pl.BlockSpec((tm, D_MODEL), lambda i, f: (i, 0)),
        out_shape=jax.ShapeDtypeStruct((m, D_MODEL), F32),
        scratch_shapes=[pltpu.VMEM((tm, D_MODEL), BF16), pltpu.VMEM((tm, D_MODEL), F32)],
        compiler_params=_cparams("parallel", "arbitrary"),
        name="ffn",
    )(x, mod, mod, mod, norm_g, final_g, wg, wu, wd)


def _shift_rows(x, off):
    n = x.shape[0]
    return x if off % n == 0 else pltpu.roll(x, (-off) % n, 0)


def _bdot(a, b):
    return jnp.einsum('bij,bjk->bik', a.astype(BF16), b.astype(BF16), preferred_element_type=F32)


def _bdot_t(a, b):
    return jnp.einsum('bik,bjk->bij', a.astype(BF16), b.astype(BF16), preferred_element_type=F32)


def _bdot_tl(a, b):
    return jnp.einsum('bki,bkj->bij', a.astype(BF16), b.astype(BF16), preferred_element_type=F32)


def _unit_tri_inverse(nmat, ii, jj):
    blk16 = (ii // 16) == (jj // 16)
    blk32 = (ii // 32) == (jj // 32)
    eye = (ii == jj).astype(F32)
    d = jnp.where(blk16, nmat, 0.0)
    x = eye - d
    p = _bdot(d, d)
    for _ in range(2):
        x = x + _bdot(x, p)
        p = _bdot(p, p)
    x = x + _bdot(x, p)
    for e in (jnp.where(blk32 & ~blk16, nmat, 0.0), jnp.where(~blk32, nmat, 0.0)):
        x = x - _bdot(x, _bdot(e, x))
    return x


def _gdn_prepare(q, k, v, gcb, rowgc, g_end, betab, sgn, ii, jj):
    c = q.shape[1]
    kb = k * betab
    gram_kk = _bdot_t(kb, k)
    gram_qk = _bdot_t(q, k)
    diff = gcb[:, :, :c] - rowgc
    order = (ii - jj) * sgn
    tri, strict = order >= 0, order > 0
    decay = jnp.where(tri, jnp.exp(jnp.where(tri, diff, 0.0)), 0.0)
    tinv = _unit_tri_inverse(jnp.where(strict, gram_kk * decay, 0.0), ii, jj)
    eg = jnp.exp(gcb)
    sol = _bdot(tinv, jnp.concatenate([v * betab, kb * eg], axis=2))
    a = jnp.where(tri, gram_qk * decay, 0.0)
    return sol[:, :, :DV_A], sol[:, :, DV_A:], a, q * eg, k * jnp.exp(g_end - gcb)


def _gdn_kernel(q_ref, k_ref, v_ref, z_ref, gate_ref, cw_ref, alog_ref, dtb_ref, ng_ref, s0_ref, o_ref, sfin_ref,
                xpad, qs, ks, vs, gc_scr, bt_scr, u_scr, w_scr, qd_scr, kd_scr, a_scr, o_scr, s_scr):
    t = q_ref.shape[0]
    n_chunks = t // CHUNK_A
    row = lax.broadcasted_iota(jnp.int32, (t, LANES), 0)
    lane = lax.broadcasted_iota(jnp.int32, (t, LANES), 1)

    xpad[0:CONV_PAD, :] = jnp.zeros((CONV_PAD, BRANCH_W), F32)
    xpad[t + CONV_PAD:t + 2 * CONV_PAD, :] = jnp.zeros((CONV_PAD, BRANCH_W), F32)

    rb = min(t, CONV_ROWS)
    for seg, (x_ref, dst) in enumerate(((q_ref, qs), (k_ref, ks), (v_ref, vs))):
        xpad[CONV_PAD:t + CONV_PAD, :] = x_ref[...].astype(F32)
        for h in range(H_A):
            lanes = slice(h * LANES, (h + 1) * LANES)
            for r in range(0, t, rb):
                acc = None
                for j in range(CONV_W):
                    r0 = CONV_PAD + r + j - CONV_W // 2
                    wj = cw_ref[j:j + 1, seg * BRANCH_W + h * LANES:seg * BRANCH_W + (h + 1) * LANES]
                    term = xpad[r0:r0 + rb, lanes] * wj
                    acc = term if acc is None else acc + term
                x = _silu(acc)
                if seg == 0:
                    x = x * lax.rsqrt(jnp.sum(x * x, axis=-1, keepdims=True) + EPS) * (DK_A ** -0.5)
                elif seg == 1:
                    x = x * lax.rsqrt(jnp.sum(x * x, axis=-1, keepdims=True) + EPS)
                dst[h, r:r + rb, :] = x

    gt = gate_ref[...]
    a = gt + dtb_ref[...]
    g = -jnp.exp(alog_ref[...]) * (jnp.maximum(a, 0.0) + jnp.log1p(jnp.exp(-jnp.abs(a))))
    pos = row % CHUNK_A
    pre, suf = g, g
    step = 1
    while step < CHUNK_A:
        pre = pre + jnp.where(pos >= step, _shift_rows(pre, -step), 0.0)
        suf = suf + jnp.where(pos < CHUNK_A - step, _shift_rows(suf, step), 0.0)
        step *= 2
    gc_scr[...] = jnp.where(lane < H_A, pre, suf)
    bt_scr[...] = jax.nn.sigmoid(gt)

    def chunk_rows(n, d):
        cn = n if d == 0 else n_chunks - 1 - n
        return pl.multiple_of(cn * CHUNK_A, CHUNK_A)

    def head_cols(x, first, rows):
        return jnp.stack([jnp.broadcast_to(x[:, first + h:first + h + 1], (rows, LANES)) for h in range(H_A)], axis=0)

    groups = [(j, d) for j in range(GDN_CHUNKS_PER_ITER) for d in range(2)]
    nb = len(groups) * H_A
    ii = lax.broadcasted_iota(jnp.int32, (nb, CHUNK_A, CHUNK_A), 1)
    jj = lax.broadcasted_iota(jnp.int32, (nb, CHUNK_A, CHUNK_A), 2)
    bb = lax.broadcasted_iota(jnp.int32, (nb, CHUNK_A, CHUNK_A), 0)
    sgn = 1 - 2 * ((bb // H_A) % 2)

    def prepare_body(i, carry):
        parts = {name: [] for name in ("q", "k", "v", "gcb", "rowgc", "g_end", "betab")}
        slices = []
        for j, d in groups:
            sl = pl.ds(chunk_rows(i * GDN_CHUNKS_PER_ITER + j, d), CHUNK_A)
            slices.append(sl)
            gch = gc_scr[sl, :]
            g_row = gch[CHUNK_A - 1:CHUNK_A, :] if d == 0 else gch[0:1, :]
            parts["q"].append(qs[:, sl, :])
            parts["k"].append(ks[:, sl, :])
            parts["v"].append(vs[:, sl, :])
            parts["gcb"].append(head_cols(gch, d * H_A, CHUNK_A))
            parts["rowgc"].append(jnp.broadcast_to(gch.T[d * H_A:(d + 1) * H_A][:, None, :],
                                                   (H_A, CHUNK_A, CHUNK_A)))
            parts["g_end"].append(head_cols(g_row, d * H_A, 1))
            parts["betab"].append(head_cols(bt_scr[sl, :], 2 * H_A + d * H_A, CHUNK_A))
        args = [jnp.concatenate(parts[name], axis=0) for name in ("q", "k", "v", "gcb", "rowgc", "g_end", "betab")]
        outs = _gdn_prepare(*args, sgn, ii, jj)
        for gi, ((j, d), sl) in enumerate(zip(groups, slices)):
            for scr, val in zip((u_scr, w_scr, a_scr, qd_scr, kd_scr), outs):
                scr[d * H_A:(d + 1) * H_A, sl, :] = val[gi * H_A:(gi + 1) * H_A].astype(BF16)
        return carry

    lax.fori_loop(0, n_chunks // GDN_CHUNKS_PER_ITER, prepare_body, 0)

    for d in range(2):
        for h in range(H_A):
            s_scr[d * H_A + h] = s0_ref[0, 0, d, h]

    def scan_body(n, carry):
        slices = [pl.ds(chunk_rows(n, d), CHUNK_A) for d in range(2)]
        both = lambda scr: jnp.concatenate([scr[d * H_A:(d + 1) * H_A, slices[d], :] for d in range(2)], axis=0)
        g_end = jnp.concatenate(
            [head_cols(gc_scr[pl.ds(chunk_rows(n, d) + (CHUNK_A - 1 if d == 0 else 0), 1), :], d * H_A, 1)
             for d in range(2)], axis=0)
        s = s_scr[...]
        v_new = both(u_scr).astype(F32) - _bdot(both(w_scr), s)
        o = _bdot(both(qd_scr), s) + _bdot(both(a_scr), v_new)
        s_scr[...] = s * jnp.exp(g_end) + _bdot_tl(both(kd_scr), v_new)
        for d in range(2):
            o_scr[d * H_A:(d + 1) * H_A, slices[d], :] = o[d * H_A:(d + 1) * H_A]
        return carry

    lax.fori_loop(0, n_chunks, scan_body, 0)

    for h in range(H_A):
        sl = slice(h * LANES, (h + 1) * LANES)
        sfin_ref[0, 0, h] = s_scr[h]
        sfin_ref[0, 1, h] = s_scr[H_A + h]
        o = o_scr[h] + o_scr[H_A + h]
        o_ref[:, sl] = (_rms(o, ng_ref[...]) * _silu(z_ref[:, sl].astype(F32))).astype(o_ref.dtype)


def _gdn(y, gates, conv_w, alog_lanes, dtb_lanes, norm_g, s0, layer, bsz, t):
    blk = lambda cb: pl.BlockSpec((t, BRANCH_W), lambda b: (b, cb // H_A))
    vec = lambda: pl.BlockSpec((1, LANES), lambda b: (0, 0))
    nhd = 2 * H_A
    return pl.pallas_call(
        _gdn_kernel,
        grid=(bsz,),
        in_specs=[blk(CB_AQ), blk(CB_AK), blk(CB_AV), blk(CB_AZ),
                  pl.BlockSpec((t, LANES), lambda b: (b, 0)),
                  pl.BlockSpec((8, 3 * BRANCH_W), lambda b: (0, 0)),
                  vec(), vec(), vec(),
                  pl.BlockSpec((1, 1, 2, H_A, DK_A, DV_A), lambda b: (b, layer, 0, 0, 0, 0))],
        out_specs=[pl.BlockSpec((t, BRANCH_W), lambda b: (b, 0)),
                   pl.BlockSpec((1, 2, H_A, DK_A, DV_A), lambda b: (b, 0, 0, 0, 0))],
        out_shape=[jax.ShapeDtypeStruct((bsz * t, BRANCH_W), BF16),
                   jax.ShapeDtypeStruct((bsz, 2, H_A, DK_A, DV_A), F32)],
        scratch_shapes=[pltpu.VMEM((t + 2 * CONV_PAD, BRANCH_W), F32),
                        pltpu.VMEM((H_A, t, LANES), F32), pltpu.VMEM((H_A, t, LANES), F32),
                        pltpu.VMEM((H_A, t, LANES), F32),
                        pltpu.VMEM((t, LANES), F32), pltpu.VMEM((t, LANES), F32),
                        pltpu.VMEM((nhd, t, DV_A), BF16), pltpu.VMEM((nhd, t, DK_A), BF16),
                        pltpu.VMEM((nhd, t, DK_A), BF16), pltpu.VMEM((nhd, t, DK_A), BF16),
                        pltpu.VMEM((nhd, t, CHUNK_A), BF16),
                        pltpu.VMEM((nhd, t, DV_A), F32), pltpu.VMEM((nhd, DK_A, DV_A), F32)],
        compiler_params=_cparams("parallel"),
        name="gdn",
    )(y, y, y, y, gates, conv_w, alog_lanes, dtb_lanes, norm_g, s0)


def _boundary_rows(bc, w, fwd):
    c = bc.shape[0]
    if w >= 4:
        pieces = []
        for start in range(0, c, 2 * w):
            r = start + w - 1 if fwd else start + w
            pieces.append(jnp.broadcast_to(bc[r:r + 1, :], (2 * w, LANES)))
        return pieces[0] if len(pieces) == 1 else jnp.concatenate(pieces, axis=0)
    pos = lax.broadcasted_iota(jnp.int32, (c, LANES), 0) % (2 * w)
    target = w - 1 if fwd else w
    out = bc
    for p in range(2 * w):
        if p != target:
            out = jnp.where(pos == p, _shift_rows(bc, target - p), out)
    return out


def _hgrn_levels():
    w = HGRN_CHUNK // 2
    while w >= 1:
        yield w
        w //= 2


def _hgrn_constants():
    idx = np.arange(HGRN_CHUNK)
    t, s = idx[:, None], idx[None, :]
    tri = np.stack([t >= s, t <= s]).astype(np.float32)
    pair = []
    for w in _hgrn_levels():
        same = (t // (2 * w)) == (s // (2 * w))
        upper_t, upper_s = (t % (2 * w)) >= w, (s % (2 * w)) >= w
        pair.append(np.stack([same & upper_t & ~upper_s, same & ~upper_t & upper_s]))
    return jnp.asarray(tri, BF16), jnp.asarray(np.stack(pair, axis=1).astype(np.float32))


def _hgrn_tiles(q, k, v, logf, st, tri, pair_ref):
    c = q.shape[1]
    bc = None
    rem = logf
    for _ in range(3):
        part = rem.astype(BF16)
        rem = rem - part.astype(F32)
        r = jnp.einsum('bij,bjk->bik', tri, part, preferred_element_type=F32)
        bc = r if bc is None else bc + r
    nb = q.shape[0]
    q_lo, k_lo = q.astype(BF16), k.astype(BF16)
    amat = None
    for lvl, w in enumerate(_hgrn_levels()):
        bm = jnp.stack([_boundary_rows(bc[i], w, i % 2 == 0) for i in range(nb)], axis=0)
        e = jnp.exp2(jnp.abs(bc - bm) * (-LOG2E)).astype(BF16)
        pair = jnp.concatenate([pair_ref[:, lvl]] * (nb // 2), axis=0)
        g = _bdot_t(q_lo * e, k_lo * e) * pair
        amat = g if amat is None else amat + g
    o = _bdot(amat, v) + jnp.sum(q * k, axis=-1, keepdims=True) * v
    tot = jnp.stack([bc[i, c - 1:c, :] if i % 2 == 0 else bc[i, 0:1, :] for i in range(nb)], axis=0)
    inc = _bdot_tl(v, k * jnp.exp(tot - bc))
    decay = jnp.exp(tot)
    states = [st]
    for j in range(nb // 2):
        states.append(states[-1] * decay[2 * j:2 * j + 2] + inc[2 * j:2 * j + 2])
    o = o + _bdot_t(q * jnp.exp(bc), jnp.concatenate(states[:-1], axis=0))
    return o, states[-1]


def _hgrn_kernel(q_ref, ff_ref, fr_ref, i_ref, g_ref, lbf_ref, lbr_ref, ng_ref, s0_ref, tri_ref, pair_ref,
                 o_ref, sfin_ref, of_scr, or_scr):
    t = q_ref.shape[0]
    n_tiles = t // HGRN_CHUNK
    unroll = min(HGRN_UNROLL, n_tiles)
    tri = jnp.concatenate([tri_ref[...]] * unroll, axis=0)

    def load(sl, f_ref, lb_ref):
        lb = lb_ref[...]
        f = jnp.maximum(lb + (1.0 - lb) * jax.nn.sigmoid(f_ref[sl, :].astype(F32)), F_FLOOR)
        return _silu(q_ref[sl, :].astype(F32)), 1.0 - f, i_ref[sl, :].astype(F32), jnp.log(f)

    def body(i, st):
        slices, loaded = [], []
        for j in range(unroll):
            n = i * unroll + j
            slf = pl.ds(pl.multiple_of(n * HGRN_CHUNK, HGRN_CHUNK), HGRN_CHUNK)
            slr = pl.ds(pl.multiple_of((n_tiles - 1 - n) * HGRN_CHUNK, HGRN_CHUNK), HGRN_CHUNK)
            slices += [slf, slr]
            loaded += [load(slf, ff_ref, lbf_ref), load(slr, fr_ref, lbr_ref)]
        o, st = _hgrn_tiles(*[jnp.stack(parts, axis=0) for parts in zip(*loaded)], st, tri, pair_ref)
        for idx, sl in enumerate(slices):
            (of_scr if idx % 2 == 0 else or_scr)[sl, :] = o[idx]
        return st

    st = lax.fori_loop(0, n_tiles // unroll, body,
                       jnp.stack([s0_ref[0, 0, 0, 0].T, s0_ref[0, 0, 1, 0].T], axis=0))
    sfin_ref[0, 0, 0] = st[0].T
    sfin_ref[0, 1, 0] = st[1].T
    o = of_scr[...] + or_scr[...]
    o_ref[...] = (_rms(o, ng_ref[...]) * _silu(g_ref[...].astype(F32))).astype(o_ref.dtype)


def _hgrn(y, lb, norm_g, s0, layer, bsz, t):
    col = lambda cb: pl.BlockSpec((t, LANES), lambda b, h: (b, cb + h))
    tri, pair = _hgrn_constants()
    return pl.pallas_call(
        _hgrn_kernel,
        grid=(bsz, H_C),
        in_specs=[col(CB_CQ), col(CB_CFF), col(CB_CFB), col(CB_CI), col(CB_CG),
                  pl.BlockSpec((1, LANES), lambda b, h: (0, h)),
                  pl.BlockSpec((1, LANES), lambda b, h: (0, H_C + h)),
                  pl.BlockSpec((1, LANES), lambda b, h: (0, 0)),
                  pl.BlockSpec((1, 1, 2, 1, DK_C, DV_C), lambda b, h: (b, layer, 0, h, 0, 0)),
                  pl.BlockSpec(tri.shape, lambda b, h: (0, 0, 0)),
                  pl.BlockSpec(pair.shape, lambda b, h: (0, 0, 0, 0))],
        out_specs=[pl.BlockSpec((t, LANES), lambda b, h: (b, h)),
                   pl.BlockSpec((1, 2, 1, DK_C, DV_C), lambda b, h: (b, 0, h, 0, 0))],
        out_shape=[jax.ShapeDtypeStruct((bsz * t, BRANCH_W), BF16),
                   jax.ShapeDtypeStruct((bsz, 2, H_C, DK_C, DV_C), F32)],
        scratch_shapes=[pltpu.VMEM((t, LANES), F32), pltpu.VMEM((t, LANES), F32)],
        compiler_params=_cparams("parallel", "parallel"),
        name="hgrn",
    )(y, y, y, y, y, lb, lb, norm_g, s0, tri, pair)


def _diff_lambda(lam_ref, lam_init):
    lp = lam_ref[...]
    return (jnp.exp(jnp.sum(lp[0:1] * lp[1:2], axis=-1, keepdims=True))
            - jnp.exp(jnp.sum(lp[2:3] * lp[3:4], axis=-1, keepdims=True)) + lam_init)


def _diff_core(problems, lam, ng, lam_init):
    scores = []
    for q, k, _ in problems:
        lane = lax.broadcasted_iota(jnp.int32, q.shape, 1)
        zero = jnp.zeros_like(q)
        scores.append((_dot_t(jnp.where(lane < DK_B, q, zero), k), _dot_t(jnp.where(lane >= DK_B, q, zero), k)))
    outs = []
    for (s1, s2), (_, _, v) in zip(scores, problems):
        e1, r1 = _softmax_parts(s1)
        e2, r2 = _softmax_parts(s2)
        outs.append(_dot(e1, v) * r1 - _dot(e2, v) * (lam * r2))
    return [_rms(o, ng) * (1.0 - lam_init) for o in outs]


def _diff_ctx_kernel(lam_ref, q_ref, k_ref, v_ref, ng_ref, o_ref, *, lam_init):
    lam = _diff_lambda(lam_ref, lam_init)
    heads = [slice(h * LANES, (h + 1) * LANES) for h in range(H_B)]
    outs = _diff_core([((q_ref[:, sl].astype(F32) * DIFF_QSCALE).astype(BF16), k_ref[:, sl], v_ref[:, sl])
                       for sl in heads], lam, ng_ref[...], lam_init)
    for sl, o in zip(heads, outs):
        o_ref[:, sl] = o.astype(o_ref.dtype)


def _diff_ctx(y, lam_p, norm_g, lam_init, bsz, t):
    blk = lambda cb: pl.BlockSpec((t, BRANCH_W), lambda b: (b, cb // H_B))
    return pl.pallas_call(
        functools.partial(_diff_ctx_kernel, lam_init=lam_init),
        grid=(bsz,),
        in_specs=[pl.BlockSpec((4, DK_B), lambda b: (0, 0)), blk(CB_BQ), blk(CB_BK), blk(CB_BV),
                  pl.BlockSpec((1, LANES), lambda b: (0, 0))],
        out_specs=pl.BlockSpec((t, BRANCH_W), lambda b: (b, 0)),
        out_shape=jax.ShapeDtypeStruct((bsz * t, BRANCH_W), BF16),
        compiler_params=_cparams("parallel"),
        name="diff_ctx",
    )(lam_p, y, y, y, norm_g)


def _rope(x, cos, sin_signed):
    lane = lax.broadcasted_iota(jnp.int32, x.shape, 1)
    rot = jnp.where((lane % DK_B) < DK_B // 2, pltpu.roll(x, LANES - DK_B // 2, 1), pltpu.roll(x, DK_B // 2, 1))
    return x * cos + rot * sin_signed


def _diff_lat_kernel(lam_ref, q_ref, k_ref, v_ref, ck_ref, cv_ref, cosq_ref, sinq_ref, cos_ref, sin_ref, ng_ref,
                     o_ref, k_scr, v_scr, *, lam_init):
    past = ck_ref.shape[3]

    @pl.when(pl.program_id(2) == 0)
    def _():
        k_scr[:past, :] = ck_ref[0, 0, 0].astype(BF16)
        v_scr[:past, :] = cv_ref[0, 0, 0].astype(BF16)
        k_scr[past:, :] = _rope(k_ref[...].astype(F32), cos_ref[...], sin_ref[...]).astype(BF16)
        v_scr[past:, :] = v_ref[...]

    lam = _diff_lambda(lam_ref, lam_init)
    q = (_rope(q_ref[...].astype(F32), cosq_ref[...], sinq_ref[...]) * DIFF_QSCALE).astype(BF16)
    k, v = k_scr[...], v_scr[...]
    n_part = q.shape[0] // DIFF_QROWS
    rows = [slice(i * DIFF_QROWS, (i + 1) * DIFF_QROWS) for i in range(n_part)]
    outs = _diff_core([(q[sl], k, v) for sl in rows], lam, ng_ref[...], lam_init)
    for sl, o in zip(rows, outs):
        o_ref[sl, :] = o.astype(o_ref.dtype)


def _diff_lat(y, ck, cv, cos, sin_signed, lam_p, norm_g, lam_init, layer, bsz, t):
    tq = 1024
    nq = t // tq
    past = ck.shape[3]
    full = lambda cb: pl.BlockSpec((t, LANES), lambda b, h, i: (b, cb + h))
    cache = lambda: pl.BlockSpec((1, 1, 1, past, LANES), lambda b, h, i: (b, layer, h, 0, 0))
    tab_q = lambda: pl.BlockSpec((tq, LANES), lambda b, h, i: (i, 0))
    tab = lambda: pl.BlockSpec((t, LANES), lambda b, h, i: (0, 0))
    return pl.pallas_call(
        functools.partial(_diff_lat_kernel, lam_init=lam_init),
        grid=(bsz, H_B, nq),
        in_specs=[pl.BlockSpec((4, DK_B), lambda b, h, i: (0, 0)),
                  pl.BlockSpec((tq, LANES), lambda b, h, i: (b * nq + i, CB_BQ + h)),
                  full(CB_BK), full(CB_BV), cache(), cache(), tab_q(), tab_q(), tab(), tab(),
                  pl.BlockSpec((1, LANES), lambda b, h, i: (0, 0))],
        out_specs=pl.BlockSpec((tq, LANES), lambda b, h, i: (b * nq + i, h)),
        out_shape=jax.ShapeDtypeStruct((bsz * t, BRANCH_W), BF16),
        scratch_shapes=[pltpu.VMEM((past + t, LANES), BF16), pltpu.VMEM((past + t, LANES), BF16)],
        compiler_params=_cparams("parallel", "parallel", "arbitrary"),
        name="diff_lat",
    )(lam_p, y, y, y, ck, cv, cos, sin_signed, cos, sin_signed, norm_g)


def _pair_attend(q, keys, vals, bias=None):
    lane = lax.broadcasted_iota(jnp.int32, q.shape, 1)
    zero = jnp.zeros_like(q)
    scores = [[_dot_t(jnp.where((lane < DH_D) == (par == 0), q, zero), kk) for kk in keys] for par in range(2)]
    outs = []
    for par in range(2):
        s = scores[par]
        if bias is not None:
            s[-1] = jnp.where(bias[par] > 0.5 * MASK_VALUE, s[-1] + bias[par], MASK_VALUE)
        m = s[0].max(axis=-1, keepdims=True)
        for x in s[1:]:
            m = jnp.maximum(m, x.max(axis=-1, keepdims=True))
        e = [jnp.exp2(x - m) for x in s]
        den = e[0].sum(axis=-1, keepdims=True)
        for x in e[1:]:
            den = den + x.sum(axis=-1, keepdims=True)
        o = _dot(e[0], vals[0])
        for x, vv in zip(e[1:], vals[1:]):
            o = o + _dot(x, vv)
        outs.append(o * (1.0 / den))
    return jnp.where(lane < DH_D, outs[0], outs[1])


def _na_ctx_kernel(q_ref, k_ref, v_ref, o_ref):
    for p in range(H_D // 2):
        sl = slice(p * LANES, (p + 1) * LANES)
        q = (q_ref[:, sl].astype(F32) * NA_QSCALE).astype(BF16)
        o_ref[:, sl] = _pair_attend(q, [k_ref[:, sl]], [v_ref[:, sl]]).astype(o_ref.dtype)


def _na_ctx(y, bsz, t):
    blk = lambda cb: pl.BlockSpec((t, BRANCH_W), lambda b: (b, cb // 4))
    return pl.pallas_call(
        _na_ctx_kernel,
        grid=(bsz,),
        in_specs=[blk(CB_DQ), blk(CB_DK), blk(CB_DV)],
        out_specs=pl.BlockSpec((t, BRANCH_W), lambda b: (b, 0)),
        out_shape=jax.ShapeDtypeStruct((bsz * t, BRANCH_W), BF16),
        compiler_params=_cparams("parallel"),
        name="na_ctx",
    )(y, y, y)


def _na_bias_kernel(rpb_ref, o_ref):
    hd = pl.program_id(0)
    qc = lax.broadcasted_iota(jnp.int32, (GRID_W, LANES), 0)
    lane = lax.broadcasted_iota(jnp.int32, (GRID_W, LANES), 1)
    kc = lane % GRID_W
    dc = jnp.clip(kc - qc + WIN_C - 1, 0, 2 * WIN_C - 2)
    c0 = jnp.clip(qc - WIN_C // 2, 0, GRID_W - WIN_C)
    col_ok = (kc >= c0) & (kc < c0 + WIN_C)
    n_dr = 2 * WIN_R - 1
    tables = []
    for dr in range(n_dr):
        acc = jnp.zeros((GRID_W, LANES), F32)
        for d in range(2 * WIN_C - 1):
            acc = jnp.where(dc == d, rpb_ref[hd, dr * (2 * WIN_C - 1) + d], acc)
        tables.append(jnp.where(col_ok, acc * LOG2E, MASK_VALUE))
    masked = jnp.full((GRID_W, LANES), MASK_VALUE, F32)
    for i in range(n_dr + 1):
        lo = tables[i - 1] if i >= 1 else masked
        hi = tables[i] if i < n_dr else masked
        o_ref[0, i] = jnp.where(lane < GRID_W, lo, hi)


def _na_bias(rpb):
    n_dr = 2 * WIN_R - 1
    return pl.pallas_call(
        _na_bias_kernel,
        grid=(H_D,),
        in_specs=[pl.BlockSpec(memory_space=pltpu.SMEM)],
        out_specs=pl.BlockSpec((1, n_dr + 1, GRID_W, LANES), lambda h: (h, 0, 0, 0)),
        out_shape=jax.ShapeDtypeStruct((H_D, n_dr + 1, GRID_W, LANES), F32),
        compiler_params=_cparams("parallel"),
        name="na_bias",
    )(rpb.reshape(H_D, n_dr * (2 * WIN_C - 1)))


def _na_lat_kernel(q_ref, k_ref, v_ref, ck_ref, cv_ref, tab_ref, o_ref, *, rows):
    wr = min(WIN_R, rows)
    lane = lax.broadcasted_iota(jnp.int32, (GRID_W, LANES), 1)
    masked = jnp.full((GRID_W, LANES), MASK_VALUE, F32)
    ck, cv = ck_ref[0, 0, 0], cv_ref[0, 0, 0]
    first_row = lambda r: min(max(r - wr // 2, 0), rows - wr)
    for g in range(rows // NA_QROWS):
        starts = [first_row(g * NA_QROWS + qi) for qi in range(NA_QROWS)]
        kp0, kp1 = min(starts) // 2, (max(starts) + wr + 1) // 2
        k_lat = k_ref[2 * kp0 * GRID_W:2 * kp1 * GRID_W, :]
        v_lat = v_ref[2 * kp0 * GRID_W:2 * kp1 * GRID_W, :]
        bias = []
        for par in range(2):
            blocks = []
            for qi in range(NA_QROWS):
                r = g * NA_QROWS + qi
                r0 = first_row(r)
                tiles = []
                for kp in range(kp0, kp1):
                    ok0 = r0 <= 2 * kp < r0 + wr
                    ok1 = r0 <= 2 * kp + 1 < r0 + wr
                    if not (ok0 or ok1):
                        tiles.append(masked)
                        continue
                    tile = tab_ref[par, 2 * kp - r + WIN_R]
                    if not ok0:
                        tile = jnp.where(lane >= GRID_W, tile, MASK_VALUE)
                    if not ok1:
                        tile = jnp.where(lane < GRID_W, tile, MASK_VALUE)
                    tiles.append(tile)
                blocks.append(jnp.concatenate(tiles, axis=1))
            bias.append(jnp.concatenate(blocks, axis=0))
        sl = slice(g * NA_QROWS * GRID_W, (g + 1) * NA_QROWS * GRID_W)
        q = (q_ref[sl, :].astype(F32) * NA_QSCALE).astype(BF16)
        o_ref[sl, :] = _pair_attend(q, [ck, k_lat], [cv, v_lat], bias).astype(o_ref.dtype)


def _na_lat(y, ck, cv, tables, layer, bsz, t):
    rows = t // GRID_W
    past = ck.shape[3]
    n_tab = tables.shape[1]
    col = lambda cb: pl.BlockSpec((t, LANES), lambda b, p: (b, cb + p))
    cache = lambda: pl.BlockSpec((1, 1, 1, past, LANES), lambda b, p: (b, layer, p, 0, 0))
    return pl.pallas_call(
        functools.partial(_na_lat_kernel, rows=rows),
        grid=(bsz, H_D // 2),
        in_specs=[col(CB_DQ), col(CB_DK), col(CB_DV), cache(), cache(),
                  pl.BlockSpec((2, n_tab, GRID_W, LANES), lambda b, p: (p, 0, 0, 0))],
        out_specs=pl.BlockSpec((t, LANES), lambda b, p: (b, p)),
        out_shape=jax.ShapeDtypeStruct((bsz * t, BRANCH_W), BF16),
        compiler_params=_cparams("parallel", "parallel"),
        name="na_lat",
    )(y, y, y, ck, cv, tables)


def _axial_rope_tables(n_tok):
    tok = jnp.arange(n_tok)
    n_freq = DK_B // 4
    inv = ROPE_BASE ** (-jnp.arange(n_freq, dtype=F32) / n_freq)
    ang = jnp.concatenate([(tok // GRID_W).astype(F32)[:, None] * inv,
                           (tok % GRID_W).astype(F32)[:, None] * inv], axis=-1)
    cos, sin = jnp.cos(ang), jnp.sin(ang)
    return jnp.tile(cos, (1, 4)), jnp.tile(jnp.concatenate([-sin, sin], axis=-1), (1, 2))


def _permute_w_in(w):
    a_end = 4 * BRANCH_W
    g_end = a_end + 4 * H_A
    mg_start = w.shape[-1] - N_BRANCH * D_MODEL
    main = jnp.concatenate([w[..., mg_start:], w[..., :a_end], w[..., g_end:mg_start]], axis=-1)
    gate = jnp.pad(w[..., a_end:g_end], ((0, 0), (0, 0), (0, GATE_COLS - 4 * H_A)))
    return main.astype(BF16), gate.astype(BF16)


def _ctx_cache_kernel(*refs):
    ins, (bk_out, bv_out, dk_out, dv_out) = refs[:-4], refs[-4:]
    for layer in range(DEPTH):
        bk, bv, dk, dv = ins[4 * layer:4 * layer + 4]

        @pl.when(pl.program_id(1) == layer)
        def _():
            for h in range(H_B):
                sl = slice(h * LANES, (h + 1) * LANES)
                bk_out[0, 0, h] = bk[:, sl].astype(F32)
                bv_out[0, 0, h] = bv[:, sl].astype(F32)
            for h in range(H_D):
                sl = slice(h * DH_D, (h + 1) * DH_D)
                dk_out[0, 0, h] = dk[:, sl].astype(F32)
                dv_out[0, 0, h] = dv[:, sl].astype(F32)


def _ctx_caches(ys, bsz, t):
    blk = lambda cb: pl.BlockSpec((t, BRANCH_W), lambda b, l: (b, cb // 4))
    out = lambda heads, width: pl.BlockSpec((1, 1, heads, t, width), lambda b, l: (b, l, 0, 0, 0))
    shape = lambda heads, width: jax.ShapeDtypeStruct((bsz, DEPTH, heads, t, width), F32)
    return pl.pallas_call(
        _ctx_cache_kernel,
        grid=(bsz, DEPTH),
        in_specs=[blk(cb) for _ in ys for cb in (CB_BK, CB_BV, CB_DK, CB_DV)],
        out_specs=[out(H_B, 2 * DK_B), out(H_B, DV_B), out(H_D, DH_D), out(H_D, DH_D)],
        out_shape=[shape(H_B, 2 * DK_B), shape(H_B, DV_B), shape(H_D, DH_D), shape(H_D, DH_D)],
        compiler_params=_cparams("parallel", "parallel"),
        name="ctx_caches",
    )(*[y for y in ys for _ in range(4)])


def kernel(x_prompt, x_sample, c, state_gdn, cache_diff_k, cache_diff_v, state_hgrn, cache_na_k, cache_na_v, c_ctx,
           w_ada, b_ada, norm1_g, w_in, gdn_conv_w, gdn_A_log, gdn_dt_bias, gdn_norm_g, diff_lambda, diff_norm_g,
           hgrn_lb_logits, hgrn_norm_g, na_rpb, w_branch, w_out, norm2_g, w_ffn_gate, w_ffn_up, w_ffn_down,
           final_norm_g):
    bp, tp, _ = x_prompt.shape
    bs, ts, _ = x_sample.shape

    cond = jnp.zeros((16, D_MODEL), F32).at[:bs].set(c).at[bs].set(c_ctx)
    mods = _adaln(cond, w_ada, b_ada)
    probs = jax.nn.softmax(hgrn_lb_logits.astype(F32), axis=0)
    lb_all = jnp.cumsum(probs, axis=0) - probs[0:1]
    cos, sin_signed = _axial_rope_tables(ts)
    past = cache_na_k.shape[3]
    pair = lambda a: a.reshape(bs, DEPTH, H_D // 2, 2, past, DH_D).transpose(0, 1, 2, 4, 3, 5).reshape(
        bs, DEPTH, H_D // 2, past, LANES).astype(BF16)
    na_ck, na_cv = pair(cache_na_k), pair(cache_na_v)
    zeros_gdn = jnp.zeros((bp, 1, 2, H_A, DK_A, DV_A), F32)
    zeros_hgrn = jnp.zeros((bp, 1, 2, H_C, DK_C, DV_C), F32)
    w_main, w_gate = _permute_w_in(w_in)
    wbr, wo = w_branch.astype(BF16), w_out.astype(BF16)
    wfg, wfu, wfd = w_ffn_gate.astype(BF16), w_ffn_up.astype(BF16), w_ffn_down.astype(BF16)

    xp = x_prompt.reshape(bp * tp, D_MODEL)
    xs = x_sample.reshape(bs * ts, D_MODEL)
    new_ctx = []
    for l in range(DEPTH):
        lam_init = 0.8 - 0.6 * math.exp(-0.3 * l)
        conv_w = jnp.zeros((8, 3 * BRANCH_W), F32).at[:CONV_W].set(gdn_conv_w[l].T)
        lanes = lambda p: jnp.pad(p.astype(F32).reshape(1, 2 * H_A), ((0, 0), (0, LANES - 2 * H_A)))
        alog, dtb = lanes(gdn_A_log[l]), lanes(gdn_dt_bias[l])
        lb = lb_all[l].reshape(1, 2 * H_C * DK_C)
        vecs = dict(n1=norm1_g[l].reshape(1, D_MODEL), n2=norm2_g[l].reshape(1, D_MODEL),
                    fin=final_norm_g.reshape(1, D_MODEL), gdn=gdn_norm_g[l].reshape(1, LANES),
                    diff=diff_norm_g[l].reshape(1, LANES), hgrn=hgrn_norm_g[l].reshape(1, LANES))
        tables = _na_bias(na_rpb[l])
        final = l == DEPTH - 1

        def dense_tail(x, mod, y, branches, tok_per_row):
            x = _merge(x, mod, y, branches, wbr, wo, l, tok_per_row)
            return _ffn(x, mod, vecs["n2"], vecs["fin"], wfg, wfu, wfd, l, tok_per_row, final)

        mod = mods[l, bs:bs + 1].reshape(1, 1, 6 * D_MODEL)
        y, gates = _inproj(xp, mod, vecs["n1"], w_main, w_gate, l, bp * tp)
        o_a, st_gdn = _gdn(y, gates, conv_w, alog, dtb, vecs["gdn"], zeros_gdn, 0, bp, tp)
        o_b = _diff_ctx(y, diff_lambda[l], vecs["diff"], lam_init, bp, tp)
        o_c, st_hgrn = _hgrn(y, lb, vecs["hgrn"], zeros_hgrn, 0, bp, tp)
        o_d = _na_ctx(y, bp, tp)
        xp = dense_tail(xp, mod, y, (o_a, o_b, o_c, o_d), bp * tp)
        new_ctx.append((st_gdn, st_hgrn, y))

        mod = mods[l, :bs].reshape(bs, 1, 6 * D_MODEL)
        y, gates = _inproj(xs, mod, vecs["n1"], w_main, w_gate, l, ts)
        o_a, _ = _gdn(y, gates, conv_w, alog, dtb, vecs["gdn"], state_gdn, l, bs, ts)
        o_b = _diff_lat(y, cache_diff_k, cache_diff_v, cos, sin_signed, diff_lambda[l], vecs["diff"], lam_init,
                        l, bs, ts)
        o_c, _ = _hgrn(y, lb, vecs["hgrn"], state_hgrn, l, bs, ts)
        o_d = _na_lat(y, na_ck, na_cv, tables, l, bs, ts)
        xs = dense_tail(xs, mod, y, (o_a, o_b, o_c, o_d), ts)

    stack = lambda i: jnp.stack([n[i] for n in new_ctx], axis=1)
    diff_k, diff_v, na_k, na_v = _ctx_caches([n[2] for n in new_ctx], bp, tp)
    return (xp.reshape(bp, tp, D_MODEL), xs.reshape(bs, ts, D_MODEL), stack(0), diff_k, diff_v, stack(1),
            na_k, na_v)
```
